```python
import math
import jax, jax.numpy as jnp
from jax import lax
import numpy as np

D_MODEL = 2048
BATCH = 8
SEQ = 4096
DEPTH = 4

N_MIXERS = 2
SSD_EXPAND = 2
D_INNER = SSD_EXPAND * D_MODEL
HEAD_DIM = 64
N_HEADS = D_INNER // HEAD_DIM
N_GROUPS = 8
HEADS_PER_GROUP = N_HEADS // N_GROUPS
D_STATE = 128
CONV_WIDTH = 5
CHUNK = 128
D_BC = N_GROUPS * D_STATE
D_XBC = D_INNER + 2 * D_BC
D_IN_PROJ = D_INNER + D_XBC + 2 * N_HEADS
DT_MIN = 1e-3
DT_MAX = 1e-1
A_INIT_MIN = 1.0
A_INIT_MAX = 16.0
POOL_WINDOWS = (2, 4, 8, 16)
N_POOL_GROUPS = len(POOL_WINDOWS)
POOL_GROUP_DIM = D_MODEL // N_POOL_GROUPS
D_FF = 4 * D_MODEL
DEEPNORM_ALPHA = (2.0 * DEPTH) ** 0.25
DEEPNORM_BETA = (8.0 * DEPTH) ** -0.25
LN_EPS = 1e-5
RMS_EPS = 1e-5
N_SSD_LAYERS = (DEPTH + 1) // 2
N_POOL_LAYERS = DEPTH // 2

kernel_name = "bidir_ssd_pool_hybrid_deepnorm"


def layer_norm(x, g, b):
    xf = x.astype(jnp.float32)
    mu = jnp.mean(xf, axis=-1, keepdims=True)
    var = jnp.mean(jnp.square(xf - mu), axis=-1, keepdims=True)
    y = (xf - mu) * lax.rsqrt(var + LN_EPS) * g.astype(jnp.float32) + b.astype(jnp.float32)
    return y.astype(x.dtype)


def centred_depthwise_conv(x, w, bias):
    c = x.shape[-1]
    pad = CONV_WIDTH // 2
    y = lax.conv_general_dilated(
        x, w, window_strides=(1,), padding=[(pad, pad)],
        dimension_numbers=("NWC", "WIO", "NWC"), feature_group_count=c)
    return y + bias


def ssd_chunked(x, dt, A, Bm, Cm):
    b, T, g, r, p = x.shape
    n = Bm.shape[-1]
    c = T // CHUNK
    f32 = jnp.float32
    xdt = (x.astype(f32) * dt[..., None]).reshape(b, c, CHUNK, g, r, p)
    a_cs = jnp.cumsum((dt * A).reshape(b, c, CHUNK, g, r), axis=2)
    Bc = Bm.astype(f32).reshape(b, c, CHUNK, g, n)
    Cc = Cm.astype(f32).reshape(b, c, CHUNK, g, n)
    mask = jnp.tril(jnp.ones((CHUNK, CHUNK), dtype=bool))[None, None, :, :, None, None]
    seg = a_cs[:, :, :, None] - a_cs[:, :, None, :]
    decay = jnp.exp(jnp.where(mask, seg, -jnp.inf))
    scores = jnp.einsum("bclgn,bcsgn->bclsg", Cc, Bc)
    y_diag = jnp.einsum("bclsg,bclsgr,bcsgrp->bclgrp", scores, decay, xdt)
    decay_to_end = jnp.exp(a_cs[:, :, -1:] - a_cs)
    states = jnp.einsum("bclgn,bclgr,bclgrp->bcgrpn", Bc, decay_to_end, xdt)
    chunk_decay = jnp.exp(a_cs[:, :, -1])

    def step(h, inp):
        s_c, d_c = inp
        return h * d_c[..., None, None] + s_c, h

    h0 = jnp.zeros((b, g, r, p, n), f32)
    _, prev = lax.scan(step, h0, (jnp.moveaxis(states, 1, 0), jnp.moveaxis(chunk_decay, 1, 0)))
    prev = jnp.moveaxis(prev, 0, 1)
    y_off = jnp.einsum("bclgn,bcgrpn,bclgr->bclgrp", Cc, prev, jnp.exp(a_cs))
    return (y_diag + y_off).reshape(b, T, g, r, p)


def ssd_mixer(u, in_proj, conv_w, conv_b, dt_bias, a_log, d_skip, norm_w, out_proj):
    b, T, _ = u.shape
    f32 = jnp.float32
    zxbcdt = u @ in_proj
    z = zxbcdt[..., :D_INNER]
    xbc = zxbcdt[..., D_INNER:D_INNER + D_XBC]
    dt_raw = zxbcdt[..., D_INNER + D_XBC:]
    xbc = jax.nn.silu(centred_depthwise_conv(xbc, conv_w, conv_b))
    xs = xbc[..., :D_INNER].reshape(b, T, N_GROUPS, HEADS_PER_GROUP, HEAD_DIM)
    Bm = xbc[..., D_INNER:D_INNER + D_BC].reshape(b, T, N_GROUPS, D_STATE)
    Cm = xbc[..., D_INNER + D_BC:].reshape(b, T, N_GROUPS, D_STATE)
    dt = jax.nn.softplus(dt_raw.astype(f32).reshape(b, T, 2, N_GROUPS, HEADS_PER_GROUP)
                         + dt_bias.astype(f32).reshape(2, N_GROUPS, HEADS_PER_GROUP))
    A = -jnp.exp(a_log.astype(f32)).reshape(2, N_GROUPS, HEADS_PER_GROUP)
    flip = lambda t: jnp.flip(t, axis=1)
    y_fwd = ssd_chunked(xs, dt[:, :, 0], A[0], Bm, Cm)
    y_bwd = flip(ssd_chunked(flip(xs), flip(dt[:, :, 1]), A[1], flip(Bm), flip(Cm)))
    y = y_fwd + y_bwd + xs.astype(f32) * d_skip.astype(f32).reshape(N_GROUPS, HEADS_PER_GROUP)[..., None]
    gy = (y.reshape(b, T, D_INNER) * jax.nn.silu(z.astype(f32))).reshape(b, T, N_GROUPS, D_INNER // N_GROUPS)
    gy = gy * lax.rsqrt(jnp.mean(jnp.square(gy), axis=-1, keepdims=True) + RMS_EPS)
    gy = gy.reshape(b, T, D_INNER) * norm_w.astype(f32)
    return gy.astype(u.dtype) @ out_proj


def pool_mixer(u, w, bias, scale):
    b, T, _ = u.shape
    uf = u.astype(jnp.float32)
    cs = jnp.concatenate([jnp.zeros((b, 1, D_MODEL), jnp.float32), jnp.cumsum(uf, axis=1)], axis=1)
    t = jnp.arange(T)
    groups = []
    for gi, win in enumerate(POOL_WINDOWS):
        lo_c, hi_c = gi * POOL_GROUP_DIM, (gi + 1) * POOL_GROUP_DIM
        start = t - win // 2
        lo = jnp.clip(start, 0, T)
        hi = jnp.clip(start + win, 0, T)
        csg = cs[..., lo_c:hi_c]
        wsum = jnp.take(csg, hi, axis=1) - jnp.take(csg, lo, axis=1)
        cnt = (hi - lo).astype(jnp.float32)[None, :, None]
        groups.append(wsum / cnt - uf[..., lo_c:hi_c])
    m = jnp.stack(groups, axis=2).astype(u.dtype)
    y = jnp.einsum("btgc,gcd->btgd", m, w) + bias
    return y.reshape(b, T, D_MODEL) * scale


def sq_relu_mlp(u, w1, w2):
    return jnp.square(jax.nn.relu(u @ w1)) @ w2


def _fwd_setup_inputs(seed: int = 0) -> dict:
    key = jax.random.key(seed)
    ks = jax.random.split(key, 20)
    f32 = jnp.float32
    nrm = lambda k, shape: jax.random.normal(k, shape, f32)
    x = nrm(ks[0], (BATCH, SEQ, D_MODEL))
    ssd_in_proj = nrm(ks[1], (N_SSD_LAYERS, D_MODEL, D_IN_PROJ)) * D_MODEL ** -0.5
    ssd_conv_w = nrm(ks[2], (N_SSD_LAYERS, CONV_WIDTH, 1, D_XBC)) * CONV_WIDTH ** -0.5
    ssd_conv_b = 0.01 * nrm(ks[3], (N_SSD_LAYERS, D_XBC))
    u_dt = jax.random.uniform(ks[4], (N_SSD_LAYERS, 2, N_HEADS), f32)
    dt0 = jnp.exp(u_dt * (math.log(DT_MAX) - math.log(DT_MIN)) + math.log(DT_MIN))
    dt0 = jnp.maximum(dt0, 1e-4)
    ssd_dt_bias = dt0 + jnp.log(-jnp.expm1(-dt0))
    ssd_A_log = jnp.log(jax.random.uniform(ks[5], (N_SSD_LAYERS, 2, N_HEADS), f32, A_INIT_MIN, A_INIT_MAX))
    ssd_D = 1.0 + 0.1 * nrm(ks[6], (N_SSD_LAYERS, N_HEADS))
    ssd_norm_w = 1.0 + 0.1 * nrm(ks[7], (N_SSD_LAYERS, D_INNER))
    ssd_out_proj = nrm(ks[8], (N_SSD_LAYERS, D_INNER, D_MODEL)) * (D_INNER ** -0.5 * DEEPNORM_BETA)
    pool_w = nrm(ks[9], (N_POOL_LAYERS, N_POOL_GROUPS, POOL_GROUP_DIM, POOL_GROUP_DIM)) * (POOL_GROUP_DIM ** -0.5 * DEEPNORM_BETA)
    pool_b = 0.01 * nrm(ks[10], (N_POOL_LAYERS, N_POOL_GROUPS, POOL_GROUP_DIM))
    pool_scale = 1.0 + 0.1 * nrm(ks[11], (N_POOL_LAYERS, D_MODEL))
    mlp_w1 = nrm(ks[12], (DEPTH, D_MODEL, D_FF)) * D_MODEL ** -0.5
    mlp_w2 = nrm(ks[13], (DEPTH, D_FF, D_MODEL)) * (D_FF ** -0.5 * DEEPNORM_BETA)
    ln_mix_g = 1.0 + 0.1 * nrm(ks[14], (DEPTH, D_MODEL))
    ln_mix_b = 0.01 * nrm(ks[15], (DEPTH, D_MODEL))
    ln_ffn_g = 1.0 + 0.1 * nrm(ks[16], (DEPTH, D_MODEL))
    ln_ffn_b = 0.01 * nrm(ks[17], (DEPTH, D_MODEL))
    return {"x": x, "ssd_in_proj": ssd_in_proj, "ssd_conv_w": ssd_conv_w, "ssd_conv_b": ssd_conv_b,
            "ssd_dt_bias": ssd_dt_bias, "ssd_A_log": ssd_A_log, "ssd_D": ssd_D, "ssd_norm_w": ssd_norm_w,
            "ssd_out_proj": ssd_out_proj, "pool_w": pool_w, "pool_b": pool_b, "pool_scale": pool_scale,
            "mlp_w1": mlp_w1, "mlp_w2": mlp_w2, "ln_mix_g": ln_mix_g, "ln_mix_b": ln_mix_b,
            "ln_ffn_g": ln_ffn_g, "ln_ffn_b": ln_ffn_b}


def _fwd_reference(x, ssd_in_proj, ssd_conv_w, ssd_conv_b, ssd_dt_bias, ssd_A_log, ssd_D, ssd_norm_w,
              ssd_out_proj, pool_w, pool_b, pool_scale, mlp_w1, mlp_w2, ln_mix_g, ln_mix_b,
              ln_ffn_g, ln_ffn_b):
    for i in range(DEPTH):
        j = i // N_MIXERS
        if i % N_MIXERS == 0:
            mix = ssd_mixer(x, ssd_in_proj[j], ssd_conv_w[j], ssd_conv_b[j], ssd_dt_bias[j],
                            ssd_A_log[j], ssd_D[j], ssd_norm_w[j], ssd_out_proj[j])
        else:
            mix = pool_mixer(x, pool_w[j], pool_b[j], pool_scale[j])
        x = layer_norm(DEEPNORM_ALPHA * x + mix, ln_mix_g[i], ln_mix_b[i])
        x = layer_norm(DEEPNORM_ALPHA * x + sq_relu_mlp(x, mlp_w1[i], mlp_w2[i]), ln_ffn_g[i], ln_ffn_b[i])
    return x


import jax as _jax
import jax.numpy as _jnp

TWIN_FORMAT = 'train_step'
FWD_PARAMS = ['x', 'ssd_in_proj', 'ssd_conv_w', 'ssd_conv_b', 'ssd_dt_bias', 'ssd_A_log', 'ssd_D', 'ssd_norm_w', 'ssd_out_proj', 'pool_w', 'pool_b', 'pool_scale', 'mlp_w1', 'mlp_w2', 'ln_mix_g', 'ln_mix_b', 'ln_ffn_g', 'ln_ffn_b']
TWIN_WEIGHTS = ['ssd_in_proj', 'ssd_conv_w', 'ssd_conv_b', 'ssd_dt_bias', 'ssd_A_log', 'ssd_D', 'ssd_norm_w', 'ssd_out_proj', 'pool_w', 'pool_b', 'pool_scale', 'mlp_w1', 'mlp_w2', 'ln_mix_g', 'ln_mix_b', 'ln_ffn_g', 'ln_ffn_b']
TWIN_DIFF_INPUT = 'x'
TWIN_INPUTS = ['x', 'ssd_in_proj', 'ssd_conv_w', 'ssd_conv_b', 'ssd_dt_bias', 'ssd_A_log', 'ssd_D', 'ssd_norm_w', 'ssd_out_proj', 'pool_w', 'pool_b', 'pool_scale', 'mlp_w1', 'mlp_w2', 'ln_mix_g', 'ln_mix_b', 'ln_ffn_g', 'ln_ffn_b', 'loss_target', 'm_ssd_in_proj', 'm_ssd_conv_w', 'm_ssd_conv_b', 'm_ssd_dt_bias', 'm_ssd_A_log', 'm_ssd_D', 'm_ssd_norm_w', 'm_ssd_out_proj', 'm_pool_w', 'm_pool_b', 'm_pool_scale', 'm_mlp_w1', 'm_mlp_w2', 'm_ln_mix_g', 'm_ln_mix_b', 'm_ln_ffn_g', 'm_ln_ffn_b', 'v_ssd_in_proj', 'v_ssd_conv_w', 'v_ssd_conv_b', 'v_ssd_dt_bias', 'v_ssd_A_log', 'v_ssd_D', 'v_ssd_norm_w', 'v_ssd_out_proj', 'v_pool_w', 'v_pool_b', 'v_pool_scale', 'v_mlp_w1', 'v_mlp_w2', 'v_ln_mix_g', 'v_ln_mix_b', 'v_ln_ffn_g', 'v_ln_ffn_b']
TWIN_OUTPUTS = ['loss', 'grad_x', 'grad_ssd_in_proj', 'grad_ssd_conv_w', 'grad_ssd_conv_b', 'grad_ssd_dt_bias', 'grad_ssd_A_log', 'grad_ssd_D', 'grad_ssd_norm_w', 'grad_ssd_out_proj', 'grad_pool_w', 'grad_pool_b', 'grad_pool_scale', 'grad_mlp_w1', 'grad_mlp_w2', 'grad_ln_mix_g', 'grad_ln_mix_b', 'grad_ln_ffn_g', 'grad_ln_ffn_b', 'delta_ssd_in_proj', 'delta_ssd_conv_w', 'delta_ssd_conv_b', 'delta_ssd_dt_bias', 'delta_ssd_A_log', 'delta_ssd_D', 'delta_ssd_norm_w', 'delta_ssd_out_proj', 'delta_pool_w', 'delta_pool_b', 'delta_pool_scale', 'delta_mlp_w1', 'delta_mlp_w2', 'delta_ln_mix_g', 'delta_ln_mix_b', 'delta_ln_ffn_g', 'delta_ln_ffn_b', 'new_m_ssd_in_proj', 'new_m_ssd_conv_w', 'new_m_ssd_conv_b', 'new_m_ssd_dt_bias', 'new_m_ssd_A_log', 'new_m_ssd_D', 'new_m_ssd_norm_w', 'new_m_ssd_out_proj', 'new_m_pool_w', 'new_m_pool_b', 'new_m_pool_scale', 'new_m_mlp_w1', 'new_m_mlp_w2', 'new_m_ln_mix_g', 'new_m_ln_mix_b', 'new_m_ln_ffn_g', 'new_m_ln_ffn_b', 'new_v_ssd_in_proj', 'new_v_ssd_conv_w', 'new_v_ssd_conv_b', 'new_v_ssd_dt_bias', 'new_v_ssd_A_log', 'new_v_ssd_D', 'new_v_ssd_norm_w', 'new_v_ssd_out_proj', 'new_v_pool_w', 'new_v_pool_b', 'new_v_pool_scale', 'new_v_mlp_w1', 'new_v_mlp_w2', 'new_v_ln_mix_g', 'new_v_ln_mix_b', 'new_v_ln_ffn_g', 'new_v_ln_ffn_b']
TWIN_LEAF_KINDS = {'loss': 'loss', 'grad_x': 'grad_x', 'grad_ssd_in_proj': 'grad_w', 'grad_ssd_conv_w': 'grad_w', 'grad_ssd_conv_b': 'grad_w', 'grad_ssd_dt_bias': 'grad_w', 'grad_ssd_A_log': 'grad_w', 'grad_ssd_D': 'grad_w', 'grad_ssd_norm_w': 'grad_w', 'grad_ssd_out_proj': 'grad_w', 'grad_pool_w': 'grad_w', 'grad_pool_b': 'grad_w', 'grad_pool_scale': 'grad_w', 'grad_mlp_w1': 'grad_w', 'grad_mlp_w2': 'grad_w', 'grad_ln_mix_g': 'grad_w', 'grad_ln_mix_b': 'grad_w', 'grad_ln_ffn_g': 'grad_w', 'grad_ln_ffn_b': 'grad_w', 'delta_ssd_in_proj': 'delta_w', 'delta_ssd_conv_w': 'delta_w', 'delta_ssd_conv_b': 'delta_w', 'delta_ssd_dt_bias': 'delta_w', 'delta_ssd_A_log': 'delta_w', 'delta_ssd_D': 'delta_w', 'delta_ssd_norm_w': 'delta_w', 'delta_ssd_out_proj': 'delta_w', 'delta_pool_w': 'delta_w', 'delta_pool_b': 'delta_w', 'delta_pool_scale': 'delta_w', 'delta_mlp_w1': 'delta_w', 'delta_mlp_w2': 'delta_w', 'delta_ln_mix_g': 'delta_w', 'delta_ln_mix_b': 'delta_w', 'delta_ln_ffn_g': 'delta_w', 'delta_ln_ffn_b': 'delta_w', 'new_m_ssd_in_proj': 'new_m', 'new_m_ssd_conv_w': 'new_m', 'new_m_ssd_conv_b': 'new_m', 'new_m_ssd_dt_bias': 'new_m', 'new_m_ssd_A_log': 'new_m', 'new_m_ssd_D': 'new_m', 'new_m_ssd_norm_w': 'new_m', 'new_m_ssd_out_proj': 'new_m', 'new_m_pool_w': 'new_m', 'new_m_pool_b': 'new_m', 'new_m_pool_scale': 'new_m', 'new_m_mlp_w1': 'new_m', 'new_m_mlp_w2': 'new_m', 'new_m_ln_mix_g': 'new_m', 'new_m_ln_mix_b': 'new_m', 'new_m_ln_ffn_g': 'new_m', 'new_m_ln_ffn_b': 'new_m', 'new_v_ssd_in_proj': 'new_v', 'new_v_ssd_conv_w': 'new_v', 'new_v_ssd_conv_b': 'new_v', 'new_v_ssd_dt_bias': 'new_v', 'new_v_ssd_A_log': 'new_v', 'new_v_ssd_D': 'new_v', 'new_v_ssd_norm_w': 'new_v', 'new_v_ssd_out_proj': 'new_v', 'new_v_pool_w': 'new_v', 'new_v_pool_b': 'new_v', 'new_v_pool_scale': 'new_v', 'new_v_mlp_w1': 'new_v', 'new_v_mlp_w2': 'new_v', 'new_v_ln_mix_g': 'new_v', 'new_v_ln_mix_b': 'new_v', 'new_v_ln_ffn_g': 'new_v', 'new_v_ln_ffn_b': 'new_v'}


def _forward(args):
    return _fwd_reference(*[args[k] for k in FWD_PARAMS])


def _output_shape():
    def fwd():
        inp = _fwd_setup_inputs(0)
        return _fwd_reference(*[inp[k] for k in FWD_PARAMS])
    out = _jax.eval_shape(fwd)
    return out.shape, out.dtype

N_MICROBATCH = 1
ADAM_LR = 0.001
ADAM_B1 = 0.9
ADAM_B2 = 0.999
ADAM_EPS = 1e-08
ADAM_WD = 0.01
ADAM_STEP = 10
PER_EXAMPLE_BATCH_AXIS = {'x': 0, 'loss_target': 0}
SHARED_INPUTS = []
_WEIGHT_DTYPES = {'ssd_in_proj': _jnp.float32, 'ssd_conv_w': _jnp.float32, 'ssd_conv_b': _jnp.float32, 'ssd_dt_bias': _jnp.float32, 'ssd_A_log': _jnp.float32, 'ssd_D': _jnp.float32, 'ssd_norm_w': _jnp.float32, 'ssd_out_proj': _jnp.float32, 'pool_w': _jnp.float32, 'pool_b': _jnp.float32, 'pool_scale': _jnp.float32, 'mlp_w1': _jnp.float32, 'mlp_w2': _jnp.float32, 'ln_mix_g': _jnp.float32, 'ln_mix_b': _jnp.float32, 'ln_ffn_g': _jnp.float32, 'ln_ffn_b': _jnp.float32}
MOMENT_SCALE = {'ssd_in_proj': 1.621056e-02, 'ssd_conv_w': 1.773368e-02, 'ssd_conv_b': 4.696991e-02, 'ssd_dt_bias': 2.733077e-02, 'ssd_A_log': 7.607931e-02, 'ssd_D': 8.344057e-02, 'ssd_norm_w': 2.905463e-02, 'ssd_out_proj': 9.429389e-02, 'pool_w': 6.547015e-02, 'pool_b': 4.088914e-01, 'pool_scale': 1.198942e-01, 'mlp_w1': 1.879688e-02, 'mlp_w2': 1.961283e-01, 'ln_mix_g': 2.698685e+00, 'ln_mix_b': 6.872708e-01, 'ln_ffn_g': 9.079445e+00, 'ln_ffn_b': 1.999578e+00}


def _to_microbatches(a, axis):
    t = _jnp.moveaxis(a, axis, 0)
    t = t.reshape((N_MICROBATCH, t.shape[0] // N_MICROBATCH) + t.shape[1:])
    return _jnp.moveaxis(t, 1, axis + 1)


def setup_inputs(seed: int = 0) -> dict:
    inp = _fwd_setup_inputs(seed)
    key = _jax.random.fold_in(_jax.random.key(seed), 7919)
    shape, _ = _output_shape()
    out = dict(inp)
    out["loss_target"] = _jax.random.normal(_jax.random.fold_in(key, 0), shape, _jnp.float32)
    for i, name in enumerate(TWIN_WEIGHTS):
        w = inp[name].astype(_jnp.float32)
        if MOMENT_SCALE is None:
            s = _jnp.sqrt(_jnp.mean(_jnp.square(w)) + 1e-30)
        else:
            s = MOMENT_SCALE[name]
        km, kv = _jax.random.split(_jax.random.fold_in(key, i + 1))
        out[name] = w
        out["m_" + name] = s * _jax.random.normal(km, w.shape, _jnp.float32)
        out["v_" + name] = (s * s) * _jax.random.uniform(kv, w.shape, _jnp.float32, 0.5, 1.5)
    if N_MICROBATCH > 1:
        for name, axis in PER_EXAMPLE_BATCH_AXIS.items():
            out[name] = _to_microbatches(out[name], axis)
    return {'x': out['x'], 'ssd_in_proj': out['ssd_in_proj'], 'ssd_conv_w': out['ssd_conv_w'], 'ssd_conv_b': out['ssd_conv_b'], 'ssd_dt_bias': out['ssd_dt_bias'], 'ssd_A_log': out['ssd_A_log'], 'ssd_D': out['ssd_D'], 'ssd_norm_w': out['ssd_norm_w'], 'ssd_out_proj': out['ssd_out_proj'], 'pool_w': out['pool_w'], 'pool_b': out['pool_b'], 'pool_scale': out['pool_scale'], 'mlp_w1': out['mlp_w1'], 'mlp_w2': out['mlp_w2'], 'ln_mix_g': out['ln_mix_g'], 'ln_mix_b': out['ln_mix_b'], 'ln_ffn_g': out['ln_ffn_g'], 'ln_ffn_b': out['ln_ffn_b'], 'loss_target': out['loss_target'], 'm_ssd_in_proj': out['m_ssd_in_proj'], 'm_ssd_conv_w': out['m_ssd_conv_w'], 'm_ssd_conv_b': out['m_ssd_conv_b'], 'm_ssd_dt_bias': out['m_ssd_dt_bias'], 'm_ssd_A_log': out['m_ssd_A_log'], 'm_ssd_D': out['m_ssd_D'], 'm_ssd_norm_w': out['m_ssd_norm_w'], 'm_ssd_out_proj': out['m_ssd_out_proj'], 'm_pool_w': out['m_pool_w'], 'm_pool_b': out['m_pool_b'], 'm_pool_scale': out['m_pool_scale'], 'm_mlp_w1': out['m_mlp_w1'], 'm_mlp_w2': out['m_mlp_w2'], 'm_ln_mix_g': out['m_ln_mix_g'], 'm_ln_mix_b': out['m_ln_mix_b'], 'm_ln_ffn_g': out['m_ln_ffn_g'], 'm_ln_ffn_b': out['m_ln_ffn_b'], 'v_ssd_in_proj': out['v_ssd_in_proj'], 'v_ssd_conv_w': out['v_ssd_conv_w'], 'v_ssd_conv_b': out['v_ssd_conv_b'], 'v_ssd_dt_bias': out['v_ssd_dt_bias'], 'v_ssd_A_log': out['v_ssd_A_log'], 'v_ssd_D': out['v_ssd_D'], 'v_ssd_norm_w': out['v_ssd_norm_w'], 'v_ssd_out_proj': out['v_ssd_out_proj'], 'v_pool_w': out['v_pool_w'], 'v_pool_b': out['v_pool_b'], 'v_pool_scale': out['v_pool_scale'], 'v_mlp_w1': out['v_mlp_w1'], 'v_mlp_w2': out['v_mlp_w2'], 'v_ln_mix_g': out['v_ln_mix_g'], 'v_ln_mix_b': out['v_ln_mix_b'], 'v_ln_ffn_g': out['v_ln_ffn_g'], 'v_ln_ffn_b': out['v_ln_ffn_b']}


def _loss(weights, diff, rest, loss_target):
    with _jax.named_scope("forward"):
        args = {**rest, TWIN_DIFF_INPUT: diff, **{k: w.astype(_WEIGHT_DTYPES[k]) for k, w in weights.items()}}
        y = _forward(args)
    with _jax.named_scope("loss_head"):
        err = _jnp.square(y.astype(_jnp.float32) - loss_target)
        return 0.5 * _jnp.sum(_jnp.mean(err, axis=-1)) if err.ndim else 0.5 * err


def _adamw(w, g, m, v):
    m = ADAM_B1 * m + (1.0 - ADAM_B1) * g
    v = ADAM_B2 * v + (1.0 - ADAM_B2) * _jnp.square(g)
    m_hat = m / (1.0 - ADAM_B1 ** ADAM_STEP)
    v_hat = v / (1.0 - ADAM_B2 ** ADAM_STEP)
    delta = -ADAM_LR * (m_hat / (_jnp.sqrt(v_hat) + ADAM_EPS) + ADAM_WD * w)
    return delta, m, v


def reference(x, ssd_in_proj, ssd_conv_w, ssd_conv_b, ssd_dt_bias, ssd_A_log, ssd_D, ssd_norm_w, ssd_out_proj, pool_w, pool_b, pool_scale, mlp_w1, mlp_w2, ln_mix_g, ln_mix_b, ln_ffn_g, ln_ffn_b, loss_target, m_ssd_in_proj, m_ssd_conv_w, m_ssd_conv_b, m_ssd_dt_bias, m_ssd_A_log, m_ssd_D, m_ssd_norm_w, m_ssd_out_proj, m_pool_w, m_pool_b, m_pool_scale, m_mlp_w1, m_mlp_w2, m_ln_mix_g, m_ln_mix_b, m_ln_ffn_g, m_ln_ffn_b, v_ssd_in_proj, v_ssd_conv_w, v_ssd_conv_b, v_ssd_dt_bias, v_ssd_A_log, v_ssd_D, v_ssd_norm_w, v_ssd_out_proj, v_pool_w, v_pool_b, v_pool_scale, v_mlp_w1, v_mlp_w2, v_ln_mix_g, v_ln_mix_b, v_ln_ffn_g, v_ln_ffn_b):
    given = dict(x=x, ssd_in_proj=ssd_in_proj, ssd_conv_w=ssd_conv_w, ssd_conv_b=ssd_conv_b, ssd_dt_bias=ssd_dt_bias, ssd_A_log=ssd_A_log, ssd_D=ssd_D, ssd_norm_w=ssd_norm_w, ssd_out_proj=ssd_out_proj, pool_w=pool_w, pool_b=pool_b, pool_scale=pool_scale, mlp_w1=mlp_w1, mlp_w2=mlp_w2, ln_mix_g=ln_mix_g, ln_mix_b=ln_mix_b, ln_ffn_g=ln_ffn_g, ln_ffn_b=ln_ffn_b, loss_target=loss_target, m_ssd_in_proj=m_ssd_in_proj, m_ssd_conv_w=m_ssd_conv_w, m_ssd_conv_b=m_ssd_conv_b, m_ssd_dt_bias=m_ssd_dt_bias, m_ssd_A_log=m_ssd_A_log, m_ssd_D=m_ssd_D, m_ssd_norm_w=m_ssd_norm_w, m_ssd_out_proj=m_ssd_out_proj, m_pool_w=m_pool_w, m_pool_b=m_pool_b, m_pool_scale=m_pool_scale, m_mlp_w1=m_mlp_w1, m_mlp_w2=m_mlp_w2, m_ln_mix_g=m_ln_mix_g, m_ln_mix_b=m_ln_mix_b, m_ln_ffn_g=m_ln_ffn_g, m_ln_ffn_b=m_ln_ffn_b, v_ssd_in_proj=v_ssd_in_proj, v_ssd_conv_w=v_ssd_conv_w, v_ssd_conv_b=v_ssd_conv_b, v_ssd_dt_bias=v_ssd_dt_bias, v_ssd_A_log=v_ssd_A_log, v_ssd_D=v_ssd_D, v_ssd_norm_w=v_ssd_norm_w, v_ssd_out_proj=v_ssd_out_proj, v_pool_w=v_pool_w, v_pool_b=v_pool_b, v_pool_scale=v_pool_scale, v_mlp_w1=v_mlp_w1, v_mlp_w2=v_mlp_w2, v_ln_mix_g=v_ln_mix_g, v_ln_mix_b=v_ln_mix_b, v_ln_ffn_g=v_ln_ffn_g, v_ln_ffn_b=v_ln_ffn_b)
    weights = {n: given[n] for n in TWIN_WEIGHTS}
    shared = {n: given[n] for n in SHARED_INPUTS}
    per_example = {n: given[n] for n in ['x']}
    grad_fn = _jax.value_and_grad(_loss, argnums=(0, 1))

    def one_microbatch(ex, loss_target):
        ex = dict(ex)
        diff = ex.pop(TWIN_DIFF_INPUT)
        return grad_fn(weights, diff, {**shared, **ex}, loss_target)

    if N_MICROBATCH == 1:
        loss, (grad_w, grad_x) = one_microbatch(per_example, given["loss_target"])
    else:
        def body(carry, xs):
            loss_sum, grad_sum = carry
            l_k, (gw_k, gx_k) = one_microbatch(xs[0], xs[1])
            with _jax.named_scope("update"):
                return (loss_sum + l_k, _jax.tree.map(_jnp.add, grad_sum, gw_k)), gx_k

        init = (_jnp.zeros((), _jnp.float32), _jax.tree.map(_jnp.zeros_like, weights))
        (loss, grad_w), grad_x = _jax.lax.scan(body, init, (per_example, given["loss_target"]))
    with _jax.named_scope("update"):
        delta_w, new_m, new_v = {}, {}, {}
        for n in TWIN_WEIGHTS:
            delta_w[n], new_m[n], new_v[n] = _adamw(weights[n], grad_w[n], given["m_" + n], given["v_" + n])
    return (loss, grad_x, *[grad_w[n] for n in TWIN_WEIGHTS], *[delta_w[n] for n in TWIN_WEIGHTS],
            *[new_m[n] for n in TWIN_WEIGHTS], *[new_v[n] for n in TWIN_WEIGHTS])
```

```python
import functools

import jax
import jax.numpy as jnp
from jax import lax
from jax.experimental import pallas as pl
from jax.experimental.pallas import tpu as pltpu

F32 = jnp.float32
BF16 = jnp.bfloat16

HEAD_DIM = 64
N_GROUPS = 8
D_STATE = 128
CHUNK = 128
CONV_WIDTH = 5
POOL_WINDOWS = (2, 4, 8, 16)
DEPTH = 4
ALPHA = (2.0 * DEPTH) ** 0.25
LN_EPS = 1e-5
RMS_EPS = 1e-5
ADAM_LR, ADAM_B1, ADAM_B2, ADAM_EPS, ADAM_WD, ADAM_STEP = 0.001, 0.9, 0.999, 1e-08, 0.01, 10

LANES = 128
VMEM_LIMIT = 48 * 1024 * 1024
NEG = -1e30
MESH = pl.DeviceIdType.MESH

NN = (((1,), (0,)), ((), ()))
NT = (((1,), (1,)), ((), ()))
TN = (((0,), (0,)), ((), ()))


def _tile(dim, pref, mult=LANES):
    if dim <= pref:
        return dim
    t = (pref // mult) * mult
    while t > mult and dim % t:
        t -= mult
    assert dim % t == 0, (dim, pref)
    return t


def _cp(sem):
    return pltpu.CompilerParams(dimension_semantics=sem, vmem_limit_bytes=VMEM_LIMIT)


def _sds(shape, dtype):
    return jax.ShapeDtypeStruct(tuple(shape), dtype)


def _dot(a, b, dn=NN):
    return lax.dot_general(a, b, dn, preferred_element_type=F32)


def _bf(x):
    return x.astype(BF16)


def _sigmoid(x):
    return 1.0 / (1.0 + jnp.exp(-x))


def _mm(name, a, b, a_spec, b_spec, dn, grid, tm, tn, out_shapes, out_specs, epi=None, extras=(), extra_specs=()):
    nk = grid[2]
    n_ex, n_out = len(extras), len(out_shapes)

    def kern(*refs):
        a_ref, b_ref = refs[0], refs[1]
        ex = refs[2:2 + n_ex]
        outs = refs[2 + n_ex:2 + n_ex + n_out]
        acc = refs[-1]
        k = pl.program_id(2)

        @pl.when(k == 0)
        def _():
            acc[...] = jnp.zeros_like(acc)

        acc[...] += _dot(_bf(a_ref[...]), _bf(b_ref[...]), dn)

        @pl.when(k == nk - 1)
        def _():
            res = epi(acc[...], *[e[...] for e in ex]) if epi is not None else (acc[...],)
            for o, r in zip(outs, res):
                o[...] = r.astype(o.dtype)

    return pl.pallas_call(
        kern, name=name, grid=grid, in_specs=[a_spec, b_spec, *extra_specs], out_specs=list(out_specs),
        out_shape=list(out_shapes), scratch_shapes=[pltpu.VMEM((tm, tn), F32)],
        compiler_params=_cp(("parallel", "parallel", "arbitrary")))(a, b, *extras)


class _ColShard:
    def __init__(self, R, C):
        self.R, self.C = R, C

    def b_nn(self, tk, tn):
        n = self.C // tn
        return pl.BlockSpec((None, tk, tn), lambda i, j, k: (j // n, k, j % n))

    def b_nt(self, tn, tk):
        n = self.C // tk
        return pl.BlockSpec((None, tn, tk), lambda i, j, k: (k // n, j, k % n))

    def out(self, tm, tn):
        n = self.C // tn
        return pl.BlockSpec((None, tm, tn), lambda i, j, k: (j // n, i, j % n))


class _RowShard:
    def __init__(self, R, C):
        self.R, self.C = R, C

    def b_nn(self, tk, tn):
        n = self.R // tk
        return pl.BlockSpec((None, tk, tn), lambda i, j, k: (k // n, k % n, j))

    def b_nt(self, tn, tk):
        n = self.R // tn
        return pl.BlockSpec((None, tn, tk), lambda i, j, k: (j // n, j % n, k))

    def out(self, tm, tn):
        n = self.R // tm
        return pl.BlockSpec((None, tm, tn), lambda i, j, k: (i // n, i % n, j))


def _a_nn(tm, tk):
    return pl.BlockSpec((tm, tk), lambda i, j, k: (i, k))


def _a_tn(tk, tm):
    return pl.BlockSpec((tk, tm), lambda i, j, k: (k, i))


def _b_tn(tk, tn):
    return pl.BlockSpec((tk, tn), lambda i, j, k: (k, j))


def _o_ij(tm, tn):
    return pl.BlockSpec((tm, tn), lambda i, j, k: (i, j))


def _mm_act(name, a, w, wspec, M, N, K, out_dtypes, epi=None, extras=(), nt=False, tn_pref=1024, tk_pref=1024):
    tm = _tile(M, 1024)
    if isinstance(wspec, (_ColShard, _RowShard)):
        nlim, klim = (wspec.R, wspec.C) if nt else (wspec.C, wspec.R)
        tn = _tile(nlim, tn_pref)
        tk = _tile(klim, tk_pref)
        b_spec = wspec.b_nt(tn, tk) if nt else wspec.b_nn(tk, tn)
    else:
        tn = _tile(N, tn_pref)
        tk = _tile(K, tk_pref)
        b_spec = (pl.BlockSpec((tn, tk), lambda i, j, k: (j, k)) if nt
                  else pl.BlockSpec((tk, tn), lambda i, j, k: (k, j)))
    grid = (M // tm, N // tn, K // tk)
    outs = [_sds((M, N), dt) for dt in out_dtypes]
    return _mm(name, a, w, _a_nn(tm, tk), b_spec, NT if nt else NN, grid, tm, tn, outs,
               [_o_ij(tm, tn)] * len(outs), epi, extras, [_o_ij(tm, tn)] * len(extras))


def _mm_wgrad(name, a, b, M, N, K, out_shape, out_spec_fn, tm_pref=1024, tn_pref=1024):
    tm = _tile(M, tm_pref)
    tn = _tile(N, tn_pref)
    tk = _tile(K, 1024)
    grid = (M // tm, N // tn, K // tk)
    return _mm(name, a, b, _a_tn(tk, tm), _b_tn(tk, tn), TN, grid, tm, tn, [out_shape], [out_spec_fn(tm, tn)])[0]


def _ln_stats(r):
    mu = jnp.mean(r, axis=-1, keepdims=True)
    xc = r - mu
    var = jnp.mean(xc * xc, axis=-1, keepdims=True)
    return xc, lax.rsqrt(var + LN_EPS)


def _part8(v):
    return v.reshape(v.shape[0] // 8, 8, v.shape[1]).sum(axis=0)


def _acc_out(ref, val, first):
    @pl.when(first)
    def _():
        ref[...] = val

    @pl.when(jnp.logical_not(first))
    def _():
        ref[...] += val


def _resln(name, x, mix, g, b):
    T, D = x.shape
    tr = _tile(T, 256, 8)

    def kern(x_ref, m_ref, g_ref, b_ref, r_ref, y_ref, yb_ref):
        r = ALPHA * x_ref[...] + m_ref[...]
        xc, rstd = _ln_stats(r)
        y = xc * rstd * g_ref[...] + b_ref[...]
        r_ref[...] = r
        y_ref[...] = y
        yb_ref[...] = _bf(y)

    row = pl.BlockSpec((tr, D), lambda i: (i, 0))
    vec = pl.BlockSpec((1, D), lambda i: (0, 0))
    return pl.pallas_call(kern, name=name, grid=(T // tr,), in_specs=[row, row, vec, vec], out_specs=[row, row, row],
                          out_shape=[_sds((T, D), F32), _sds((T, D), F32), _sds((T, D), BF16)],
                          compiler_params=_cp(("parallel",)))(x, mix, g, b)


def _lnbwd(name, dy, r, g):
    T, D = r.shape
    tr = _tile(T, 256, 8)

    def kern(dy_ref, r_ref, g_ref, dr_ref, drb_ref, dg_ref, db_ref):
        dyv = dy_ref[...]
        xc, rstd = _ln_stats(r_ref[...])
        xh = xc * rstd
        dxh = dyv * g_ref[...]
        m1 = jnp.mean(dxh, axis=-1, keepdims=True)
        m2 = jnp.mean(dxh * xh, axis=-1, keepdims=True)
        dr = rstd * (dxh - m1 - xh * m2)
        dr_ref[...] = dr
        drb_ref[...] = _bf(dr)
        first = pl.program_id(0) == 0
        _acc_out(dg_ref, _part8(dyv * xh), first)
        _acc_out(db_ref, _part8(dyv), first)

    row = pl.BlockSpec((tr, D), lambda i: (i, 0))
    vec = pl.BlockSpec((1, D), lambda i: (0, 0))
    acc = pl.BlockSpec((8, D), lambda i: (0, 0))
    return pl.pallas_call(kern, name=name, grid=(T // tr,), in_specs=[row, row, vec], out_specs=[row, row, acc, acc],
                          out_shape=[_sds((T, D), F32), _sds((T, D), BF16), _sds((8, D), F32), _sds((8, D), F32)],
                          compiler_params=_cp(("arbitrary",)))(dy, r, g)


def _loss_head(name, y, tgt):
    T, D = y.shape
    tr = _tile(T, 256, 8)
    nt = T // tr

    def kern(y_ref, t_ref, dy_ref, loss_ref, acc):
        i = pl.program_id(0)
        e = y_ref[...] - t_ref[...]
        dy_ref[...] = e * (1.0 / D)
        _acc_out(acc, _part8(e * e), i == 0)

        @pl.when(i == nt - 1)
        def _():
            tot = jnp.sum(jnp.sum(acc[...], axis=1, keepdims=True), axis=0, keepdims=True)
            loss_ref[...] = jnp.broadcast_to(tot * (0.5 / D), loss_ref.shape)

    row = pl.BlockSpec((tr, D), lambda i: (i, 0))
    return pl.pallas_call(kern, name=name, grid=(nt,), in_specs=[row, row],
                          out_specs=[row, pl.BlockSpec((8, LANES), lambda i: (0, 0))],
                          out_shape=[_sds((T, D), F32), _sds((8, LANES), F32)],
                          scratch_shapes=[pltpu.VMEM((8, D), F32)], compiler_params=_cp(("arbitrary",)))(y, tgt)


def _shift(x, o):
    if o == 0:
        return x
    T = x.shape[0]
    rolled = pltpu.roll(x, (-o) % T, 0)
    t = lax.broadcasted_iota(jnp.int32, x.shape, 0)
    return jnp.where((t + o >= 0) & (t + o < T), rolled, 0.0)


def _run(u, h, step):
    s, k = u, 1
    while k < h:
        s = s + _shift(s, step * k)
        k *= 2
    return s


def _winsum(u, win, transposed):
    h = win // 2
    if not transposed:
        return _run(u, h, 1) + _shift(_run(u, h, -1), -1)
    return _run(u, h, -1) + _shift(_run(u, h, 1), 1)


def _wincount(shape, win):
    t = lax.broadcasted_iota(jnp.int32, shape, 0)
    T = shape[0]
    lo = jnp.maximum(t - win // 2, 0)
    hi = jnp.minimum(t - win // 2 + win, T)
    return (hi - lo).astype(F32)


def _pool_m(name, u):
    T, D = u.shape
    per = (D // len(POOL_WINDOWS)) // LANES

    def kern(u_ref, m_ref):
        j = pl.program_id(0)
        for gi, win in enumerate(POOL_WINDOWS):
            @pl.when(j // per == gi)
            def _():
                uv = u_ref[...]
                m_ref[...] = _bf(_winsum(uv, win, False) / _wincount(uv.shape, win) - uv)

    col = pl.BlockSpec((T, LANES), lambda j: (0, j))
    return pl.pallas_call(kern, name=name, grid=(D // LANES,), in_specs=[col], out_specs=col,
                          out_shape=_sds((T, D), BF16), compiler_params=_cp(("parallel",)))(u)


def _pool_fwd(name, m, x, w, bias, scale, g, b):
    T, D = x.shape
    ng = len(POOL_WINDOWS)
    dg = D // ng
    tr = _tile(T, 256, 16)

    def kern(m_ref, x_ref, w_ref, bias_ref, sc_ref, g_ref, b_ref, yp_ref, r_ref, y_ref, yb_ref):
        for gi in range(ng):
            sl = slice(gi * dg, (gi + 1) * dg)
            yp_ref[:, sl] = _dot(m_ref[:, sl], w_ref[gi]) + bias_ref[:, sl]
        r = ALPHA * x_ref[...] + yp_ref[...] * sc_ref[...]
        xc, rstd = _ln_stats(r)
        y = xc * rstd * g_ref[...] + b_ref[...]
        r_ref[...] = r
        y_ref[...] = y
        yb_ref[...] = _bf(y)

    row = pl.BlockSpec((tr, D), lambda i: (i, 0))
    vec = pl.BlockSpec((1, D), lambda i: (0, 0))
    wsp = pl.BlockSpec((ng, dg, dg), lambda i: (0, 0, 0))
    return pl.pallas_call(kern, name=name, grid=(T // tr,), in_specs=[row, row, wsp, vec, vec, vec, vec],
                          out_specs=[row, row, row, row],
                          out_shape=[_sds((T, D), F32), _sds((T, D), F32), _sds((T, D), F32), _sds((T, D), BF16)],
                          compiler_params=_cp(("parallel",)))(m, x, w, bias, scale, g, b)


def _pool_bwd_a(name, dr, ypre, scale, w):
    T, D = dr.shape
    ng = len(POOL_WINDOWS)
    dg = D // ng
    tr = _tile(T, 256, 16)

    def kern(dr_ref, yp_ref, sc_ref, w_ref, dm_ref, dyp_ref, dsc_ref, dbi_ref):
        drv = dr_ref[...]
        dyp = drv * sc_ref[...]
        dyp_ref[...] = _bf(dyp)
        for gi in range(ng):
            sl = slice(gi * dg, (gi + 1) * dg)
            dm_ref[:, sl] = _dot(dyp_ref[:, sl], w_ref[gi], NT)
        first = pl.program_id(0) == 0
        _acc_out(dsc_ref, _part8(drv * yp_ref[...]), first)
        _acc_out(dbi_ref, _part8(dyp), first)

    row = pl.BlockSpec((tr, D), lambda i: (i, 0))
    vec = pl.BlockSpec((1, D), lambda i: (0, 0))
    acc = pl.BlockSpec((8, D), lambda i: (0, 0))
    wsp = pl.BlockSpec((ng, dg, dg), lambda i: (0, 0, 0))
    return pl.pallas_call(kern, name=name, grid=(T // tr,), in_specs=[row, row, vec, wsp],
                          out_specs=[row, row, acc, acc],
                          out_shape=[_sds((T, D), F32), _sds((T, D), BF16), _sds((8, D), F32), _sds((8, D), F32)],
                          compiler_params=_cp(("arbitrary",)))(dr, ypre, scale, w)


def _pool_bwd_win(name, dm, dr):
    T, D = dm.shape
    per = (D // len(POOL_WINDOWS)) // LANES

    def kern(dm_ref, dr_ref, du_ref):
        j = pl.program_id(0)
        for gi, win in enumerate(POOL_WINDOWS):
            @pl.when(j // per == gi)
            def _():
                dmv = dm_ref[...]
                du_ref[...] = ALPHA * dr_ref[...] + _winsum(dmv / _wincount(dmv.shape, win), win, True) - dmv

    col = pl.BlockSpec((T, LANES), lambda j: (0, j))
    return pl.pallas_call(kern, name=name, grid=(D // LANES,), in_specs=[col, col], out_specs=col,
                          out_shape=_sds((T, D), F32), compiler_params=_cp(("parallel",)))(dm, dr)


def _conv_pre(x, w_ref, b_ref):
    acc = b_ref[...] + w_ref[2:3, :] * x
    for k in (0, 1, 3, 4):
        acc = acc + w_ref[k:k + 1, :] * _shift(x, k - 2)
    return acc


def _conv_fwd(name, zx, blk0, wblk0, w, b, G, cw):
    T = zx.shape[0]
    per = cw // LANES

    def kern(x_ref, w_ref, b_ref, o_ref):
        pre = _conv_pre(x_ref[...], w_ref, b_ref)
        o_ref[0] = pre * _sigmoid(pre)

    return pl.pallas_call(
        kern, name=name, grid=(G * per,),
        in_specs=[pl.BlockSpec((T, LANES), lambda j: (0, blk0 + j)),
                  pl.BlockSpec((CONV_WIDTH, LANES), lambda j: (0, wblk0 + j)),
                  pl.BlockSpec((1, LANES), lambda j: (0, wblk0 + j))],
        out_specs=pl.BlockSpec((1, T, LANES), lambda j: (j // per, 0, j % per)),
        out_shape=_sds((G, T, cw), F32), compiler_params=_cp(("parallel",)))(zx, w, b)


def _conv_bwd(name, zx, blk0, wblk0, w, b, adds, dyd=None):
    T = zx.shape[0]
    G, _, cw = adds[0].shape
    per = cw // LANES
    na = len(adds)

    def kern(*refs):
        x_ref, w_ref, b_ref = refs[:3]
        add_refs = refs[3:3 + na]
        rest = refs[3 + na:]
        if dyd is not None:
            dy_ref, dv_ref = rest[:2]
            rest = rest[2:]
        dx_ref, db_ref = rest[0], rest[1]
        dw_refs = rest[2:]
        x = x_ref[...]
        pre = _conv_pre(x, w_ref, b_ref)
        sg = _sigmoid(pre)
        dact = add_refs[0][0]
        for r in add_refs[1:]:
            dact = dact + r[0]
        if dyd is not None:
            dact = dact + dy_ref[0] * dv_ref[...]
        dpre = dact * (sg * (1.0 + pre * (1.0 - sg)))
        row0 = lax.broadcasted_iota(jnp.int32, (8, LANES), 0) == 0

        def put(ref, v):
            ref[...] = jnp.where(row0, jnp.sum(v, axis=0, keepdims=True), 0.0)

        put(db_ref, dpre)
        dx = w_ref[2:3, :] * dpre
        put(dw_refs[2], dpre * x)
        for k in (0, 1, 3, 4):
            put(dw_refs[k], dpre * _shift(x, k - 2))
            dx = dx + w_ref[k:k + 1, :] * _shift(dpre, 2 - k)
        dx_ref[...] = _bf(dx)

    gsp = pl.BlockSpec((1, T, LANES), lambda j: (j // per, 0, j % per))
    in_specs = [pl.BlockSpec((T, LANES), lambda j: (0, blk0 + j)),
                pl.BlockSpec((CONV_WIDTH, LANES), lambda j: (0, wblk0 + j)),
                pl.BlockSpec((1, LANES), lambda j: (0, wblk0 + j))] + [gsp] * na
    args = [zx, w, b, *adds]
    if dyd is not None:
        in_specs += [gsp, pl.BlockSpec((1, LANES), lambda j: (0, j))]
        args += list(dyd)
    n = G * cw
    p8 = pl.BlockSpec((8, LANES), lambda j: (0, j))
    res = pl.pallas_call(
        kern, name=name, grid=(G * per,), in_specs=in_specs,
        out_specs=[pl.BlockSpec((T, LANES), lambda j: (0, j))] + [p8] * (1 + CONV_WIDTH),
        out_shape=[_sds((T, n), BF16)] + [_sds((8, n), F32)] * (1 + CONV_WIDTH),
        compiler_params=_cp(("parallel",)))(*args)
    return res[0], res[1], res[2:]


def _split3(x):
    x1 = _bf(x)
    r1 = x - x1.astype(F32)
    x2 = _bf(r1)
    x3 = _bf(r1 - x2.astype(F32))
    return x1, x2, x3


def _tri_dot(tri, x, dn=NN):
    a, b, c = _split3(x)
    return _dot(tri, a, dn) + _dot(tri, b, dn) + _dot(tri, c, dn)


def _tri(lower):
    i = lax.broadcasted_iota(jnp.int32, (CHUNK, CHUNK), 0)
    j = lax.broadcasted_iota(jnp.int32, (CHUNK, CHUNK), 1)
    return jnp.where((i >= j) if lower else (i <= j), 1.0, 0.0).astype(BF16)


def _softplus(x):
    return jnp.maximum(x, 0.0) + jnp.log(1.0 + jnp.exp(-jnp.abs(x)))


def _dt_prep(name, zx, dtblk, dt_bias, a_log):
    T = zx.shape[0]
    H = LANES // 2

    def kern(x_ref, bias_ref, al_ref, dt_ref, acs_ref):
        dt = _softplus(x_ref[...] + bias_ref[...])
        dta = dt * (-jnp.exp(al_ref[...]))
        lane = lax.broadcasted_iota(jnp.int32, dta.shape, 1)
        dt_ref[...] = dt
        acs_ref[...] = jnp.where(lane < H, _tri_dot(_tri(True), dta), _tri_dot(_tri(False), dta))

    blk = pl.BlockSpec((CHUNK, LANES), lambda c: (c, 0))
    vec = pl.BlockSpec((1, LANES), lambda c: (0, 0))
    return pl.pallas_call(kern, name=name, grid=(T // CHUNK,),
                          in_specs=[pl.BlockSpec((CHUNK, LANES), lambda c: (c, dtblk)), vec, vec],
                          out_specs=[blk, blk], out_shape=[_sds((T, LANES), F32), _sds((T, LANES), F32)],
                          compiler_params=_cp(("parallel",)))(zx, dt_bias, a_log)


def _dt_bwd(name, ddta, ddtx, zx, dtblk, dt_bias, a_log):
    T = zx.shape[0]

    def kern(da_ref, dx_ref, x_ref, bias_ref, al_ref, draw_ref, dbias_ref, dal_ref):
        pre = x_ref[...] + bias_ref[...]
        dt = _softplus(pre)
        A = -jnp.exp(al_ref[...])
        dav = da_ref[...]
        draw = (dav * A + dx_ref[...]) * _sigmoid(pre)
        draw_ref[...] = _bf(draw)
        first = pl.program_id(0) == 0
        _acc_out(dbias_ref, _part8(draw), first)
        _acc_out(dal_ref, _part8(dav * dt) * A, first)

    blk = pl.BlockSpec((CHUNK, LANES), lambda c: (c, 0))
    vec = pl.BlockSpec((1, LANES), lambda c: (0, 0))
    acc = pl.BlockSpec((8, LANES), lambda c: (0, 0))
    return pl.pallas_call(kern, name=name, grid=(T // CHUNK,),
                          in_specs=[blk, blk, pl.BlockSpec((CHUNK, LANES), lambda c: (c, dtblk)), vec, vec],
                          out_specs=[blk, acc, acc],
                          out_shape=[_sds((T, LANES), BF16), _sds((8, LANES), F32), _sds((8, LANES), F32)],
                          compiler_params=_cp(("arbitrary",)))(ddta, ddtx, zx, dt_bias, a_log)


def _ssd_specs(T, GW, hpg, d, cmap):
    nc = T // CHUNK
    xs = pl.BlockSpec((1, CHUNK, GW), lambda g, c: (g, cmap(c), 0))
    bc = pl.BlockSpec((1, CHUNK, D_STATE), lambda g, c: (g, cmap(c), 0))
    col = pl.BlockSpec((1, CHUNK, hpg), lambda g, c: (d * N_GROUPS + g, cmap(c), 0))
    row = pl.BlockSpec((1, hpg, CHUNK), lambda g, c: (d * N_GROUPS + g, 0, cmap(c)))
    st = pl.BlockSpec((1, 1, GW, D_STATE), lambda g, c: (g, cmap(c), 0, 0))
    ocol = pl.BlockSpec((1, CHUNK, hpg), lambda g, c: (g, cmap(c), 0))
    return nc, xs, bc, col, row, st, ocol


def _chunk_mask(rev):
    li = lax.broadcasted_iota(jnp.int32, (CHUNK, CHUNK), 0)
    si = lax.broadcasted_iota(jnp.int32, (CHUNK, CHUNK), 1)
    return (li <= si) if rev else (li >= si)


def _ssd_fwd(name, xs, Bg, Cg, dt_col, acs_col, acs_row, d):
    G, T, GW = xs.shape
    hpg = GW // HEAD_DIM
    rev = d == 1
    nc0 = T // CHUNK
    cmap = (lambda c: nc0 - 1 - c) if rev else (lambda c: c)
    nc, xs_s, bc_s, col_s, row_s, st_s, _ = _ssd_specs(T, GW, hpg, d, cmap)
    last = 0 if rev else CHUNK - 1

    def kern(xs_ref, b_ref, c_ref, dt_ref, ac_ref, ar_ref, y_ref, st_ref, state):
        @pl.when(pl.program_id(1) == 0)
        def _():
            state[...] = jnp.zeros_like(state)

        st_ref[0, 0] = state[...]
        Bm = _bf(b_ref[0])
        Cm = _bf(c_ref[0])
        S = _dot(Cm, Bm, NT)
        mask = _chunk_mask(rev)
        x = xs_ref[0]
        dtc = dt_ref[0]
        ac = ac_ref[0]
        ar = ar_ref[0]
        for r in range(hpg):
            hs = slice(r * HEAD_DIM, (r + 1) * HEAD_DIM)
            a_col = ac[:, r:r + 1]
            lam = jnp.exp(jnp.where(mask, a_col - ar[r:r + 1, :], NEG))
            xdt = x[:, hs] * dtc[:, r:r + 1]
            Hp = state[hs, :]
            a_end = a_col[last:last + 1, :]
            yd = _dot(_bf(S * lam), _bf(xdt))
            yo = _dot(Cm, _bf(Hp), NT) * jnp.exp(a_col)
            y_ref[0, :, hs] = yd + yo
            Hn = _dot(_bf(xdt * jnp.exp(a_end - a_col)), Bm, TN)
            state[hs, :] = jnp.exp(a_end) * Hp + Hn

    return pl.pallas_call(
        kern, name=name, grid=(G, nc), in_specs=[xs_s, bc_s, bc_s, col_s, col_s, row_s], out_specs=[xs_s, st_s],
        out_shape=[_sds((G, T, GW), F32), _sds((G, nc, GW, D_STATE), F32)],
        scratch_shapes=[pltpu.VMEM((GW, D_STATE), F32)],
        compiler_params=_cp(("parallel", "arbitrary")))(xs, Bg, Cg, dt_col, acs_col, acs_row)


def _ssd_bwd(name, dy, xs, Bg, Cg, dt_col, acs_col, acs_row, states, d):
    G, T, GW = xs.shape
    hpg = GW // HEAD_DIM
    rev = d == 1
    nc0 = T // CHUNK
    cmap = (lambda c: c) if rev else (lambda c: nc0 - 1 - c)
    nc, xs_s, bc_s, col_s, row_s, st_s, ocol_s = _ssd_specs(T, GW, hpg, d, cmap)
    last = 0 if rev else CHUNK - 1

    def kern(dy_ref, xs_ref, b_ref, c_ref, dt_ref, ac_ref, ar_ref, st_ref,
             dxs_ref, db_ref, dc_ref, dda_ref, ddx_ref, dstate):
        @pl.when(pl.program_id(1) == 0)
        def _():
            dstate[...] = jnp.zeros_like(dstate)

        Bm = _bf(b_ref[0])
        Cm = _bf(c_ref[0])
        S = _dot(Cm, Bm, NT)
        mask = _chunk_mask(rev)
        x = xs_ref[0]
        dyv = dy_ref[0]
        dtc = dt_ref[0]
        ac = ac_ref[0]
        ar = ar_ref[0]
        lane = lax.broadcasted_iota(jnp.int32, (CHUNK, LANES), 1)
        rowi = lax.broadcasted_iota(jnp.int32, (CHUNK, 1), 0)
        dS = jnp.zeros((CHUNK, CHUNK), F32)
        dBa = jnp.zeros((CHUNK, D_STATE), F32)
        dCa = jnp.zeros((CHUNK, D_STATE), F32)
        dacs = jnp.zeros((CHUNK, LANES), F32)
        ddx = jnp.zeros((CHUNK, LANES), F32)
        for r in range(hpg):
            hs = slice(r * HEAD_DIM, (r + 1) * HEAD_DIM)
            a_col = ac[:, r:r + 1]
            lam = jnp.exp(jnp.where(mask, a_col - ar[r:r + 1, :], NEG))
            Mf = S * lam
            xr = x[:, hs]
            dt_c = dtc[:, r:r + 1]
            xdt = xr * dt_c
            dyr = dyv[:, hs]
            Hp = st_ref[0, 0, hs, :]
            dHn = dstate[hs, :]
            Hpb, dHnb = _bf(Hp), _bf(dHn)
            a_end = a_col[last:last + 1, :]
            E = jnp.exp(a_col)
            e_end = jnp.exp(a_end)
            dte = jnp.exp(a_end - a_col)
            BdH = _dot(Bm, dHnb, NT)
            dxdt = _dot(_bf(Mf), _bf(dyr), TN) + dte * BdH
            dM = _dot(_bf(dyr), _bf(xdt), NT)
            dS = dS + dM * lam
            Gm = dM * Mf
            CHp = _dot(Cm, Hpb, NT)
            Edy = E * dyr
            dCa = dCa + _dot(_bf(Edy), Hpb)
            dBa = dBa + _dot(_bf(dte * xdt), dHnb)
            dstate[hs, :] = e_end * dHn + _dot(_bf(Edy), Cm, TN)
            t_state = dte * jnp.sum(xdt * BdH, axis=-1, keepdims=True)
            da = (jnp.sum(Gm, axis=-1, keepdims=True) - jnp.sum(Gm.T, axis=-1, keepdims=True)
                  + jnp.sum(Edy * CHp, axis=-1, keepdims=True) - t_state)
            tot = (jnp.sum(t_state, axis=0, keepdims=True)
                   + e_end * jnp.sum(jnp.sum(dHn * Hp, axis=-1, keepdims=True), axis=0, keepdims=True))
            da = da + jnp.where(rowi == last, tot, 0.0)
            dacs = dacs + jnp.where(lane == r, da, 0.0)
            ddx = ddx + jnp.where(lane == r, jnp.sum(dxdt * xr, axis=-1, keepdims=True), 0.0)
            dxs_ref[0, :, hs] = dxdt * dt_c
        dSb = _bf(dS)
        dc_ref[0] = dCa + _dot(dSb, Bm)
        db_ref[0] = dBa + _dot(dSb, Cm, TN)
        dda_ref[0] = _tri_dot(_tri(rev), dacs)[:, :hpg]
        ddx_ref[0] = ddx[:, :hpg]

    return pl.pallas_call(
        kern, name=name, grid=(G, nc), in_specs=[xs_s, xs_s, bc_s, bc_s, col_s, col_s, row_s, st_s],
        out_specs=[xs_s, bc_s, bc_s, ocol_s, ocol_s],
        out_shape=[_sds((G, T, GW), F32), _sds((G, T, D_STATE), F32), _sds((G, T, D_STATE), F32),
                   _sds((G, T, hpg), F32), _sds((G, T, hpg), F32)],
        scratch_shapes=[pltpu.VMEM((GW, D_STATE), F32)],
        compiler_params=_cp(("parallel", "arbitrary")))(dy, xs, Bg, Cg, dt_col, acs_col, acs_row, states)


def _gate_core(yf, yb, xs, z, dv):
    y = yf + yb + xs * dv
    sg = _sigmoid(z)
    sz = z * sg
    gy = y * sz
    rstd = lax.rsqrt(jnp.mean(gy * gy, axis=-1, keepdims=True) + RMS_EPS)
    return y, sg, sz, gy, rstd


def _gate_fwd(name, yf, yb, xs, zx, dvec, nw):
    G, T, GW = xs.shape
    tr = _tile(T, 512, 16)

    def kern(yf_ref, yb_ref, xs_ref, z_ref, dv_ref, nw_ref, o_ref):
        _, _, _, gy, rstd = _gate_core(yf_ref[0], yb_ref[0], xs_ref[0], z_ref[...], dv_ref[...])
        o_ref[...] = _bf(gy * rstd * nw_ref[...])

    gsp = pl.BlockSpec((1, tr, GW), lambda g, t: (g, t, 0))
    zsp = pl.BlockSpec((tr, GW), lambda g, t: (t, g))
    vsp = pl.BlockSpec((1, GW), lambda g, t: (0, g))
    return pl.pallas_call(kern, name=name, grid=(G, T // tr), in_specs=[gsp, gsp, gsp, zsp, vsp, vsp], out_specs=zsp,
                          out_shape=_sds((T, G * GW), BF16),
                          compiler_params=_cp(("parallel", "parallel")))(yf, yb, xs, zx, dvec, nw)


def _gate_bwd(name, dgyn, yf, yb, xs, zx, dvec, nw):
    G, T, GW = xs.shape
    tr = _tile(T, 512, 16)

    def kern(dg_ref, yf_ref, yb_ref, xs_ref, z_ref, dv_ref, nw_ref, dy_ref, dz_ref, dnw_ref, ddl_ref):
        xsv = xs_ref[0]
        zv = z_ref[...]
        y, sg, sz, gy, rstd = _gate_core(yf_ref[0], yb_ref[0], xsv, zv, dv_ref[...])
        n = gy * rstd
        dgv = dg_ref[...]
        dn = dgv * nw_ref[...]
        dgy = rstd * (dn - n * jnp.mean(dn * n, axis=-1, keepdims=True))
        dyv = dgy * sz
        dy_ref[0] = dyv
        dz_ref[...] = _bf(dgy * y * (sg * (1.0 + zv * (1.0 - sg))))
        first = pl.program_id(1) == 0
        _acc_out(dnw_ref, _part8(dgv * n), first)
        _acc_out(ddl_ref, _part8(dyv * xsv), first)

    gsp = pl.BlockSpec((1, tr, GW), lambda g, t: (g, t, 0))
    zsp = pl.BlockSpec((tr, GW), lambda g, t: (t, g))
    vsp = pl.BlockSpec((1, GW), lambda g, t: (0, g))
    asp = pl.BlockSpec((8, GW), lambda g, t: (0, g))
    return pl.pallas_call(kern, name=name, grid=(G, T // tr), in_specs=[zsp, gsp, gsp, gsp, zsp, vsp, vsp],
                          out_specs=[gsp, zsp, asp, asp],
                          out_shape=[_sds((G, T, GW), F32), _sds((T, G * GW), BF16), _sds((8, G * GW), F32),
                                     _sds((8, G * GW), F32)],
                          compiler_params=_cp(("parallel", "arbitrary")))(dgyn, yf, yb, xs, zx, dvec, nw)


def _head_sum(name, v):
    n = v.shape[1]
    H = n // HEAD_DIM

    def kern(v_ref, o_ref):
        i = lax.broadcasted_iota(jnp.int32, (n, H), 0)
        j = lax.broadcasted_iota(jnp.int32, (n, H), 1)
        sel = jnp.where((i >= j * HEAD_DIM) & (i < (j + 1) * HEAD_DIM), 1.0, 0.0).astype(BF16)
        a, b, c = _split3(v_ref[...])
        o_ref[...] = _dot(a, sel) + _dot(b, sel) + _dot(c, sel)

    return pl.pallas_call(kern, name=name, out_shape=_sds((8, H), F32))(v)


def _relu2_epi(acc):
    return acc, jnp.square(jnp.maximum(acc, 0.0))


def _dh_epi(acc, h):
    return (acc * (2.0 * jnp.maximum(h, 0.0)),)


def _resid_epi(acc, e):
    return (acc + ALPHA * e,)


def _cols(v, G):
    T = v.shape[0]
    return v.reshape(T, 2 * G, -1).transpose(1, 0, 2)


def _ssd_fwd_layer(tag, x, xb, W, j):
    T, D = x.shape
    DI = 2 * D
    DBC = N_GROUPS * D_STATE
    win = W["win"][j]
    NZ = win.shape[1]
    zx = _mm_act(tag + "_inproj", xb, win, None, T, NZ, D, [F32], tn_pref=1152)[0]
    zb = DI // LANES
    cw, cb = W["conv_w"][j], W["conv_b"][j]
    xs = _conv_fwd(tag + "_convx", zx, zb, 0, cw, cb, N_GROUPS, DI // N_GROUPS)
    Bg = _conv_fwd(tag + "_convb", zx, zb + DI // LANES, DI // LANES, cw, cb, N_GROUPS, D_STATE)
    Cg = _conv_fwd(tag + "_convc", zx, zb + (DI + DBC) // LANES, (DI + DBC) // LANES, cw, cb, N_GROUPS, D_STATE)
    dtblk = (2 * DI + 2 * DBC) // LANES
    dt, acs = _dt_prep(tag + "_dtprep", zx, dtblk, W["dt_bias"][j], W["a_log"][j])
    dt_col, acs_col = _cols(dt, N_GROUPS), _cols(acs, N_GROUPS)
    acs_row = acs_col.transpose(0, 2, 1)
    yf, stf = _ssd_fwd(tag + "_scanf", xs, Bg, Cg, dt_col, acs_col, acs_row, 0)
    yb, stb = _ssd_fwd(tag + "_scanb", xs, Bg, Cg, dt_col, acs_col, acs_row, 1)
    gyn = _gate_fwd(tag + "_gate", yf, yb, xs, zx, W["dvec"][j], W["norm_w"][j])
    mix = _mm_act(tag + "_outproj", gyn, W["wout"][j], _RowShard(DI // 4, D), T, D, DI, [F32])[0]
    saved = dict(zx=zx, xs=xs, Bg=Bg, Cg=Cg, dt_col=dt_col, acs_col=acs_col, acs_row=acs_row, stf=stf, stb=stb,
                 yf=yf, yb=yb, gyn=gyn, dtblk=dtblk)
    return mix, saved


def _ssd_bwd_layer(tag, xb, dr1, dr1b, W, j, s, small):
    T, D = dr1.shape
    DI = 2 * D
    DBC = N_GROUPS * D_STATE
    win = W["win"][j]
    NZ = win.shape[1]
    zx = s["zx"]
    rs = _RowShard(DI // 4, D)
    dgyn = _mm_act(tag + "_dgyn", dr1b, W["wout"][j], rs, T, DI, D, [F32], nt=True)[0]
    p_out = _mm_wgrad(tag + "_dwout", s["gyn"], dr1b, DI, D, T, _sds((4, DI // 4, D), BF16), rs.out)
    dvec, nw = W["dvec"][j], W["norm_w"][j]
    dy, dzb, dnw, ddl = _gate_bwd(tag + "_dgate", dgyn, s["yf"], s["yb"], s["xs"], zx, dvec, nw)
    res = []
    for d, st in ((0, s["stf"]), (1, s["stb"])):
        res.append(_ssd_bwd(tag + "_dscan%d" % d, dy, s["xs"], s["Bg"], s["Cg"], s["dt_col"], s["acs_col"],
                            s["acs_row"], st, d))
    nat = lambda k: jnp.concatenate([res[d][k].transpose(1, 0, 2).reshape(T, -1) for d in (0, 1)], axis=1)
    drawb, dbias, dal = _dt_bwd(tag + "_ddt", nat(3), nat(4), zx, s["dtblk"], W["dt_bias"][j], W["a_log"][j])
    zb = DI // LANES
    cw, cb = W["conv_w"][j], W["conv_b"][j]
    dxx, dbx, dwx = _conv_bwd(tag + "_dconvx", zx, zb, 0, cw, cb, [res[0][0], res[1][0]], (dy, dvec))
    dxb, dbb, dwb = _conv_bwd(tag + "_dconvb", zx, zb + DI // LANES, DI // LANES, cw, cb, [res[0][1], res[1][1]])
    dxc, dbc, dwc = _conv_bwd(tag + "_dconvc", zx, zb + (DI + DBC) // LANES, (DI + DBC) // LANES, cw, cb,
                              [res[0][2], res[1][2]])
    dzx = jnp.concatenate([dzb, dxx, dxb, dxc, drawb], axis=1)
    dwin = _mm_wgrad(tag + "_dwin", xb, dzx, D, NZ, T, _sds((D, NZ), BF16),
                     lambda tm, tn: pl.BlockSpec((tm, tn), lambda i, jj, k: (i, jj)), tn_pref=1152)
    p_in = dwin.reshape(D, 4, NZ // 4).transpose(1, 0, 2)
    dx = _mm_act(tag + "_dxin", dzx, win, None, T, D, NZ, [F32], epi=_resid_epi, extras=(dr1,), nt=True,
                 tk_pref=1152)[0]
    small["conv_w"].append(jnp.concatenate([p[k] for k in range(CONV_WIDTH) for p in (dwx, dwb, dwc)], axis=1))
    small["conv_b"].append(jnp.concatenate([dbx, dbb, dbc], axis=1))
    small["dt_bias"].append(dbias)
    small["a_log"].append(dal)
    small["d"].append(ddl)
    small["norm_w"].append(dnw)
    return dx, p_in, p_out


def _pool_bwd_layer(tag, dr1, W, j, s, small):
    T, D = dr1.shape
    ng = len(POOL_WINDOWS)
    dg = D // ng
    dm, dypb, dsc, dbi = _pool_bwd_a(tag + "_dpool", dr1, s["ypre"], W["pool_scale"][j], W["wp"][j])
    tk = _tile(T, 1024)
    dwp = _mm(tag + "_dwp", s["m"], dypb, pl.BlockSpec((tk, dg), lambda i, jj, k: (k, i)),
              pl.BlockSpec((tk, dg), lambda i, jj, k: (k, i)), TN, (ng, 1, T // tk), dg, dg,
              [_sds((ng, dg, dg), BF16)], [pl.BlockSpec((None, dg, dg), lambda i, jj, k: (i, 0, 0))])[0]
    p_pool = dwp.reshape(ng, 4, dg // 4, dg).transpose(1, 0, 2, 3).reshape(4, dg, dg)
    dx = _pool_bwd_win(tag + "_dwin", dm, dr1)
    small["pool_b"].append(dbi)
    small["pool_scale"].append(dsc)
    return dx, p_pool


SMALL_NAMES = ("conv_w", "conv_b", "dt_bias", "a_log", "d", "norm_w", "pool_b", "pool_scale",
               "ln_mix_g", "ln_mix_b", "ln_ffn_g", "ln_ffn_b")


def _local_step(x, tgt, W):
    T, D = x.shape
    DFF = 4 * D
    cs, rs = _ColShard(D, DFF // 4), _RowShard(DFF // 4, D)
    saved = []
    xb = _bf(x)
    for i in range(DEPTH):
        j = i // 2
        tag = "L%d" % i
        s = dict(x=x, xb=xb)
        if i % 2 == 0:
            mix, ss = _ssd_fwd_layer(tag, x, xb, W, j)
            s.update(ss)
            r1, x1, x1b = _resln(tag + "_lnmix", x, mix, W["ln_mix_g"][i], W["ln_mix_b"][i])
        else:
            m = _pool_m(tag + "_poolm", x)
            ypre, r1, x1, x1b = _pool_fwd(tag + "_pool", m, x, W["wp"][j], W["pool_b"][j], W["pool_scale"][j],
                                          W["ln_mix_g"][i], W["ln_mix_b"][i])
            s.update(m=m, ypre=ypre)
        h, a = _mm_act(tag + "_mlp1", x1b, W["w1"][i], cs, T, DFF, D, [F32, BF16], epi=_relu2_epi)
        mlp = _mm_act(tag + "_mlp2", a, W["w2"][i], rs, T, D, DFF, [F32])[0]
        r2, x2, x2b = _resln(tag + "_lnffn", x1, mlp, W["ln_ffn_g"][i], W["ln_ffn_b"][i])
        s.update(r1=r1, x1=x1, x1b=x1b, h=h, a=a, r2=r2)
        saved.append(s)
        x, xb = x2, x2b

    dx, loss = _loss_head("loss", x, tgt)
    small = {n: [] for n in SMALL_NAMES}
    P = dict(win=[], wout=[], w1=[], w2=[], wp=[])
    for i in reversed(range(DEPTH)):
        j = i // 2
        tag = "L%d" % i
        s = saved[i]
        dr2, dr2b, dg2, db2 = _lnbwd(tag + "_dlnffn", dx, s["r2"], W["ln_ffn_g"][i])
        dh = _mm_act(tag + "_dh", dr2b, W["w2"][i], rs, T, DFF, D, [BF16], epi=_dh_epi, extras=(s["h"],), nt=True)[0]
        P["w2"].append(_mm_wgrad(tag + "_dw2", s["a"], dr2b, DFF, D, T, _sds((4, DFF // 4, D), BF16), rs.out))
        P["w1"].append(_mm_wgrad(tag + "_dw1", s["x1b"], dh, D, DFF, T, _sds((4, D, DFF // 4), BF16), cs.out))
        dx1 = _mm_act(tag + "_dx1", dh, W["w1"][i], cs, T, D, DFF, [F32], epi=_resid_epi, extras=(dr2,), nt=True)[0]
        dr1, dr1b, dg1, db1 = _lnbwd(tag + "_dlnmix", dx1, s["r1"], W["ln_mix_g"][i])
        if i % 2 == 0:
            dx, p_in, p_out = _ssd_bwd_layer(tag, s["xb"], dr1, dr1b, W, j, s, small)
            P["win"].append(p_in)
            P["wout"].append(p_out)
        else:
            dx, p_pool = _pool_bwd_layer(tag, dr1, W, j, s, small)
            P["wp"].append(p_pool)
        small["ln_ffn_g"].append(dg2)
        small["ln_ffn_b"].append(db2)
        small["ln_mix_g"].append(dg1)
        small["ln_mix_b"].append(db1)
    P = {k: v[::-1] for k, v in P.items()}
    small = {k: jnp.concatenate(v[::-1], axis=1) for k, v in small.items()}
    small["d"] = _head_sum("dD", small["d"])
    return loss, dx, P, small


ANY = pl.BlockSpec(memory_space=pl.ANY)


def _pos():
    return lax.axis_index("x"), lax.axis_index("y"), lax.axis_index("c")


def _other_chips(x, y):
    return [(1 - x, y), (x, 1 - y), (1 - x, 1 - y)]


def _rcopy(src, dst, ssem, rsem, dev):
    return pltpu.make_async_remote_copy(src_ref=src, dst_ref=dst, send_sem=ssem, recv_sem=rsem,
                                        device_id=dev, device_id_type=MESH)


def _gather(name, slabs, split):
    n = len(slabs)

    def body(*refs):
        src, out = refs[:n], refs[n:2 * n]
        ssem, rsem, fssem, frsem, lsem = refs[2 * n:]
        x, y, c = _pos()
        chip = 2 * x + y
        chips = _other_chips(x, y)
        sib = (x, y, 1 - c)

        def mine(t, half):
            if split[t]:
                h = slabs[t].shape[0] // 2
                return src[t].at[pl.ds(half * h, h)]
            return src[t]

        def region(t, ch, half):
            if split[t]:
                h = slabs[t].shape[0] // 2
                return out[t].at[ch, pl.ds(half * h, h)]
            return out[t].at[ch]

        local = [pltpu.make_async_copy(src[t], out[t].at[chip], lsem.at[t]) for t in range(n)]
        for cp in local:
            cp.start()
        sends = []
        for t in range(n):
            for j, (px, py) in enumerate(chips):
                cp = _rcopy(mine(t, c), region(t, chip, c), ssem.at[t, j], rsem.at[t, j], (px, py, c))
                cp.start()
                sends.append(cp)
        for t in range(n):
            for j, (px, py) in enumerate(chips):
                pch = 2 * px + py
                _rcopy(mine(t, c), region(t, pch, c), ssem.at[t, j], rsem.at[t, j], (px, py, c)).wait_recv()
                if split[t]:
                    cp = _rcopy(region(t, pch, c), region(t, pch, c), fssem.at[t, j], frsem.at[t, j], sib)
                    cp.start()
                    sends.append(cp)
        for t in range(n):
            if split[t]:
                for j, (px, py) in enumerate(chips):
                    pch = 2 * px + py
                    _rcopy(region(t, pch, 1 - c), region(t, pch, 1 - c), fssem.at[t, j], frsem.at[t, j],
                           sib).wait_recv()
        for cp in sends:
            cp.wait_send()
        for cp in local:
            cp.wait()

    sem = pltpu.SemaphoreType.DMA
    return pl.pallas_call(
        body, name=name, in_specs=[ANY] * n, out_specs=[ANY] * n,
        out_shape=[_sds((4,) + s.shape, s.dtype) for s in slabs],
        scratch_shapes=[sem((n, 3)), sem((n, 3)), sem((n, 3)), sem((n, 3)), sem((n,))])(*slabs)


def _rs1(name, Ps):
    n = len(Ps)

    def body(*refs):
        src, out = refs[:n], refs[n:2 * n]
        ssem, rsem = refs[2 * n:]
        x, y, c = _pos()
        cps = []
        for t in range(n):
            h = Ps[t].shape[1] // 2
            cp = _rcopy(src[t].at[pl.ds(0, 4), pl.ds((1 - c) * h, h)], out[t], ssem.at[t], rsem.at[t], (x, y, 1 - c))
            cp.start()
            cps.append(cp)
        for cp in cps:
            cp.wait()

    sem = pltpu.SemaphoreType.DMA
    return pl.pallas_call(
        body, name=name, in_specs=[ANY] * n, out_specs=[ANY] * n,
        out_shape=[_sds((4, p.shape[1] // 2, p.shape[2]), p.dtype) for p in Ps],
        scratch_shapes=[sem((n,)), sem((n,))])(*Ps)


def _rs2(name, Qs):
    n = len(Qs)

    def body(*refs):
        src, out = refs[:n], refs[n:2 * n]
        ssem, rsem = refs[2 * n:]
        x, y, c = _pos()
        cps = []
        for t in range(n):
            for j, (px, py) in enumerate(_other_chips(x, y)):
                cp = _rcopy(src[t].at[2 * px + py], out[t].at[j], ssem.at[t, j], rsem.at[t, j], (px, py, c))
                cp.start()
                cps.append(cp)
        for cp in cps:
            cp.wait()

    sem = pltpu.SemaphoreType.DMA
    return pl.pallas_call(
        body, name=name, in_specs=[ANY] * n, out_specs=[ANY] * n,
        out_shape=[_sds((3,) + q.shape[1:], q.dtype) for q in Qs],
        scratch_shapes=[sem((n, 3)), sem((n, 3))])(*Qs)


def _rs3(name, Fs):
    n = len(Fs)
    h, B = Fs[0].shape

    def body(*refs):
        src, out = refs[:n], refs[n]
        ssem, rsem, lsem = refs[n + 1:]
        x, y, c = _pos()
        cps = []
        for l in range(n):
            dst = out.at[l, pl.ds(c * h, h)]
            lc = pltpu.make_async_copy(src[l], dst, lsem.at[l])
            lc.start()
            cp = _rcopy(src[l], dst, ssem.at[l], rsem.at[l], (x, y, 1 - c))
            cp.start()
            cps += [lc, cp]
        for cp in cps:
            cp.wait()

    sem = pltpu.SemaphoreType.DMA
    return pl.pallas_call(
        body, name=name, in_specs=[ANY] * n, out_specs=ANY, out_shape=_sds((n, 2 * h, B), Fs[0].dtype),
        scratch_shapes=[sem((n,)), sem((n,)), sem((n,))])(*Fs)


def _allgather_small(name, v):
    def body(v_ref, out_ref, ssem, rsem, lsem):
        x, y, c = _pos()
        me = 4 * x + 2 * y + c
        lc = pltpu.make_async_copy(v_ref, out_ref.at[me], lsem)
        lc.start()
        cps = []
        for k in range(1, 8):
            flip = lambda a, bit: (1 - a) if bit else a
            peer = (flip(x, k & 4), flip(y, k & 2), flip(c, k & 1))
            cp = _rcopy(v_ref, out_ref.at[me], ssem.at[k - 1], rsem.at[k - 1], peer)
            cp.start()
            cps.append(cp)
        for cp in cps:
            cp.wait()
        lc.wait()

    sem = pltpu.SemaphoreType.DMA
    return pl.pallas_call(body, name=name, in_specs=[ANY], out_specs=ANY, out_shape=_sds((8,) + v.shape, v.dtype),
                          scratch_shapes=[sem((7,)), sem((7,)), sem])(v)


def _row_tile(R, C, mult):
    return _tile(R, max(mult, (1 << 19) // C), mult)


def _sum1(name, P, R1, c):
    _, A, B = P.shape
    h = A // 2
    ta = _row_tile(h, B, 16)
    nb = h // ta

    def kern(c_ref, p_ref, r_ref, q_ref):
        q_ref[...] = _bf(p_ref[...].astype(F32) + r_ref[...].astype(F32))

    gs = pltpu.PrefetchScalarGridSpec(
        num_scalar_prefetch=1, grid=(4, nb),
        in_specs=[pl.BlockSpec((1, ta, B), lambda s, i, c_ref: (s, c_ref[0] * nb + i, 0)),
                  pl.BlockSpec((1, ta, B), lambda s, i, c_ref: (s, i, 0))],
        out_specs=pl.BlockSpec((1, ta, B), lambda s, i, c_ref: (s, i, 0)))
    return pl.pallas_call(kern, name=name, grid_spec=gs, out_shape=_sds((4, h, B), BF16),
                          compiler_params=_cp(("parallel", "parallel")))(c.reshape(1), P, R1)


def _sum2(name, Q, R2, chip):
    _, h, B = Q.shape
    ta = _row_tile(h, B, 16)

    def kern(s_ref, q_ref, r_ref, f_ref):
        f = q_ref[0].astype(F32)
        for j in range(3):
            f = f + r_ref[j].astype(F32)
        f_ref[...] = f

    gs = pltpu.PrefetchScalarGridSpec(
        num_scalar_prefetch=1, grid=(h // ta,),
        in_specs=[pl.BlockSpec((1, ta, B), lambda i, s_ref: (s_ref[0], i, 0)),
                  pl.BlockSpec((3, ta, B), lambda i, s_ref: (0, i, 0))],
        out_specs=pl.BlockSpec((ta, B), lambda i, s_ref: (i, 0)))
    return pl.pallas_call(kern, name=name, grid_spec=gs, out_shape=_sds((h, B), F32),
                          compiler_params=_cp(("parallel",)))(chip.reshape(1), Q, R2)


def _adam(name, w, g, m, v):
    R, C = w.shape
    tr = _row_tile(R, C, 8)

    def kern(w_ref, g_ref, m_ref, v_ref, d_ref, mo_ref, vo_ref):
        gv = g_ref[...]
        mn = ADAM_B1 * m_ref[...] + (1.0 - ADAM_B1) * gv
        vn = ADAM_B2 * v_ref[...] + (1.0 - ADAM_B2) * jnp.square(gv)
        m_hat = mn / (1.0 - ADAM_B1 ** ADAM_STEP)
        v_hat = vn / (1.0 - ADAM_B2 ** ADAM_STEP)
        d_ref[...] = -ADAM_LR * (m_hat / (jnp.sqrt(v_hat) + ADAM_EPS) + ADAM_WD * w_ref[...])
        mo_ref[...] = mn
        vo_ref[...] = vn

    blk = pl.BlockSpec((tr, C), lambda i: (i, 0))
    return pl.pallas_call(kern, name=name, grid=(R // tr,), in_specs=[blk] * 4, out_specs=[blk] * 3,
                          out_shape=[_sds((R, C), F32)] * 3, compiler_params=_cp(("parallel",)))(w, g, m, v)


def _rowsum8(name, v):
    n = v.shape[1]
    tn = _tile(n, 16384)

    def kern(v_ref, o_ref):
        o_ref[...] = jnp.sum(v_ref[...], axis=0, keepdims=True)

    return pl.pallas_call(kern, name=name, grid=(n // tn,), in_specs=[pl.BlockSpec((8, tn), lambda i: (0, i))],
                          out_specs=pl.BlockSpec((1, tn), lambda i: (0, i)), out_shape=_sds((1, n), F32))(v)


def _sum_devices(name, v):
    n = v.shape[2]
    tn = _tile(n, 4096)

    def kern(v_ref, o_ref):
        s = v_ref[0]
        for d in range(1, 8):
            s = s + v_ref[d]
        o_ref[...] = s

    return pl.pallas_call(kern, name=name, grid=(n // tn,), in_specs=[pl.BlockSpec((8, 8, tn), lambda i: (0, 0, i))],
                          out_specs=pl.BlockSpec((8, tn), lambda i: (0, i)), out_shape=_sds((8, n), F32))(v)


def _pack8(parts, quantum=8 * LANES):
    flat = jnp.concatenate([p.reshape(-1) for p in parts])
    n = flat.shape[0]
    npad = -n % quantum
    return jnp.pad(flat, (0, npad)).reshape(8, -1), n


def _unpack(flat, shapes):
    out, o = [], 0
    for s in shapes:
        k = 1
        for d in s:
            k *= d
        out.append(flat[o:o + k].reshape(s))
        o += k
    return out


def kernel(x, ssd_in_proj, ssd_conv_w, ssd_conv_b, ssd_dt_bias, ssd_A_log, ssd_D, ssd_norm_w, ssd_out_proj, pool_w, pool_b, pool_scale, mlp_w1, mlp_w2, ln_mix_g, ln_mix_b, ln_ffn_g, ln_ffn_b, loss_target, m_ssd_in_proj, m_ssd_conv_w, m_ssd_conv_b, m_ssd_dt_bias, m_ssd_A_log, m_ssd_D, m_ssd_norm_w, m_ssd_out_proj, m_pool_w, m_pool_b, m_pool_scale, m_mlp_w1, m_mlp_w2, m_ln_mix_g, m_ln_mix_b, m_ln_ffn_g, m_ln_ffn_b, v_ssd_in_proj, v_ssd_conv_w, v_ssd_conv_b, v_ssd_dt_bias, v_ssd_A_log, v_ssd_D, v_ssd_norm_w, v_ssd_out_proj, v_pool_w, v_pool_b, v_pool_scale, v_mlp_w1, v_mlp_w2, v_ln_mix_g, v_ln_mix_b, v_ln_ffn_g, v_ln_ffn_b):
    _, T, D = x.shape
    DI, DFF = 2 * D, 4 * D
    NZ = 4 * ssd_in_proj.shape[2]
    nssd, npool = ssd_in_proj.shape[0], pool_w.shape[0]
    ng = len(POOL_WINDOWS)
    dg = D // ng
    xi, yi, ci = _pos()
    chip = 2 * xi + yi

    in_b, out_b, w1_b, w2_b, pw_b = (a.astype(BF16) for a in (ssd_in_proj, ssd_out_proj, mlp_w1, mlp_w2, pool_w))
    g_in, g_out, g_pw, g_w1, g_w2 = [], [], [], [], []
    for i in range(DEPTH):
        j = i // 2
        if i % 2 == 0:
            a, b, c1, c2 = _gather("gather_L%d" % i, [in_b[j], out_b[j], w1_b[i], w2_b[i]], [True] * 4)
            g_in.append(a.transpose(1, 0, 2).reshape(D, NZ))
            g_out.append(b)
        else:
            a, c1, c2 = _gather("gather_L%d" % i, [pw_b[j], w1_b[i], w2_b[i]], [True] * 3)
            g_pw.append(a.transpose(1, 0, 2, 3).reshape(ng, dg, dg))
        g_w1.append(c1)
        g_w2.append(c2)
    g_cw, g_pb, g_ps = _gather("gather_small", [ssd_conv_w, pool_b, pool_scale], [False] * 3)
    W = dict(
        win=g_in, wout=g_out, w1=g_w1, w2=g_w2, wp=g_pw,
        conv_w=[g_cw[:, j, :, 0, :].transpose(1, 0, 2).reshape(CONV_WIDTH, -1) for j in range(nssd)],
        conv_b=[ssd_conv_b[j].reshape(1, -1) for j in range(nssd)],
        dt_bias=[ssd_dt_bias[j].reshape(1, -1) for j in range(nssd)],
        a_log=[ssd_A_log[j].reshape(1, -1) for j in range(nssd)],
        dvec=[jnp.repeat(ssd_D[j], HEAD_DIM).reshape(1, -1) for j in range(nssd)],
        norm_w=[ssd_norm_w[j].reshape(1, -1) for j in range(nssd)],
        pool_b=[g_pb[:, j].transpose(1, 0, 2).reshape(1, -1) for j in range(npool)],
        pool_scale=[g_ps[:, j].reshape(1, -1) for j in range(npool)],
        ln_mix_g=[ln_mix_g[i].reshape(1, -1) for i in range(DEPTH)],
        ln_mix_b=[ln_mix_b[i].reshape(1, -1) for i in range(DEPTH)],
        ln_ffn_g=[ln_ffn_g[i].reshape(1, -1) for i in range(DEPTH)],
        ln_ffn_b=[ln_ffn_b[i].reshape(1, -1) for i in range(DEPTH)],
    )

    loss_blk, dx, P, small = _local_step(x[0], loss_target[0], W)
    loss = lax.psum(loss_blk[0, 0], ("x", "y", "c"))

    Fh = dict(win=[], wout=[], w1=[], w2=[], wp=[])
    for i in range(DEPTH):
        j = i // 2
        keys = ["win", "wout", "w1", "w2"] if i % 2 == 0 else ["wp", "w1", "w2"]
        units = [P[k][j if k in ("win", "wout", "wp") else i] for k in keys]
        R1 = _rs1("rs1_L%d" % i, units)
        Q = [_sum1("sum1_L%d_%s" % (i, k), p, r, ci) for k, p, r in zip(keys, units, R1)]
        R2 = _rs2("rs2_L%d" % i, Q)
        for k, q, r in zip(keys, Q, R2):
            Fh[k].append(_sum2("sum2_L%d_%s" % (i, k), q, r, chip))
    big = [("win", ssd_in_proj, m_ssd_in_proj, v_ssd_in_proj), ("wout", ssd_out_proj, m_ssd_out_proj, v_ssd_out_proj),
           ("wp", pool_w, m_pool_w, v_pool_w), ("w1", mlp_w1, m_mlp_w1, v_mlp_w1), ("w2", mlp_w2, m_mlp_w2, v_mlp_w2)]
    res = {}
    for k, w, m, v in big:
        g = _rs3("rs3_" + k, Fh[k])
        C = g.shape[-1]
        d2, m2, v2 = _adam("adam_" + k, w.reshape(-1, C), g.reshape(-1, C), m.reshape(-1, C), v.reshape(-1, C))
        res[k] = tuple(a.reshape(w.shape) for a in (g, d2, m2, v2))

    flat8 = jnp.concatenate([small[n] for n in SMALL_NAMES], axis=1)
    ns = flat8.shape[1]
    flat8 = jnp.pad(flat8, ((0, 0), (0, -ns % (8 * LANES))))
    mine8 = _rowsum8("small_rowsum", flat8).reshape(8, -1)
    tot = _sum_devices("small_sum", _allgather_small("small_allgather", mine8)).reshape(-1)
    sw = [ssd_conv_w, ssd_conv_b, ssd_dt_bias, ssd_A_log, ssd_D, ssd_norm_w, pool_b, pool_scale,
          ln_mix_g, ln_mix_b, ln_ffn_g, ln_ffn_b]
    sm = [m_ssd_conv_w, m_ssd_conv_b, m_ssd_dt_bias, m_ssd_A_log, m_ssd_D, m_ssd_norm_w, m_pool_b, m_pool_scale,
          m_ln_mix_g, m_ln_mix_b, m_ln_ffn_g, m_ln_ffn_b]
    sv = [v_ssd_conv_w, v_ssd_conv_b, v_ssd_dt_bias, v_ssd_A_log, v_ssd_D, v_ssd_norm_w, v_pool_b, v_pool_scale,
          v_ln_mix_g, v_ln_mix_b, v_ln_ffn_g, v_ln_ffn_b]
    full_shapes = [(nssd, CONV_WIDTH, 1, DI + 2 * N_GROUPS * D_STATE)] + [w.shape for w in sw[1:6]] \
        + [(npool, ng, dg), (npool, D)] + [w.shape for w in sw[8:]]
    sg = _unpack(tot, full_shapes)
    sg[0] = lax.dynamic_slice_in_dim(sg[0], chip * sw[0].shape[3], sw[0].shape[3], axis=3)
    sg[6] = lax.dynamic_slice_in_dim(sg[6], chip * sw[6].shape[2], sw[6].shape[2], axis=2)
    sg[7] = lax.dynamic_slice_in_dim(sg[7], chip * sw[7].shape[1], sw[7].shape[1], axis=1)
    packs = [_pack8(parts)[0] for parts in (sw, sg, sm, sv)]
    sd, smn, svn = _adam("adam_small", *packs)
    shapes = [w.shape for w in sw]
    sd, smn, svn = (_unpack(a.reshape(-1), shapes) for a in (sd, smn, svn))

    order = ["win", 0, 1, 2, 3, 4, 5, "wout", "wp", 6, 7, "w1", "w2", 8, 9, 10, 11]
    outs = [loss, dx.reshape(x.shape)]
    for slot, small_vals in ((0, sg), (1, sd), (2, smn), (3, svn)):
        for o in order:
            outs.append(res[o][slot] if isinstance(o, str) else small_vals[o])
    return tuple(outs)
```

```python
import functools

import jax
import jax.numpy as jnp
from jax import lax
from jax.experimental import pallas as pl
from jax.experimental.pallas import tpu as pltpu

F32 = jnp.float32
BF16 = jnp.bfloat16

HEAD_DIM = 64
N_GROUPS = 8
D_STATE = 128
CHUNK = 128
CONV_WIDTH = 5
POOL_WINDOWS = (2, 4, 8, 16)
DEPTH = 4
ALPHA = (2.0 * DEPTH) ** 0.25
LN_EPS = 1e-5
RMS_EPS = 1e-5
ADAM_LR, ADAM_B1, ADAM_B2, ADAM_EPS, ADAM_WD, ADAM_STEP = 0.001, 0.9, 0.999, 1e-08, 0.01, 10

LANES = 128
VMEM_LIMIT = 48 * 1024 * 1024
NEG = -1e30
MESH = pl.DeviceIdType.MESH

NN = (((1,), (0,)), ((), ()))
NT = (((1,), (1,)), ((), ()))
TN = (((0,), (0,)), ((), ()))


def _tile(dim, pref, mult=LANES):
    if dim <= pref:
        return dim
    t = (pref // mult) * mult
    while t > mult and dim % t:
        t -= mult
    assert dim % t == 0, (dim, pref)
    return t


def _cp(sem):
    return pltpu.CompilerParams(dimension_semantics=sem, vmem_limit_bytes=VMEM_LIMIT)


def _sds(shape, dtype):
    return jax.ShapeDtypeStruct(tuple(shape), dtype)


def _dot(a, b, dn=NN):
    return lax.dot_general(a, b, dn, preferred_element_type=F32)


def _bf(x):
    return x.astype(BF16)


def _sigmoid(x):
    return 1.0 / (1.0 + jnp.exp(-x))


def _mm(name, a, b, a_spec, b_spec, dn, grid, tm, tn, out_shapes, out_specs, epi=None, extras=(), extra_specs=()):
    nk = grid[2]
    n_ex, n_out = len(extras), len(out_shapes)

    def kern(*refs):
        a_ref, b_ref = refs[0], refs[1]
        ex = refs[2:2 + n_ex]
        outs = refs[2 + n_ex:2 + n_ex + n_out]
        acc = refs[-1]
        k = pl.program_id(2)

        @pl.when(k == 0)
        def _():
            acc[...] = jnp.zeros_like(acc)

        acc[...] += _dot(_bf(a_ref[...]), _bf(b_ref[...]), dn)

        @pl.when(k == nk - 1)
        def _():
            res = epi(acc[...], *[e[...] for e in ex]) if epi is not None else (acc[...],)
            for o, r in zip(outs, res):
                o[...] = r.astype(o.dtype)

    return pl.pallas_call(
        kern, name=name, grid=grid, in_specs=[a_spec, b_spec, *extra_specs], out_specs=list(out_specs),
        out_shape=list(out_shapes), scratch_shapes=[pltpu.VMEM((tm, tn), F32)],
        compiler_params=_cp(("parallel", "parallel", "arbitrary")))(a, b, *extras)


class _ColShard:
    def __init__(self, R, C):
        self.R, self.C = R, C

    def b_nn(self, tk, tn):
        n = self.C // tn
        return pl.BlockSpec((None, tk, tn), lambda i, j, k: (j // n, k, j % n))

    def b_nt(self, tn, tk):
        n = self.C // tk
        return pl.BlockSpec((None, tn, tk), lambda i, j, k: (k // n, j, k % n))

    def out(self, tm, tn):
        n = self.C // tn
        return pl.BlockSpec((None, tm, tn), lambda i, j, k: (j // n, i, j % n))


class _RowShard:
    def __init__(self, R, C):
        self.R, self.C = R, C

    def b_nn(self, tk, tn):
        n = self.R // tk
        return pl.BlockSpec((None, tk, tn), lambda i, j, k: (k // n, k % n, j))

    def b_nt(self, tn, tk):
        n = self.R // tn
        return pl.BlockSpec((None, tn, tk), lambda i, j, k: (j // n, j % n, k))

    def out(self, tm, tn):
        n = self.R // tm
        return pl.BlockSpec((None, tm, tn), lambda i, j, k: (i // n, i % n, j))


def _a_nn(tm, tk):
    return pl.BlockSpec((tm, tk), lambda i, j, k: (i, k))


def _a_tn(tk, tm):
    return pl.BlockSpec((tk, tm), lambda i, j, k: (k, i))


def _b_tn(tk, tn):
    return pl.BlockSpec((tk, tn), lambda i, j, k: (k, j))


def _o_ij(tm, tn):
    return pl.BlockSpec((tm, tn), lambda i, j, k: (i, j))


def _mm_act(name, a, w, wspec, M, N, K, out_dtypes, epi=None, extras=(), nt=False, tn_pref=1024, tk_pref=1024):
    tm = _tile(M, 1024)
    if isinstance(wspec, (_ColShard, _RowShard)):
        nlim, klim = (wspec.R, wspec.C) if nt else (wspec.C, wspec.R)
        tn = _tile(nlim, tn_pref)
        tk = _tile(klim, tk_pref)
        b_spec = wspec.b_nt(tn, tk) if nt else wspec.b_nn(tk, tn)
    else:
        tn = _tile(N, tn_pref)
        tk = _tile(K, tk_pref)
        b_spec = (pl.BlockSpec((tn, tk), lambda i, j, k: (j, k)) if nt
                  else pl.BlockSpec((tk, tn), lambda i, j, k: (k, j)))
    grid = (M // tm, N // tn, K // tk)
    outs = [_sds((M, N), dt) for dt in out_dtypes]
    return _mm(name, a, w, _a_nn(tm, tk), b_spec, NT if nt else NN, grid, tm, tn, outs,
               [_o_ij(tm, tn)] * len(outs), epi, extras, [_o_ij(tm, tn)] * len(extras))


def _mm_wgrad(name, a, b, M, N, K, out_shape, out_spec_fn, tm_pref=1024, tn_pref=1024):
    tm = _tile(M, tm_pref)
    tn = _tile(N, tn_pref)
    tk = _tile(K, 1024)
    grid = (M // tm, N // tn, K // tk)
    return _mm(name, a, b, _a_tn(tk, tm), _b_tn(tk, tn), TN, grid, tm, tn, [out_shape], [out_spec_fn(tm, tn)])[0]


def _ln_stats(r):
    mu = jnp.mean(r, axis=-1, keepdims=True)
    xc = r - mu
    var = jnp.mean(xc * xc, axis=-1, keepdims=True)
    return xc, lax.rsqrt(var + LN_EPS)


def _part8(v):
    return v.reshape(v.shape[0] // 8, 8, v.shape[1]).sum(axis=0)


def _acc_out(ref, val, first):
    @pl.when(first)
    def _():
        ref[...] = val

    @pl.when(jnp.logical_not(first))
    def _():
        ref[...] += val


def _resln(name, x, mix, g, b):
    T, D = x.shape
    tr = _tile(T, 256, 8)

    def kern(x_ref, m_ref, g_ref, b_ref, r_ref, y_ref, yb_ref):
        r = ALPHA * x_ref[...] + m_ref[...]
        xc, rstd = _ln_stats(r)
        y = xc * rstd * g_ref[...] + b_ref[...]
        r_ref[...] = r
        y_ref[...] = y
        yb_ref[...] = _bf(y)

    row = pl.BlockSpec((tr, D), lambda i: (i, 0))
    vec = pl.BlockSpec((1, D), lambda i: (0, 0))
    return pl.pallas_call(kern, name=name, grid=(T // tr,), in_specs=[row, row, vec, vec], out_specs=[row, row, row],
                          out_shape=[_sds((T, D), F32), _sds((T, D), F32), _sds((T, D), BF16)],
                          compiler_params=_cp(("parallel",)))(x, mix, g, b)


def _lnbwd(name, dy, r, g):
    T, D = r.shape
    tr = _tile(T, 256, 8)

    def kern(dy_ref, r_ref, g_ref, dr_ref, drb_ref, dg_ref, db_ref):
        dyv = dy_ref[...]
        xc, rstd = _ln_stats(r_ref[...])
        xh = xc * rstd
        dxh = dyv * g_ref[...]
        m1 = jnp.mean(dxh, axis=-1, keepdims=True)
        m2 = jnp.mean(dxh * xh, axis=-1, keepdims=True)
        dr = rstd * (dxh - m1 - xh * m2)
        dr_ref[...] = dr
        drb_ref[...] = _bf(dr)
        first = pl.program_id(0) == 0
        _acc_out(dg_ref, _part8(dyv * xh), first)
        _acc_out(db_ref, _part8(dyv), first)

    row = pl.BlockSpec((tr, D), lambda i: (i, 0))
    vec = pl.BlockSpec((1, D), lambda i: (0, 0))
    acc = pl.BlockSpec((8, D), lambda i: (0, 0))
    return pl.pallas_call(kern, name=name, grid=(T // tr,), in_specs=[row, row, vec], out_specs=[row, row, acc, acc],
                          out_shape=[_sds((T, D), F32), _sds((T, D), BF16), _sds((8, D), F32), _sds((8, D), F32)],
                          compiler_params=_cp(("arbitrary",)))(dy, r, g)


def _loss_head(name, y, tgt):
    T, D = y.shape
    tr = _tile(T, 256, 8)
    nt = T // tr

    def kern(y_ref, t_ref, dy_ref, loss_ref, acc):
        i = pl.program_id(0)
        e = y_ref[...] - t_ref[...]
        dy_ref[...] = e * (1.0 / D)
        _acc_out(acc, _part8(e * e), i == 0)

        @pl.when(i == nt - 1)
        def _():
            tot = jnp.sum(jnp.sum(acc[...], axis=1, keepdims=True), axis=0, keepdims=True)
            loss_ref[...] = jnp.broadcast_to(tot * (0.5 / D), loss_ref.shape)

    row = pl.BlockSpec((tr, D), lambda i: (i, 0))
    return pl.pallas_call(kern, name=name, grid=(nt,), in_specs=[row, row],
                          out_specs=[row, pl.BlockSpec((8, LANES), lambda i: (0, 0))],
                          out_shape=[_sds((T, D), F32), _sds((8, LANES), F32)],
                          scratch_shapes=[pltpu.VMEM((8, D), F32)], compiler_params=_cp(("arbitrary",)))(y, tgt)


def _shift(x, o):
    if o == 0:
        return x
    T = x.shape[0]
    rolled = pltpu.roll(x, (-o) % T, 0)
    t = lax.broadcasted_iota(jnp.int32, x.shape, 0)
    return jnp.where((t + o >= 0) & (t + o < T), rolled, 0.0)


def _run(u, h, step):
    s, k = u, 1
    while k < h:
        s = s + _shift(s, step * k)
        k *= 2
    return s


def _winsum(u, win, transposed):
    h = win // 2
    if not transposed:
        return _run(u, h, 1) + _shift(_run(u, h, -1), -1)
    return _run(u, h, -1) + _shift(_run(u, h, 1), 1)


def _wincount(shape, win):
    t = lax.broadcasted_iota(jnp.int32, shape, 0)
    T = shape[0]
    lo = jnp.maximum(t - win // 2, 0)
    hi = jnp.minimum(t - win // 2 + win, T)
    return (hi - lo).astype(F32)


def _pool_m(name, u):
    T, D = u.shape
    per = (D // len(POOL_WINDOWS)) // LANES

    def kern(u_ref, m_ref):
        j = pl.program_id(0)
        for gi, win in enumerate(POOL_WINDOWS):
            @pl.when(j // per == gi)
            def _():
                uv = u_ref[...]
                m_ref[...] = _bf(_winsum(uv, win, False) / _wincount(uv.shape, win) - uv)

    col = pl.BlockSpec((T, LANES), lambda j: (0, j))
    return pl.pallas_call(kern, name=name, grid=(D // LANES,), in_specs=[col], out_specs=col,
                          out_shape=_sds((T, D), BF16), compiler_params=_cp(("parallel",)))(u)


def _pool_fwd(name, m, x, w, bias, scale, g, b):
    T, D = x.shape
    ng = len(POOL_WINDOWS)
    dg = D // ng
    tr = _tile(T, 256, 16)

    def kern(m_ref, x_ref, w_ref, bias_ref, sc_ref, g_ref, b_ref, yp_ref, r_ref, y_ref, yb_ref):
        for gi in range(ng):
            sl = slice(gi * dg, (gi + 1) * dg)
            yp_ref[:, sl] = _dot(m_ref[:, sl], w_ref[gi]) + bias_ref[:, sl]
        r = ALPHA * x_ref[...] + yp_ref[...] * sc_ref[...]
        xc, rstd = _ln_stats(r)
        y = xc * rstd * g_ref[...] + b_ref[...]
        r_ref[...] = r
        y_ref[...] = y
        yb_ref[...] = _bf(y)

    row = pl.BlockSpec((tr, D), lambda i: (i, 0))
    vec = pl.BlockSpec((1, D), lambda i: (0, 0))
    wsp = pl.BlockSpec((ng, dg, dg), lambda i: (0, 0, 0))
    return pl.pallas_call(kern, name=name, grid=(T // tr,), in_specs=[row, row, wsp, vec, vec, vec, vec],
                          out_specs=[row, row, row, row],
                          out_shape=[_sds((T, D), F32), _sds((T, D), F32), _sds((T, D), F32), _sds((T, D), BF16)],
                          compiler_params=_cp(("parallel",)))(m, x, w, bias, scale, g, b)


def _pool_bwd_a(name, dr, ypre, scale, w):
    T, D = dr.shape
    ng = len(POOL_WINDOWS)
    dg = D // ng
    tr = _tile(T, 256, 16)

    def kern(dr_ref, yp_ref, sc_ref, w_ref, dm_ref, dyp_ref, dsc_ref, dbi_ref):
        drv = dr_ref[...]
        dyp = drv * sc_ref[...]
        dyp_ref[...] = _bf(dyp)
        for gi in range(ng):
            sl = slice(gi * dg, (gi + 1) * dg)
            dm_ref[:, sl] = _dot(dyp_ref[:, sl], w_ref[gi], NT)
        first = pl.program_id(0) == 0
        _acc_out(dsc_ref, _part8(drv * yp_ref[...]), first)
        _acc_out(dbi_ref, _part8(dyp), first)

    row = pl.BlockSpec((tr, D), lambda i: (i, 0))
    vec = pl.BlockSpec((1, D), lambda i: (0, 0))
    acc = pl.BlockSpec((8, D), lambda i: (0, 0))
    wsp = pl.BlockSpec((ng, dg, dg), lambda i: (0, 0, 0))
    return pl.pallas_call(kern, name=name, grid=(T // tr,), in_specs=[row, row, vec, wsp],
                          out_specs=[row, row, acc, acc],
                          out_shape=[_sds((T, D), F32), _sds((T, D), BF16), _sds((8, D), F32), _sds((8, D), F32)],
                          compiler_params=_cp(("arbitrary",)))(dr, ypre, scale, w)


def _pool_bwd_win(name, dm, dr):
    T, D = dm.shape
    per = (D // len(POOL_WINDOWS)) // LANES

    def kern(dm_ref, dr_ref, du_ref):
        j = pl.program_id(0)
        for gi, win in enumerate(POOL_WINDOWS):
            @pl.when(j // per == gi)
            def _():
                dmv = dm_ref[...]
                du_ref[...] = ALPHA * dr_ref[...] + _winsum(dmv / _wincount(dmv.shape, win), win, True) - dmv

    col = pl.BlockSpec((T, LANES), lambda j: (0, j))
    return pl.pallas_call(kern, name=name, grid=(D // LANES,), in_specs=[col, col], out_specs=col,
                          out_shape=_sds((T, D), F32), compiler_params=_cp(("parallel",)))(dm, dr)


def _conv_pre(x, w_ref, b_ref):
    acc = b_ref[...] + w_ref[2:3, :] * x
    for k in (0, 1, 3, 4):
        acc = acc + w_ref[k:k + 1, :] * _shift(x, k - 2)
    return acc


def _conv_fwd(name, zx, blk0, wblk0, w, b, G, cw):
    T = zx.shape[0]
    per = cw // LANES

    def kern(x_ref, w_ref, b_ref, o_ref):
        pre = _conv_pre(x_ref[...], w_ref, b_ref)
        o_ref[0] = pre * _sigmoid(pre)

    return pl.pallas_call(
        kern, name=name, grid=(G * per,),
        in_specs=[pl.BlockSpec((T, LANES), lambda j: (0, blk0 + j)),
                  pl.BlockSpec((CONV_WIDTH, LANES), lambda j: (0, wblk0 + j)),
                  pl.BlockSpec((1, LANES), lambda j: (0, wblk0 + j))],
        out_specs=pl.BlockSpec((1, T, LANES), lambda j: (j // per, 0, j % per)),
        out_shape=_sds((G, T, cw), F32), compiler_params=_cp(("parallel",)))(zx, w, b)


def _conv_bwd(name, zx, blk0, wblk0, w, b, adds, dyd=None):
    T = zx.shape[0]
    G, _, cw = adds[0].shape
    per = cw // LANES
    na = len(adds)

    def kern(*refs):
        x_ref, w_ref, b_ref = refs[:3]
        add_refs = refs[3:3 + na]
        rest = refs[3 + na:]
        if dyd is not None:
            dy_ref, dv_ref = rest[:2]
            rest = rest[2:]
        dx_ref, db_ref = rest[0], rest[1]
        dw_refs = rest[2:]
        x = x_ref[...]
        pre = _conv_pre(x, w_ref, b_ref)
        sg = _sigmoid(pre)
        dact = add_refs[0][0]
        for r in add_refs[1:]:
            dact = dact + r[0]
        if dyd is not None:
            dact = dact + dy_ref[0] * dv_ref[...]
        dpre = dact * (sg * (1.0 + pre * (1.0 - sg)))
        row0 = lax.broadcasted_iota(jnp.int32, (8, LANES), 0) == 0

        def put(ref, v):
            ref[...] = jnp.where(row0, jnp.sum(v, axis=0, keepdims=True), 0.0)

        put(db_ref, dpre)
        dx = w_ref[2:3, :] * dpre
        put(dw_refs[2], dpre * x)
        for k in (0, 1, 3, 4):
            put(dw_refs[k], dpre * _shift(x, k - 2))
            dx = dx + w_ref[k:k + 1, :] * _shift(dpre, 2 - k)
        dx_ref[...] = _bf(dx)

    gsp = pl.BlockSpec((1, T, LANES), lambda j: (j // per, 0, j % per))
    in_specs = [pl.BlockSpec((T, LANES), lambda j: (0, blk0 + j)),
                pl.BlockSpec((CONV_WIDTH, LANES), lambda j: (0, wblk0 + j)),
                pl.BlockSpec((1, LANES), lambda j: (0, wblk0 + j))] + [gsp] * na
    args = [zx, w, b, *adds]
    if dyd is not None:
        in_specs += [gsp, pl.BlockSpec((1, LANES), lambda j: (0, j))]
        args += list(dyd)
    n = G * cw
    p8 = pl.BlockSpec((8, LANES), lambda j: (0, j))
    res = pl.pallas_call(
        kern, name=name, grid=(G * per,), in_specs=in_specs,
        out_specs=[pl.BlockSpec((T, LANES), lambda j: (0, j))] + [p8] * (1 + CONV_WIDTH),
        out_shape=[_sds((T, n), BF16)] + [_sds((8, n), F32)] * (1 + CONV_WIDTH),
        compiler_params=_cp(("parallel",)))(*args)
    return res[0], res[1], res[2:]


def _split3(x):
    x1 = _bf(x)
    r1 = x - x1.astype(F32)
    x2 = _bf(r1)
    x3 = _bf(r1 - x2.astype(F32))
    return x1, x2, x3


def _tri_dot(tri, x, dn=NN):
    a, b, c = _split3(x)
    return _dot(tri, a, dn) + _dot(tri, b, dn) + _dot(tri, c, dn)


def _tri(lower):
    i = lax.broadcasted_iota(jnp.int32, (CHUNK, CHUNK), 0)
    j = lax.broadcasted_iota(jnp.int32, (CHUNK, CHUNK), 1)
    return jnp.where((i >= j) if lower else (i <= j), 1.0, 0.0).astype(BF16)


def _softplus(x):
    return jnp.maximum(x, 0.0) + jnp.log(1.0 + jnp.exp(-jnp.abs(x)))


def _dt_prep(name, zx, dtblk, dt_bias, a_log):
    T = zx.shape[0]
    H = LANES // 2

    def kern(x_ref, bias_ref, al_ref, dt_ref, acs_ref, acst_ref):
        dt = _softplus(x_ref[...] + bias_ref[...])
        dta = dt * (-jnp.exp(al_ref[...]))
        lane = lax.broadcasted_iota(jnp.int32, dta.shape, 1)
        dt_ref[...] = dt
        acs = jnp.where(lane < H, _tri_dot(_tri(True), dta), _tri_dot(_tri(False), dta))
        acs_ref[...] = acs
        acst_ref[...] = acs.T

    blk = pl.BlockSpec((CHUNK, LANES), lambda c: (c, 0))
    vec = pl.BlockSpec((1, LANES), lambda c: (0, 0))
    return pl.pallas_call(kern, name=name, grid=(T // CHUNK,),
                          in_specs=[pl.BlockSpec((CHUNK, LANES), lambda c: (c, dtblk)), vec, vec],
                          out_specs=[blk, blk, pl.BlockSpec((LANES, CHUNK), lambda c: (0, c))],
                          out_shape=[_sds((T, LANES), F32), _sds((T, LANES), F32), _sds((LANES, T), F32)],
                          compiler_params=_cp(("parallel",)))(zx, dt_bias, a_log)


def _dt_bwd(name, ddta, ddtx, zx, dtblk, dt_bias, a_log):
    T = zx.shape[0]

    def kern(da_ref, dx_ref, x_ref, bias_ref, al_ref, draw_ref, dbias_ref, dal_ref):
        pre = x_ref[...] + bias_ref[...]
        dt = _softplus(pre)
        A = -jnp.exp(al_ref[...])
        dav = da_ref[...]
        draw = (dav * A + dx_ref[...]) * _sigmoid(pre)
        draw_ref[...] = _bf(draw)
        first = pl.program_id(0) == 0
        _acc_out(dbias_ref, _part8(draw), first)
        _acc_out(dal_ref, _part8(dav * dt) * A, first)

    blk = pl.BlockSpec((CHUNK, LANES), lambda c: (c, 0))
    vec = pl.BlockSpec((1, LANES), lambda c: (0, 0))
    acc = pl.BlockSpec((8, LANES), lambda c: (0, 0))
    return pl.pallas_call(kern, name=name, grid=(T // CHUNK,),
                          in_specs=[blk, blk, pl.BlockSpec((CHUNK, LANES), lambda c: (c, dtblk)), vec, vec],
                          out_specs=[blk, acc, acc],
                          out_shape=[_sds((T, LANES), BF16), _sds((8, LANES), F32), _sds((8, LANES), F32)],
                          compiler_params=_cp(("arbitrary",)))(ddta, ddtx, zx, dt_bias, a_log)


def _ssd_specs(T, GW, hpg, cmap):
    nc = T // CHUNK
    xs = pl.BlockSpec((1, CHUNK, GW), lambda g, c: (g, cmap(c), 0))
    bc = pl.BlockSpec((1, CHUNK, D_STATE), lambda g, c: (g, cmap(c), 0))
    nat = pl.BlockSpec((CHUNK, LANES), lambda g, c: (cmap(c), 0))
    natT = pl.BlockSpec((LANES, CHUNK), lambda g, c: (0, cmap(c)))
    st = pl.BlockSpec((1, 1, D_STATE, GW), lambda g, c: (g, cmap(c), 0, 0))
    ocol = pl.BlockSpec((1, CHUNK, hpg), lambda g, c: (g, cmap(c), 0))
    return nc, xs, bc, nat, natT, st, ocol


def _dot3(x, sel, dn=NN):
    a, b, c = _split3(x)
    return _dot(a, sel, dn) + _dot(b, sel, dn) + _dot(c, sel, dn)


def _head_select(base, GW):
    k = lax.broadcasted_iota(jnp.int32, (LANES, GW), 0)
    j = lax.broadcasted_iota(jnp.int32, (LANES, GW), 1)
    lo = (k - base) * HEAD_DIM
    return jnp.where((j >= lo) & (j < lo + HEAD_DIM), 1.0, 0.0).astype(BF16)


def _head_collect(GW):
    j = lax.broadcasted_iota(jnp.int32, (GW, LANES), 0)
    k = lax.broadcasted_iota(jnp.int32, (GW, LANES), 1)
    return jnp.where((j >= k * HEAD_DIM) & (j < (k + 1) * HEAD_DIM), 1.0, 0.0).astype(BF16)


def _chunk_mask(rev):
    li = lax.broadcasted_iota(jnp.int32, (CHUNK, CHUNK), 0)
    si = lax.broadcasted_iota(jnp.int32, (CHUNK, CHUNK), 1)
    return (li <= si) if rev else (li >= si)


def _ssd_fwd(name, xs, Bg, Cg, dt, acs, acsT, d):
    G, T, GW = xs.shape
    hpg = GW // HEAD_DIM
    rev = d == 1
    nc0 = T // CHUNK
    cmap = (lambda c: nc0 - 1 - c) if rev else (lambda c: c)
    nc, xs_s, bc_s, nat_s, natT_s, st_s, _ = _ssd_specs(T, GW, hpg, cmap)
    last = 0 if rev else CHUNK - 1

    def kern(xs_ref, b_ref, c_ref, dt_ref, ac_ref, art_ref, y_ref, st_ref, state):
        @pl.when(pl.program_id(1) == 0)
        def _():
            state[...] = jnp.zeros_like(state)

        base = d * (G * hpg) + pl.program_id(0) * hpg
        Hp = state[...]
        st_ref[0, 0] = Hp
        Bm = _bf(b_ref[0])
        Cm = _bf(c_ref[0])
        S = _dot(Cm, Bm, NT)
        mask = _chunk_mask(rev)
        sel = _head_select(base, GW)
        dt_e = _dot3(dt_ref[...], sel)
        a_e = _dot3(ac_ref[...], sel)
        xdt = xs_ref[0] * dt_e
        a_end = a_e[last:last + 1, :]
        yo = _dot(Cm, _bf(Hp)) * jnp.exp(a_e)
        for r in range(hpg):
            hs = slice(r * HEAD_DIM, (r + 1) * HEAD_DIM)
            a_col = a_e[:, r * HEAD_DIM:r * HEAD_DIM + 1]
            lam = jnp.exp(jnp.where(mask, a_col - art_ref[pl.ds(base + r, 1), :], NEG))
            y_ref[0, :, hs] = _dot(_bf(S * lam), _bf(xdt[:, hs])) + yo[:, hs]
        Hn = _dot(Bm, _bf(xdt * jnp.exp(a_end - a_e)), TN)
        state[...] = jnp.exp(a_end) * Hp + Hn

    return pl.pallas_call(
        kern, name=name, grid=(G, nc), in_specs=[xs_s, bc_s, bc_s, nat_s, nat_s, natT_s], out_specs=[xs_s, st_s],
        out_shape=[_sds((G, T, GW), F32), _sds((G, nc, D_STATE, GW), F32)],
        scratch_shapes=[pltpu.VMEM((D_STATE, GW), F32)],
        compiler_params=_cp(("parallel", "arbitrary")))(xs, Bg, Cg, dt, acs, acsT)


def _ssd_bwd(name, dy, xs, Bg, Cg, dt, acs, acsT, states, d):
    G, T, GW = xs.shape
    hpg = GW // HEAD_DIM
    rev = d == 1
    nc0 = T // CHUNK
    cmap = (lambda c: c) if rev else (lambda c: nc0 - 1 - c)
    nc, xs_s, bc_s, nat_s, natT_s, st_s, ocol_s = _ssd_specs(T, GW, hpg, cmap)
    last = 0 if rev else CHUNK - 1

    def kern(dy_ref, xs_ref, b_ref, c_ref, dt_ref, ac_ref, art_ref, st_ref,
             dxs_ref, db_ref, dc_ref, dda_ref, ddx_ref, dstate, dxq):
        @pl.when(pl.program_id(1) == 0)
        def _():
            dstate[...] = jnp.zeros_like(dstate)

        base = d * (G * hpg) + pl.program_id(0) * hpg
        Bm = _bf(b_ref[0])
        Cm = _bf(c_ref[0])
        S = _dot(Cm, Bm, NT)
        mask = _chunk_mask(rev)
        sel = _head_select(base, GW)
        col = _head_collect(GW)
        dt_e = _dot3(dt_ref[...], sel)
        a_e = _dot3(ac_ref[...], sel)
        x = xs_ref[0]
        dyv = dy_ref[0]
        xdt = x * dt_e
        a_end = a_e[last:last + 1, :]
        e_end = jnp.exp(a_end)
        dte = jnp.exp(a_end - a_e)
        Hp = st_ref[0, 0]
        dHn = dstate[...]
        Hpb, dHnb = _bf(Hp), _bf(dHn)
        BdH = _dot(Bm, dHnb)
        CHp = _dot(Cm, Hpb)
        Edy = jnp.exp(a_e) * dyv
        wv = dte * xdt
        dCa = _dot(_bf(Edy), Hpb, NT)
        dBa = _dot(_bf(wv), dHnb, NT)
        dstate[...] = e_end * dHn + _dot(Cm, _bf(Edy), TN)
        lane = lax.broadcasted_iota(jnp.int32, (CHUNK, LANES), 1)
        rowi = lax.broadcasted_iota(jnp.int32, (CHUNK, LANES), 0)
        dS = jnp.zeros((CHUNK, CHUNK), F32)
        dq = jnp.zeros((CHUNK, LANES), F32)
        for r in range(hpg):
            hs = slice(r * HEAD_DIM, (r + 1) * HEAD_DIM)
            a_col = a_e[:, r * HEAD_DIM:r * HEAD_DIM + 1]
            lam = jnp.exp(jnp.where(mask, a_col - art_ref[pl.ds(base + r, 1), :], NEG))
            Mf = S * lam
            dyr = _bf(dyv[:, hs])
            dxq[:, hs] = _dot(_bf(Mf), dyr, TN)
            dM = _dot(dyr, _bf(xdt[:, hs]), NT)
            dS = dS + dM * lam
            Gb = _bf(dM * Mf)
            oh = jnp.where(lane == r, 1.0, 0.0).astype(BF16)
            dq = dq + _dot(Gb, oh) - _dot(Gb, oh, TN)
        dxdt = dxq[...] + dte * BdH
        dxs_ref[0] = dxdt * dt_e
        ts = _dot(_bf(wv * BdH), col)
        ddx_ref[0] = _dot(_bf(dxdt * x), col)[:, :hpg]
        hh = _dot3(jnp.broadcast_to(e_end * jnp.sum(dHn * Hp, axis=0, keepdims=True), (8, GW)), col)[0:1, :]
        tot = jnp.sum(ts, axis=0, keepdims=True) + hh
        da = dq + _dot(_bf(Edy * CHp), col) - ts + jnp.where(rowi == last, tot, 0.0)
        dda_ref[0] = _tri_dot(_tri(rev), da)[:, :hpg]
        dSb = _bf(dS)
        dc_ref[0] = dCa + _dot(dSb, Bm)
        db_ref[0] = dBa + _dot(dSb, Cm, TN)

    return pl.pallas_call(
        kern, name=name, grid=(G, nc), in_specs=[xs_s, xs_s, bc_s, bc_s, nat_s, nat_s, natT_s, st_s],
        out_specs=[xs_s, bc_s, bc_s, ocol_s, ocol_s],
        out_shape=[_sds((G, T, GW), F32), _sds((G, T, D_STATE), F32), _sds((G, T, D_STATE), F32),
                   _sds((G, T, hpg), F32), _sds((G, T, hpg), F32)],
        scratch_shapes=[pltpu.VMEM((D_STATE, GW), F32), pltpu.VMEM((CHUNK, GW), F32)],
        compiler_params=_cp(("parallel", "arbitrary")))(dy, xs, Bg, Cg, dt, acs, acsT, states)


def _gate_core(yf, yb, xs, z, dv):
    y = yf + yb + xs * dv
    sg = _sigmoid(z)
    sz = z * sg
    gy = y * sz
    rstd = lax.rsqrt(jnp.mean(gy * gy, axis=-1, keepdims=True) + RMS_EPS)
    return y, sg, sz, gy, rstd


def _gate_fwd(name, yf, yb, xs, zx, dvec, nw):
    G, T, GW = xs.shape
    tr = _tile(T, 512, 16)

    def kern(yf_ref, yb_ref, xs_ref, z_ref, dv_ref, nw_ref, o_ref):
        _, _, _, gy, rstd = _gate_core(yf_ref[0], yb_ref[0], xs_ref[0], z_ref[...], dv_ref[...])
        o_ref[...] = _bf(gy * rstd * nw_ref[...])

    gsp = pl.BlockSpec((1, tr, GW), lambda g, t: (g, t, 0))
    zsp = pl.BlockSpec((tr, GW), lambda g, t: (t, g))
    vsp = pl.BlockSpec((1, GW), lambda g, t: (0, g))
    return pl.pallas_call(kern, name=name, grid=(G, T // tr), in_specs=[gsp, gsp, gsp, zsp, vsp, vsp], out_specs=zsp,
                          out_shape=_sds((T, G * GW), BF16),
                          compiler_params=_cp(("parallel", "parallel")))(yf, yb, xs, zx, dvec, nw)


def _gate_bwd(name, dgyn, yf, yb, xs, zx, dvec, nw):
    G, T, GW = xs.shape
    tr = _tile(T, 512, 16)

    def kern(dg_ref, yf_ref, yb_ref, xs_ref, z_ref, dv_ref, nw_ref, dy_ref, dz_ref, dnw_ref, ddl_ref):
        xsv = xs_ref[0]
        zv = z_ref[...]
        y, sg, sz, gy, rstd = _gate_core(yf_ref[0], yb_ref[0], xsv, zv, dv_ref[...])
        n = gy * rstd
        dgv = dg_ref[...]
        dn = dgv * nw_ref[...]
        dgy = rstd * (dn - n * jnp.mean(dn * n, axis=-1, keepdims=True))
        dyv = dgy * sz
        dy_ref[0] = dyv
        dz_ref[...] = _bf(dgy * y * (sg * (1.0 + zv * (1.0 - sg))))
        first = pl.program_id(1) == 0
        _acc_out(dnw_ref, _part8(dgv * n), first)
        _acc_out(ddl_ref, _part8(dyv * xsv), first)

    gsp = pl.BlockSpec((1, tr, GW), lambda g, t: (g, t, 0))
    zsp = pl.BlockSpec((tr, GW), lambda g, t: (t, g))
    vsp = pl.BlockSpec((1, GW), lambda g, t: (0, g))
    asp = pl.BlockSpec((8, GW), lambda g, t: (0, g))
    return pl.pallas_call(kern, name=name, grid=(G, T // tr), in_specs=[zsp, gsp, gsp, gsp, zsp, vsp, vsp],
                          out_specs=[gsp, zsp, asp, asp],
                          out_shape=[_sds((G, T, GW), F32), _sds((T, G * GW), BF16), _sds((8, G * GW), F32),
                                     _sds((8, G * GW), F32)],
                          compiler_params=_cp(("parallel", "arbitrary")))(dgyn, yf, yb, xs, zx, dvec, nw)


def _head_sum(name, v):
    n = v.shape[1]
    H = n // HEAD_DIM

    def kern(v_ref, o_ref):
        i = lax.broadcasted_iota(jnp.int32, (n, H), 0)
        j = lax.broadcasted_iota(jnp.int32, (n, H), 1)
        sel = jnp.where((i >= j * HEAD_DIM) & (i < (j + 1) * HEAD_DIM), 1.0, 0.0).astype(BF16)
        a, b, c = _split3(v_ref[...])
        o_ref[...] = _dot(a, sel) + _dot(b, sel) + _dot(c, sel)

    return pl.pallas_call(kern, name=name, out_shape=_sds((8, H), F32))(v)


def _relu2_epi(acc):
    return acc, jnp.square(jnp.maximum(acc, 0.0))


def _dh_epi(acc, h):
    return (acc * (2.0 * jnp.maximum(h, 0.0)),)


def _resid_epi(acc, e):
    return (acc + ALPHA * e,)


def _ssd_fwd_layer(tag, x, xb, W, j):
    T, D = x.shape
    DI = 2 * D
    DBC = N_GROUPS * D_STATE
    win = W["win"][j]
    NZ = win.shape[1]
    zx = _mm_act(tag + "_inproj", xb, win, None, T, NZ, D, [F32], tn_pref=1152)[0]
    zb = DI // LANES
    cw, cb = W["conv_w"][j], W["conv_b"][j]
    xs = _conv_fwd(tag + "_convx", zx, zb, 0, cw, cb, N_GROUPS, DI // N_GROUPS)
    Bg = _conv_fwd(tag + "_convb", zx, zb + DI // LANES, DI // LANES, cw, cb, N_GROUPS, D_STATE)
    Cg = _conv_fwd(tag + "_convc", zx, zb + (DI + DBC) // LANES, (DI + DBC) // LANES, cw, cb, N_GROUPS, D_STATE)
    dtblk = (2 * DI + 2 * DBC) // LANES
    dt, acs, acsT = _dt_prep(tag + "_dtprep", zx, dtblk, W["dt_bias"][j], W["a_log"][j])
    yf, stf = _ssd_fwd(tag + "_scanf", xs, Bg, Cg, dt, acs, acsT, 0)
    yb, stb = _ssd_fwd(tag + "_scanb", xs, Bg, Cg, dt, acs, acsT, 1)
    gyn = _gate_fwd(tag + "_gate", yf, yb, xs, zx, W["dvec"][j], W["norm_w"][j])
    mix = _mm_act(tag + "_outproj", gyn, W["wout"][j], _RowShard(DI // 4, D), T, D, DI, [F32])[0]
    saved = dict(zx=zx, xs=xs, Bg=Bg, Cg=Cg, dt=dt, acs=acs, acsT=acsT, stf=stf, stb=stb,
                 yf=yf, yb=yb, gyn=gyn, dtblk=dtblk)
    return mix, saved


def _ssd_bwd_layer(tag, xb, dr1, dr1b, W, j, s, small):
    T, D = dr1.shape
    DI = 2 * D
    DBC = N_GROUPS * D_STATE
    win = W["win"][j]
    NZ = win.shape[1]
    zx = s["zx"]
    rs = _RowShard(DI // 4, D)
    dgyn = _mm_act(tag + "_dgyn", dr1b, W["wout"][j], rs, T, DI, D, [F32], nt=True)[0]
    p_out = _mm_wgrad(tag + "_dwout", s["gyn"], dr1b, DI, D, T, _sds((4, DI // 4, D), BF16), rs.out)
    dvec, nw = W["dvec"][j], W["norm_w"][j]
    dy, dzb, dnw, ddl = _gate_bwd(tag + "_dgate", dgyn, s["yf"], s["yb"], s["xs"], zx, dvec, nw)
    res = []
    for d, st in ((0, s["stf"]), (1, s["stb"])):
        res.append(_ssd_bwd(tag + "_dscan%d" % d, dy, s["xs"], s["Bg"], s["Cg"], s["dt"], s["acs"], s["acsT"], st, d))
    nat = lambda k: jnp.concatenate([res[d][k].transpose(1, 0, 2).reshape(T, -1) for d in (0, 1)], axis=1)
    drawb, dbias, dal = _dt_bwd(tag + "_ddt", nat(3), nat(4), zx, s["dtblk"], W["dt_bias"][j], W["a_log"][j])
    zb = DI // LANES
    cw, cb = W["conv_w"][j], W["conv_b"][j]
    dxx, dbx, dwx = _conv_bwd(tag + "_dconvx", zx, zb, 0, cw, cb, [res[0][0], res[1][0]], (dy, dvec))
    dxb, dbb, dwb = _conv_bwd(tag + "_dconvb", zx, zb + DI // LANES, DI // LANES, cw, cb, [res[0][1], res[1][1]])
    dxc, dbc, dwc = _conv_bwd(tag + "_dconvc", zx, zb + (DI + DBC) // LANES, (DI + DBC) // LANES, cw, cb,
                              [res[0][2], res[1][2]])
    dzx = jnp.concatenate([dzb, dxx, dxb, dxc, drawb], axis=1)
    dwin = _mm_wgrad(tag + "_dwin", xb, dzx, D, NZ, T, _sds((D, NZ), BF16),
                     lambda tm, tn: pl.BlockSpec((tm, tn), lambda i, jj, k: (i, jj)), tn_pref=1152)
    p_in = dwin.reshape(D, 4, NZ // 4).transpose(1, 0, 2)
    dx = _mm_act(tag + "_dxin", dzx, win, None, T, D, NZ, [F32], epi=_resid_epi, extras=(dr1,), nt=True,
                 tk_pref=1152)[0]
    small["conv_w"].append(jnp.concatenate([p[k] for k in range(CONV_WIDTH) for p in (dwx, dwb, dwc)], axis=1))
    small["conv_b"].append(jnp.concatenate([dbx, dbb, dbc], axis=1))
    small["dt_bias"].append(dbias)
    small["a_log"].append(dal)
    small["d"].append(ddl)
    small["norm_w"].append(dnw)
    return dx, p_in, p_out


def _pool_bwd_layer(tag, dr1, W, j, s, small):
    T, D = dr1.shape
    ng = len(POOL_WINDOWS)
    dg = D // ng
    dm, dypb, dsc, dbi = _pool_bwd_a(tag + "_dpool", dr1, s["ypre"], W["pool_scale"][j], W["wp"][j])
    tk = _tile(T, 1024)
    dwp = _mm(tag + "_dwp", s["m"], dypb, pl.BlockSpec((tk, dg), lambda i, jj, k: (k, i)),
              pl.BlockSpec((tk, dg), lambda i, jj, k: (k, i)), TN, (ng, 1, T // tk), dg, dg,
              [_sds((ng, dg, dg), BF16)], [pl.BlockSpec((None, dg, dg), lambda i, jj, k: (i, 0, 0))])[0]
    p_pool = dwp.reshape(ng, 4, dg // 4, dg).transpose(1, 0, 2, 3).reshape(4, dg, dg)
    dx = _pool_bwd_win(tag + "_dwin", dm, dr1)
    small["pool_b"].append(dbi)
    small["pool_scale"].append(dsc)
    return dx, p_pool


SMALL_NAMES = ("conv_w", "conv_b", "dt_bias", "a_log", "d", "norm_w", "pool_b", "pool_scale",
               "ln_mix_g", "ln_mix_b", "ln_ffn_g", "ln_ffn_b")


def _local_step(x, tgt, W):
    T, D = x.shape
    DFF = 4 * D
    cs, rs = _ColShard(D, DFF // 4), _RowShard(DFF // 4, D)
    saved = []
    xb = _bf(x)
    for i in range(DEPTH):
        j = i // 2
        tag = "L%d" % i
        s = dict(x=x, xb=xb)
        if i % 2 == 0:
            mix, ss = _ssd_fwd_layer(tag, x, xb, W, j)
            s.update(ss)
            r1, x1, x1b = _resln(tag + "_lnmix", x, mix, W["ln_mix_g"][i], W["ln_mix_b"][i])
        else:
            m = _pool_m(tag + "_poolm", x)
            ypre, r1, x1, x1b = _pool_fwd(tag + "_pool", m, x, W["wp"][j], W["pool_b"][j], W["pool_scale"][j],
                                          W["ln_mix_g"][i], W["ln_mix_b"][i])
            s.update(m=m, ypre=ypre)
        h, a = _mm_act(tag + "_mlp1", x1b, W["w1"][i], cs, T, DFF, D, [F32, BF16], epi=_relu2_epi)
        mlp = _mm_act(tag + "_mlp2", a, W["w2"][i], rs, T, D, DFF, [F32])[0]
        r2, x2, x2b = _resln(tag + "_lnffn", x1, mlp, W["ln_ffn_g"][i], W["ln_ffn_b"][i])
        s.update(r1=r1, x1=x1, x1b=x1b, h=h, a=a, r2=r2)
        saved.append(s)
        x, xb = x2, x2b

    dx, loss = _loss_head("loss", x, tgt)
    small = {n: [] for n in SMALL_NAMES}
    P = dict(win=[], wout=[], w1=[], w2=[], wp=[])
    for i in reversed(range(DEPTH)):
        j = i // 2
        tag = "L%d" % i
        s = saved[i]
        dr2, dr2b, dg2, db2 = _lnbwd(tag + "_dlnffn", dx, s["r2"], W["ln_ffn_g"][i])
        dh = _mm_act(tag + "_dh", dr2b, W["w2"][i], rs, T, DFF, D, [BF16], epi=_dh_epi, extras=(s["h"],), nt=True)[0]
        P["w2"].append(_mm_wgrad(tag + "_dw2", s["a"], dr2b, DFF, D, T, _sds((4, DFF // 4, D), BF16), rs.out))
        P["w1"].append(_mm_wgrad(tag + "_dw1", s["x1b"], dh, D, DFF, T, _sds((4, D, DFF // 4), BF16), cs.out))
        dx1 = _mm_act(tag + "_dx1", dh, W["w1"][i], cs, T, D, DFF, [F32], epi=_resid_epi, extras=(dr2,), nt=True)[0]
        dr1, dr1b, dg1, db1 = _lnbwd(tag + "_dlnmix", dx1, s["r1"], W["ln_mix_g"][i])
        if i % 2 == 0:
            dx, p_in, p_out = _ssd_bwd_layer(tag, s["xb"], dr1, dr1b, W, j, s, small)
            P["win"].append(p_in)
            P["wout"].append(p_out)
        else:
            dx, p_pool = _pool_bwd_layer(tag, dr1, W, j, s, small)
            P["wp"].append(p_pool)
        small["ln_ffn_g"].append(dg2)
        small["ln_ffn_b"].append(db2)
        small["ln_mix_g"].append(dg1)
        small["ln_mix_b"].append(db1)
    P = {k: v[::-1] for k, v in P.items()}
    small = {k: jnp.concatenate(v[::-1], axis=1) for k, v in small.items()}
    small["d"] = _head_sum("dD", small["d"])
    return loss, dx, P, small


ANY = pl.BlockSpec(memory_space=pl.ANY)


def _pos():
    return lax.axis_index("x"), lax.axis_index("y"), lax.axis_index("c")


def _other_chips(x, y):
    return [(1 - x, y), (x, 1 - y), (1 - x, 1 - y)]


def _rcopy(src, dst, ssem, rsem, dev):
    return pltpu.make_async_remote_copy(src_ref=src, dst_ref=dst, send_sem=ssem, recv_sem=rsem,
                                        device_id=dev, device_id_type=MESH)


def _gather(name, slabs, split):
    n = len(slabs)

    def body(*refs):
        src, out = refs[:n], refs[n:2 * n]
        ssem, rsem, fssem, frsem, lsem = refs[2 * n:]
        x, y, c = _pos()
        chip = 2 * x + y
        chips = _other_chips(x, y)
        sib = (x, y, 1 - c)

        def mine(t, half):
            if split[t]:
                h = slabs[t].shape[0] // 2
                return src[t].at[pl.ds(half * h, h)]
            return src[t]

        def region(t, ch, half):
            if split[t]:
                h = slabs[t].shape[0] // 2
                return out[t].at[ch, pl.ds(half * h, h)]
            return out[t].at[ch]

        local = [pltpu.make_async_copy(src[t], out[t].at[chip], lsem.at[t]) for t in range(n)]
        for cp in local:
            cp.start()
        sends = []
        for t in range(n):
            for j, (px, py) in enumerate(chips):
                cp = _rcopy(mine(t, c), region(t, chip, c), ssem.at[t, j], rsem.at[t, j], (px, py, c))
                cp.start()
                sends.append(cp)
        for t in range(n):
            for j, (px, py) in enumerate(chips):
                pch = 2 * px + py
                _rcopy(mine(t, c), region(t, pch, c), ssem.at[t, j], rsem.at[t, j], (px, py, c)).wait_recv()
                if split[t]:
                    cp = _rcopy(region(t, pch, c), region(t, pch, c), fssem.at[t, j], frsem.at[t, j], sib)
                    cp.start()
                    sends.append(cp)
        for t in range(n):
            if split[t]:
                for j, (px, py) in enumerate(chips):
                    pch = 2 * px + py
                    _rcopy(region(t, pch, 1 - c), region(t, pch, 1 - c), fssem.at[t, j], frsem.at[t, j],
                           sib).wait_recv()
        for cp in sends:
            cp.wait_send()
        for cp in local:
            cp.wait()

    sem = pltpu.SemaphoreType.DMA
    return pl.pallas_call(
        body, name=name, in_specs=[ANY] * n, out_specs=[ANY] * n,
        out_shape=[_sds((4,) + s.shape, s.dtype) for s in slabs],
        scratch_shapes=[sem((n, 3)), sem((n, 3)), sem((n, 3)), sem((n, 3)), sem((n,))])(*slabs)


def _rs1(name, Ps):
    n = len(Ps)

    def body(*refs):
        src, out = refs[:n], refs[n:2 * n]
        ssem, rsem = refs[2 * n:]
        x, y, c = _pos()
        cps = []
        for t in range(n):
            h = Ps[t].shape[1] // 2
            cp = _rcopy(src[t].at[pl.ds(0, 4), pl.ds((1 - c) * h, h)], out[t], ssem.at[t], rsem.at[t], (x, y, 1 - c))
            cp.start()
            cps.append(cp)
        for cp in cps:
            cp.wait()

    sem = pltpu.SemaphoreType.DMA
    return pl.pallas_call(
        body, name=name, in_specs=[ANY] * n, out_specs=[ANY] * n,
        out_shape=[_sds((4, p.shape[1] // 2, p.shape[2]), p.dtype) for p in Ps],
        scratch_shapes=[sem((n,)), sem((n,))])(*Ps)


def _rs2(name, Qs):
    n = len(Qs)

    def body(*refs):
        src, r2, qo = refs[:n], refs[n:2 * n], refs[2 * n:3 * n]
        ssem, rsem, fssem, frsem, qssem, qrsem, lsem = refs[3 * n:]
        x, y, c = _pos()
        chip = 2 * x + y
        chips = _other_chips(x, y)
        sib = (x, y, 1 - c)
        sends, local = [], []
        for t in range(n):
            h = Qs[t].shape[1]
            lc = pltpu.make_async_copy(src[t].at[chip], qo[t].at[pl.ds(c * h, h)], lsem.at[t])
            lc.start()
            local.append(lc)
            cp = _rcopy(src[t].at[chip], qo[t].at[pl.ds(c * h, h)], qssem.at[t], qrsem.at[t], sib)
            cp.start()
            sends.append(cp)
            for j, (px, py) in enumerate(chips):
                cp = _rcopy(src[t].at[2 * px + py], r2[t].at[j, pl.ds(c * h, h)], ssem.at[t, j], rsem.at[t, j],
                            (px, py, c))
                cp.start()
                sends.append(cp)
        for t in range(n):
            h = Qs[t].shape[1]
            for j, (px, py) in enumerate(chips):
                mine = r2[t].at[j, pl.ds(c * h, h)]
                _rcopy(src[t].at[chip], mine, ssem.at[t, j], rsem.at[t, j], (px, py, c)).wait_recv()
                cp = _rcopy(mine, mine, fssem.at[t, j], frsem.at[t, j], sib)
                cp.start()
                sends.append(cp)
        for t in range(n):
            h = Qs[t].shape[1]
            _rcopy(src[t].at[chip], qo[t].at[pl.ds((1 - c) * h, h)], qssem.at[t], qrsem.at[t], sib).wait_recv()
            for j in range(3):
                theirs = r2[t].at[j, pl.ds((1 - c) * h, h)]
                _rcopy(theirs, theirs, fssem.at[t, j], frsem.at[t, j], sib).wait_recv()
        for cp in sends:
            cp.wait_send()
        for lc in local:
            lc.wait()

    sem = pltpu.SemaphoreType.DMA
    return pl.pallas_call(
        body, name=name, in_specs=[ANY] * n, out_specs=[ANY] * (2 * n),
        out_shape=[_sds((3, 2 * q.shape[1], q.shape[2]), q.dtype) for q in Qs]
        + [_sds((2 * q.shape[1], q.shape[2]), q.dtype) for q in Qs],
        scratch_shapes=[sem((n, 3)), sem((n, 3)), sem((n, 3)), sem((n, 3)), sem((n,)), sem((n,)), sem((n,))])(*Qs)


def _allgather_small(name, v):
    def body(v_ref, out_ref, ssem, rsem, lsem):
        x, y, c = _pos()
        me = 4 * x + 2 * y + c
        lc = pltpu.make_async_copy(v_ref, out_ref.at[me], lsem)
        lc.start()
        cps = []
        for k in range(1, 8):
            flip = lambda a, bit: (1 - a) if bit else a
            peer = (flip(x, k & 4), flip(y, k & 2), flip(c, k & 1))
            cp = _rcopy(v_ref, out_ref.at[me], ssem.at[k - 1], rsem.at[k - 1], peer)
            cp.start()
            cps.append(cp)
        for cp in cps:
            cp.wait()
        lc.wait()

    sem = pltpu.SemaphoreType.DMA
    return pl.pallas_call(body, name=name, in_specs=[ANY], out_specs=ANY, out_shape=_sds((8,) + v.shape, v.dtype),
                          scratch_shapes=[sem((7,)), sem((7,)), sem])(v)


def _row_tile(R, C, mult):
    return _tile(R, max(mult, (1 << 19) // C), mult)


def _sum1(name, P, R1, c):
    _, A, B = P.shape
    h = A // 2
    ta = _row_tile(h, B, 16)
    nb = h // ta

    def kern(c_ref, p_ref, r_ref, q_ref):
        q_ref[...] = _bf(p_ref[...].astype(F32) + r_ref[...].astype(F32))

    gs = pltpu.PrefetchScalarGridSpec(
        num_scalar_prefetch=1, grid=(4, nb),
        in_specs=[pl.BlockSpec((1, ta, B), lambda s, i, c_ref: (s, c_ref[0] * nb + i, 0)),
                  pl.BlockSpec((1, ta, B), lambda s, i, c_ref: (s, i, 0))],
        out_specs=pl.BlockSpec((1, ta, B), lambda s, i, c_ref: (s, i, 0)))
    return pl.pallas_call(kern, name=name, grid_spec=gs, out_shape=_sds((4, h, B), BF16),
                          compiler_params=_cp(("parallel", "parallel")))(c.reshape(1), P, R1)


def _adam_math(w, gv, m, v):
    mn = ADAM_B1 * m + (1.0 - ADAM_B1) * gv
    vn = ADAM_B2 * v + (1.0 - ADAM_B2) * jnp.square(gv)
    m_hat = mn / (1.0 - ADAM_B1 ** ADAM_STEP)
    v_hat = vn / (1.0 - ADAM_B2 ** ADAM_STEP)
    return -ADAM_LR * (m_hat / (jnp.sqrt(v_hat) + ADAM_EPS) + ADAM_WD * w), mn, vn


def _sum2_adam(name, qo, r2, w, m, v, l, prev):
    L, A, B = w.shape
    ta = _row_tile(A, B, 16)

    def kern(q_ref, r_ref, w_ref, m_ref, v_ref, *rest):
        g_ref, d_ref, mo_ref, vo_ref = rest[-4:]
        gv = q_ref[...].astype(F32)
        for j in range(3):
            gv = gv + r_ref[j].astype(F32)
        g_ref[...] = gv
        d_ref[...], mo_ref[...], vo_ref[...] = _adam_math(w_ref[...], gv, m_ref[...], v_ref[...])

    lay = pl.BlockSpec((None, ta, B), lambda i: (l, i, 0))
    in_specs = [pl.BlockSpec((ta, B), lambda i: (i, 0)), pl.BlockSpec((3, ta, B), lambda i: (0, i, 0)), lay, lay, lay]
    args = [qo, r2, w, m, v]
    aliases = {}
    if prev is not None:
        in_specs += [ANY] * 4
        args += list(prev)
        aliases = {5 + k: k for k in range(4)}
    return pl.pallas_call(kern, name=name, grid=(A // ta,), in_specs=in_specs, out_specs=[lay] * 4,
                          out_shape=[_sds((L, A, B), F32)] * 4, input_output_aliases=aliases,
                          compiler_params=_cp(("parallel",)))(*args)


def _adam(name, w, g, m, v):
    R, C = w.shape
    tr = _row_tile(R, C, 8)

    def kern(w_ref, g_ref, m_ref, v_ref, d_ref, mo_ref, vo_ref):
        d_ref[...], mo_ref[...], vo_ref[...] = _adam_math(w_ref[...], g_ref[...], m_ref[...], v_ref[...])

    blk = pl.BlockSpec((tr, C), lambda i: (i, 0))
    return pl.pallas_call(kern, name=name, grid=(R // tr,), in_specs=[blk] * 4, out_specs=[blk] * 3,
                          out_shape=[_sds((R, C), F32)] * 3, compiler_params=_cp(("parallel",)))(w, g, m, v)


def _rowsum8(name, v):
    n = v.shape[1]
    tn = _tile(n, 16384)

    def kern(v_ref, o_ref):
        o_ref[...] = jnp.sum(v_ref[...], axis=0, keepdims=True)

    return pl.pallas_call(kern, name=name, grid=(n // tn,), in_specs=[pl.BlockSpec((8, tn), lambda i: (0, i))],
                          out_specs=pl.BlockSpec((1, tn), lambda i: (0, i)), out_shape=_sds((1, n), F32))(v)


def _sum_devices(name, v):
    n = v.shape[2]
    tn = _tile(n, 4096)

    def kern(v_ref, o_ref):
        s = v_ref[0]
        for d in range(1, 8):
            s = s + v_ref[d]
        o_ref[...] = s

    return pl.pallas_call(kern, name=name, grid=(n // tn,), in_specs=[pl.BlockSpec((8, 8, tn), lambda i: (0, 0, i))],
                          out_specs=pl.BlockSpec((8, tn), lambda i: (0, i)), out_shape=_sds((8, n), F32))(v)


def _pack8(parts, quantum=8 * LANES):
    flat = jnp.concatenate([p.reshape(-1) for p in parts])
    n = flat.shape[0]
    npad = -n % quantum
    return jnp.pad(flat, (0, npad)).reshape(8, -1), n


def _unpack(flat, shapes):
    out, o = [], 0
    for s in shapes:
        k = 1
        for d in s:
            k *= d
        out.append(flat[o:o + k].reshape(s))
        o += k
    return out


def kernel(x, ssd_in_proj, ssd_conv_w, ssd_conv_b, ssd_dt_bias, ssd_A_log, ssd_D, ssd_norm_w, ssd_out_proj, pool_w, pool_b, pool_scale, mlp_w1, mlp_w2, ln_mix_g, ln_mix_b, ln_ffn_g, ln_ffn_b, loss_target, m_ssd_in_proj, m_ssd_conv_w, m_ssd_conv_b, m_ssd_dt_bias, m_ssd_A_log, m_ssd_D, m_ssd_norm_w, m_ssd_out_proj, m_pool_w, m_pool_b, m_pool_scale, m_mlp_w1, m_mlp_w2, m_ln_mix_g, m_ln_mix_b, m_ln_ffn_g, m_ln_ffn_b, v_ssd_in_proj, v_ssd_conv_w, v_ssd_conv_b, v_ssd_dt_bias, v_ssd_A_log, v_ssd_D, v_ssd_norm_w, v_ssd_out_proj, v_pool_w, v_pool_b, v_pool_scale, v_mlp_w1, v_mlp_w2, v_ln_mix_g, v_ln_mix_b, v_ln_ffn_g, v_ln_ffn_b):
    _, T, D = x.shape
    DI, DFF = 2 * D, 4 * D
    NZ = 4 * ssd_in_proj.shape[2]
    nssd, npool = ssd_in_proj.shape[0], pool_w.shape[0]
    ng = len(POOL_WINDOWS)
    dg = D // ng
    xi, yi, ci = _pos()
    chip = 2 * xi + yi

    in_b, out_b, w1_b, w2_b, pw_b = (a.astype(BF16) for a in (ssd_in_proj, ssd_out_proj, mlp_w1, mlp_w2, pool_w))
    g_in, g_out, g_pw, g_w1, g_w2 = [], [], [], [], []
    for i in range(DEPTH):
        j = i // 2
        if i % 2 == 0:
            a, b, c1, c2 = _gather("gather_L%d" % i, [in_b[j], out_b[j], w1_b[i], w2_b[i]], [True] * 4)
            g_in.append(a.transpose(1, 0, 2).reshape(D, NZ))
            g_out.append(b)
        else:
            a, c1, c2 = _gather("gather_L%d" % i, [pw_b[j], w1_b[i], w2_b[i]], [True] * 3)
            g_pw.append(a.transpose(1, 0, 2, 3).reshape(ng, dg, dg))
        g_w1.append(c1)
        g_w2.append(c2)
    g_cw, g_pb, g_ps = _gather("gather_small", [ssd_conv_w, pool_b, pool_scale], [False] * 3)
    W = dict(
        win=g_in, wout=g_out, w1=g_w1, w2=g_w2, wp=g_pw,
        conv_w=[g_cw[:, j, :, 0, :].transpose(1, 0, 2).reshape(CONV_WIDTH, -1) for j in range(nssd)],
        conv_b=[ssd_conv_b[j].reshape(1, -1) for j in range(nssd)],
        dt_bias=[ssd_dt_bias[j].reshape(1, -1) for j in range(nssd)],
        a_log=[ssd_A_log[j].reshape(1, -1) for j in range(nssd)],
        dvec=[jnp.repeat(ssd_D[j], HEAD_DIM).reshape(1, -1) for j in range(nssd)],
        norm_w=[ssd_norm_w[j].reshape(1, -1) for j in range(nssd)],
        pool_b=[g_pb[:, j].transpose(1, 0, 2).reshape(1, -1) for j in range(npool)],
        pool_scale=[g_ps[:, j].reshape(1, -1) for j in range(npool)],
        ln_mix_g=[ln_mix_g[i].reshape(1, -1) for i in range(DEPTH)],
        ln_mix_b=[ln_mix_b[i].reshape(1, -1) for i in range(DEPTH)],
        ln_ffn_g=[ln_ffn_g[i].reshape(1, -1) for i in range(DEPTH)],
        ln_ffn_b=[ln_ffn_b[i].reshape(1, -1) for i in range(DEPTH)],
    )

    loss_blk, dx, P, small = _local_step(x[0], loss_target[0], W)
    loss = lax.psum(loss_blk[0, 0], ("x", "y", "c"))

    big = dict(win=(ssd_in_proj, m_ssd_in_proj, v_ssd_in_proj), wout=(ssd_out_proj, m_ssd_out_proj, v_ssd_out_proj),
               wp=(pool_w, m_pool_w, v_pool_w), w1=(mlp_w1, m_mlp_w1, v_mlp_w1), w2=(mlp_w2, m_mlp_w2, v_mlp_w2))
    big = {k: tuple(a.reshape(a.shape[0], -1, a.shape[-1]) for a in t) for k, t in big.items()}
    res = {}
    for i in range(DEPTH):
        j = i // 2
        keys = ["win", "wout", "w1", "w2"] if i % 2 == 0 else ["wp", "w1", "w2"]
        lay = [j if k in ("win", "wout", "wp") else i for k in keys]
        units = [P[k][l] for k, l in zip(keys, lay)]
        R1 = _rs1("rs1_L%d" % i, units)
        Q = [_sum1("sum1_L%d_%s" % (i, k), p, r, ci) for k, p, r in zip(keys, units, R1)]
        R2Q = _rs2("rs2_L%d" % i, Q)
        for t, (k, l) in enumerate(zip(keys, lay)):
            res[k] = _sum2_adam("adam_L%d_%s" % (i, k), R2Q[len(keys) + t], R2Q[t], *big[k], l, res.get(k))
    res = {k: tuple(a.reshape(s.shape) for a in res[k])
           for k, s in dict(win=ssd_in_proj, wout=ssd_out_proj, wp=pool_w, w1=mlp_w1, w2=mlp_w2).items()}

    flat8 = jnp.concatenate([small[n] for n in SMALL_NAMES], axis=1)
    ns = flat8.shape[1]
    flat8 = jnp.pad(flat8, ((0, 0), (0, -ns % (8 * LANES))))
    mine8 = _rowsum8("small_rowsum", flat8).reshape(8, -1)
    tot = _sum_devices("small_sum", _allgather_small("small_allgather", mine8)).reshape(-1)
    sw = [ssd_conv_w, ssd_conv_b, ssd_dt_bias, ssd_A_log, ssd_D, ssd_norm_w, pool_b, pool_scale,
          ln_mix_g, ln_mix_b, ln_ffn_g, ln_ffn_b]
    sm = [m_ssd_conv_w, m_ssd_conv_b, m_ssd_dt_bias, m_ssd_A_log, m_ssd_D, m_ssd_norm_w, m_pool_b, m_pool_scale,
          m_ln_mix_g, m_ln_mix_b, m_ln_ffn_g, m_ln_ffn_b]
    sv = [v_ssd_conv_w, v_ssd_conv_b, v_ssd_dt_bias, v_ssd_A_log, v_ssd_D, v_ssd_norm_w, v_pool_b, v_pool_scale,
          v_ln_mix_g, v_ln_mix_b, v_ln_ffn_g, v_ln_ffn_b]
    full_shapes = [(nssd, CONV_WIDTH, 1, DI + 2 * N_GROUPS * D_STATE)] + [w.shape for w in sw[1:6]] \
        + [(npool, ng, dg), (npool, D)] + [w.shape for w in sw[8:]]
    sg = _unpack(tot, full_shapes)
    sg[0] = lax.dynamic_slice_in_dim(sg[0], chip * sw[0].shape[3], sw[0].shape[3], axis=3)
    sg[6] = lax.dynamic_slice_in_dim(sg[6], chip * sw[6].shape[2], sw[6].shape[2], axis=2)
    sg[7] = lax.dynamic_slice_in_dim(sg[7], chip * sw[7].shape[1], sw[7].shape[1], axis=1)
    packs = [_pack8(parts)[0] for parts in (sw, sg, sm, sv)]
    sd, smn, svn = _adam("adam_small", *packs)
    shapes = [w.shape for w in sw]
    sd, smn, svn = (_unpack(a.reshape(-1), shapes) for a in (sd, smn, svn))

    order = ["win", 0, 1, 2, 3, 4, 5, "wout", "wp", 6, 7, "w1", "w2", 8, 9, 10, 11]
    outs = [loss, dx.reshape(x.shape)]
    for slot, small_vals in ((0, sg), (1, sd), (2, smn), (3, svn)):
        for o in order:
            outs.append(res[o][slot] if isinstance(o, str) else small_vals[o])
    return tuple(outs)
```

```python
import functools

import jax
import jax.numpy as jnp
from jax import lax
from jax.experimental import pallas as pl
from jax.experimental.pallas import tpu as pltpu

F32 = jnp.float32
BF16 = jnp.bfloat16

HEAD_DIM = 64
N_GROUPS = 8
D_STATE = 128
CHUNK = 128
CONV_WIDTH = 5
POOL_WINDOWS = (2, 4, 8, 16)
DEPTH = 4
ALPHA = (2.0 * DEPTH) ** 0.25
LN_EPS = 1e-5
RMS_EPS = 1e-5
ADAM_LR, ADAM_B1, ADAM_B2, ADAM_EPS, ADAM_WD, ADAM_STEP = 0.001, 0.9, 0.999, 1e-08, 0.01, 10

LANES = 128
VMEM_LIMIT = 48 * 1024 * 1024
NEG = -1e30
MESH = pl.DeviceIdType.MESH

NN = (((1,), (0,)), ((), ()))
NT = (((1,), (1,)), ((), ()))
TN = (((0,), (0,)), ((), ()))


def _tile(dim, pref, mult=LANES):
    if dim <= pref:
        return dim
    t = (pref // mult) * mult
    while t > mult and dim % t:
        t -= mult
    assert dim % t == 0, (dim, pref)
    return t


def _cp(sem):
    return pltpu.CompilerParams(dimension_semantics=sem, vmem_limit_bytes=VMEM_LIMIT)


def _sds(shape, dtype):
    return jax.ShapeDtypeStruct(tuple(shape), dtype)


def _dot(a, b, dn=NN):
    return lax.dot_general(a, b, dn, preferred_element_type=F32)


def _bf(x):
    return x.astype(BF16)


def _sigmoid(x):
    return 1.0 / (1.0 + jnp.exp(-x))


def _mm(name, a, b, a_spec, b_spec, dn, grid, tm, tn, out_shapes, out_specs, epi=None, extras=(), extra_specs=()):
    nk = grid[2]
    n_ex, n_out = len(extras), len(out_shapes)

    def kern(*refs):
        a_ref, b_ref = refs[0], refs[1]
        ex = refs[2:2 + n_ex]
        outs = refs[2 + n_ex:2 + n_ex + n_out]
        acc = refs[-1]
        k = pl.program_id(2)

        @pl.when(k == 0)
        def _():
            acc[...] = jnp.zeros_like(acc)

        acc[...] += _dot(_bf(a_ref[...]), _bf(b_ref[...]), dn)

        @pl.when(k == nk - 1)
        def _():
            res = epi(acc[...], *[e[...] for e in ex]) if epi is not None else (acc[...],)
            for o, r in zip(outs, res):
                o[...] = r.astype(o.dtype)

    return pl.pallas_call(
        kern, name=name, grid=grid, in_specs=[a_spec, b_spec, *extra_specs], out_specs=list(out_specs),
        out_shape=list(out_shapes), scratch_shapes=[pltpu.VMEM((tm, tn), F32)],
        compiler_params=_cp(("parallel", "parallel", "arbitrary")))(a, b, *extras)


class _ColShard:
    def __init__(self, R, C):
        self.R, self.C = R, C

    def b_nn(self, tk, tn):
        n = self.C // tn
        return pl.BlockSpec((None, tk, tn), lambda i, j, k: (j // n, k, j % n))

    def b_nt(self, tn, tk):
        n = self.C // tk
        return pl.BlockSpec((None, tn, tk), lambda i, j, k: (k // n, j, k % n))

    def out(self, tm, tn):
        n = self.C // tn
        return pl.BlockSpec((None, tm, tn), lambda i, j, k: (j // n, i, j % n))


class _RowShard:
    def __init__(self, R, C):
        self.R, self.C = R, C

    def b_nn(self, tk, tn):
        n = self.R // tk
        return pl.BlockSpec((None, tk, tn), lambda i, j, k: (k // n, k % n, j))

    def b_nt(self, tn, tk):
        n = self.R // tn
        return pl.BlockSpec((None, tn, tk), lambda i, j, k: (j // n, j % n, k))

    def out(self, tm, tn):
        n = self.R // tm
        return pl.BlockSpec((None, tm, tn), lambda i, j, k: (i // n, i % n, j))


def _a_nn(tm, tk):
    return pl.BlockSpec((tm, tk), lambda i, j, k: (i, k))


def _a_tn(tk, tm):
    return pl.BlockSpec((tk, tm), lambda i, j, k: (k, i))


def _b_tn(tk, tn):
    return pl.BlockSpec((tk, tn), lambda i, j, k: (k, j))


def _o_ij(tm, tn):
    return pl.BlockSpec((tm, tn), lambda i, j, k: (i, j))


def _mm_act(name, a, w, wspec, M, N, K, out_dtypes, epi=None, extras=(), nt=False, tn_pref=1024, tk_pref=1024):
    tm = _tile(M, 1024)
    if isinstance(wspec, (_ColShard, _RowShard)):
        nlim, klim = (wspec.R, wspec.C) if nt else (wspec.C, wspec.R)
        tn = _tile(nlim, tn_pref)
        tk = _tile(klim, tk_pref)
        b_spec = wspec.b_nt(tn, tk) if nt else wspec.b_nn(tk, tn)
    else:
        tn = _tile(N, tn_pref)
        tk = _tile(K, tk_pref)
        b_spec = (pl.BlockSpec((tn, tk), lambda i, j, k: (j, k)) if nt
                  else pl.BlockSpec((tk, tn), lambda i, j, k: (k, j)))
    grid = (M // tm, N // tn, K // tk)
    outs = [_sds((M, N), dt) for dt in out_dtypes]
    return _mm(name, a, w, _a_nn(tm, tk), b_spec, NT if nt else NN, grid, tm, tn, outs,
               [_o_ij(tm, tn)] * len(outs), epi, extras, [_o_ij(tm, tn)] * len(extras))


def _mm_wgrad(name, a, b, M, N, K, out_shape, out_spec_fn, tm_pref=1024, tn_pref=1024):
    tm = _tile(M, tm_pref)
    tn = _tile(N, tn_pref)
    tk = _tile(K, 1024)
    grid = (M // tm, N // tn, K // tk)
    return _mm(name, a, b, _a_tn(tk, tm), _b_tn(tk, tn), TN, grid, tm, tn, [out_shape], [out_spec_fn(tm, tn)])[0]


def _ln_stats(r):
    mu = jnp.mean(r, axis=-1, keepdims=True)
    xc = r - mu
    var = jnp.mean(xc * xc, axis=-1, keepdims=True)
    return xc, lax.rsqrt(var + LN_EPS)


def _part8(v):
    return v.reshape(v.shape[0] // 8, 8, v.shape[1]).sum(axis=0)


def _acc_out(ref, val, first):
    @pl.when(first)
    def _():
        ref[...] = val

    @pl.when(jnp.logical_not(first))
    def _():
        ref[...] += val


def _resln(name, x, mix, g, b):
    T, D = x.shape
    tr = _tile(T, 256, 8)

    def kern(x_ref, m_ref, g_ref, b_ref, r_ref, y_ref, yb_ref):
        r = ALPHA * x_ref[...] + m_ref[...]
        xc, rstd = _ln_stats(r)
        y = xc * rstd * g_ref[...] + b_ref[...]
        r_ref[...] = r
        y_ref[...] = y
        yb_ref[...] = _bf(y)

    row = pl.BlockSpec((tr, D), lambda i: (i, 0))
    vec = pl.BlockSpec((1, D), lambda i: (0, 0))
    return pl.pallas_call(kern, name=name, grid=(T // tr,), in_specs=[row, row, vec, vec], out_specs=[row, row, row],
                          out_shape=[_sds((T, D), F32), _sds((T, D), F32), _sds((T, D), BF16)],
                          compiler_params=_cp(("parallel",)))(x, mix, g, b)


def _lnbwd(name, dy, r, g):
    T, D = r.shape
    tr = _tile(T, 256, 8)

    def kern(dy_ref, r_ref, g_ref, dr_ref, drb_ref, dg_ref, db_ref):
        dyv = dy_ref[...]
        xc, rstd = _ln_stats(r_ref[...])
        xh = xc * rstd
        dxh = dyv * g_ref[...]
        m1 = jnp.mean(dxh, axis=-1, keepdims=True)
        m2 = jnp.mean(dxh * xh, axis=-1, keepdims=True)
        dr = rstd * (dxh - m1 - xh * m2)
        dr_ref[...] = dr
        drb_ref[...] = _bf(dr)
        first = pl.program_id(0) == 0
        _acc_out(dg_ref, _part8(dyv * xh), first)
        _acc_out(db_ref, _part8(dyv), first)

    row = pl.BlockSpec((tr, D), lambda i: (i, 0))
    vec = pl.BlockSpec((1, D), lambda i: (0, 0))
    acc = pl.BlockSpec((8, D), lambda i: (0, 0))
    return pl.pallas_call(kern, name=name, grid=(T // tr,), in_specs=[row, row, vec], out_specs=[row, row, acc, acc],
                          out_shape=[_sds((T, D), F32), _sds((T, D), BF16), _sds((8, D), F32), _sds((8, D), F32)],
                          compiler_params=_cp(("arbitrary",)))(dy, r, g)


def _loss_head(name, y, tgt):
    T, D = y.shape
    tr = _tile(T, 256, 8)
    nt = T // tr

    def kern(y_ref, t_ref, dy_ref, loss_ref, acc):
        i = pl.program_id(0)
        e = y_ref[...] - t_ref[...]
        dy_ref[...] = e * (1.0 / D)
        _acc_out(acc, _part8(e * e), i == 0)

        @pl.when(i == nt - 1)
        def _():
            tot = jnp.sum(jnp.sum(acc[...], axis=1, keepdims=True), axis=0, keepdims=True)
            loss_ref[...] = jnp.broadcast_to(tot * (0.5 / D), loss_ref.shape)

    row = pl.BlockSpec((tr, D), lambda i: (i, 0))
    return pl.pallas_call(kern, name=name, grid=(nt,), in_specs=[row, row],
                          out_specs=[row, pl.BlockSpec((8, LANES), lambda i: (0, 0))],
                          out_shape=[_sds((T, D), F32), _sds((8, LANES), F32)],
                          scratch_shapes=[pltpu.VMEM((8, D), F32)], compiler_params=_cp(("arbitrary",)))(y, tgt)


def _shift(x, o):
    if o == 0:
        return x
    T = x.shape[0]
    rolled = pltpu.roll(x, (-o) % T, 0)
    t = lax.broadcasted_iota(jnp.int32, x.shape, 0)
    return jnp.where((t + o >= 0) & (t + o < T), rolled, 0.0)


def _run(u, h, step):
    s, k = u, 1
    while k < h:
        s = s + _shift(s, step * k)
        k *= 2
    return s


def _winsum(u, win, transposed):
    h = win // 2
    if not transposed:
        return _run(u, h, 1) + _shift(_run(u, h, -1), -1)
    return _run(u, h, -1) + _shift(_run(u, h, 1), 1)


def _wincount(shape, win):
    t = lax.broadcasted_iota(jnp.int32, shape, 0)
    T = shape[0]
    lo = jnp.maximum(t - win // 2, 0)
    hi = jnp.minimum(t - win // 2 + win, T)
    return (hi - lo).astype(F32)


def _pool_m(name, u):
    T, D = u.shape
    per = (D // len(POOL_WINDOWS)) // LANES

    def kern(u_ref, m_ref):
        j = pl.program_id(0)
        for gi, win in enumerate(POOL_WINDOWS):
            @pl.when(j // per == gi)
            def _():
                uv = u_ref[...]
                m_ref[...] = _bf(_winsum(uv, win, False) / _wincount(uv.shape, win) - uv)

    col = pl.BlockSpec((T, LANES), lambda j: (0, j))
    return pl.pallas_call(kern, name=name, grid=(D // LANES,), in_specs=[col], out_specs=col,
                          out_shape=_sds((T, D), BF16), compiler_params=_cp(("parallel",)))(u)


def _pool_fwd(name, m, x, w, bias, scale, g, b):
    T, D = x.shape
    ng = len(POOL_WINDOWS)
    dg = D // ng
    tr = _tile(T, 256, 16)

    def kern(m_ref, x_ref, w_ref, bias_ref, sc_ref, g_ref, b_ref, yp_ref, r_ref, y_ref, yb_ref):
        for gi in range(ng):
            sl = slice(gi * dg, (gi + 1) * dg)
            yp_ref[:, sl] = _dot(m_ref[:, sl], w_ref[gi]) + bias_ref[:, sl]
        r = ALPHA * x_ref[...] + yp_ref[...] * sc_ref[...]
        xc, rstd = _ln_stats(r)
        y = xc * rstd * g_ref[...] + b_ref[...]
        r_ref[...] = r
        y_ref[...] = y
        yb_ref[...] = _bf(y)

    row = pl.BlockSpec((tr, D), lambda i: (i, 0))
    vec = pl.BlockSpec((1, D), lambda i: (0, 0))
    wsp = pl.BlockSpec((ng, dg, dg), lambda i: (0, 0, 0))
    return pl.pallas_call(kern, name=name, grid=(T // tr,), in_specs=[row, row, wsp, vec, vec, vec, vec],
                          out_specs=[row, row, row, row],
                          out_shape=[_sds((T, D), F32), _sds((T, D), F32), _sds((T, D), F32), _sds((T, D), BF16)],
                          compiler_params=_cp(("parallel",)))(m, x, w, bias, scale, g, b)


def _pool_bwd_a(name, dr, ypre, scale, w):
    T, D = dr.shape
    ng = len(POOL_WINDOWS)
    dg = D // ng
    tr = _tile(T, 256, 16)

    def kern(dr_ref, yp_ref, sc_ref, w_ref, dm_ref, dyp_ref, dsc_ref, dbi_ref):
        drv = dr_ref[...]
        dyp = drv * sc_ref[...]
        dyp_ref[...] = _bf(dyp)
        for gi in range(ng):
            sl = slice(gi * dg, (gi + 1) * dg)
            dm_ref[:, sl] = _dot(dyp_ref[:, sl], w_ref[gi], NT)
        first = pl.program_id(0) == 0
        _acc_out(dsc_ref, _part8(drv * yp_ref[...]), first)
        _acc_out(dbi_ref, _part8(dyp), first)

    row = pl.BlockSpec((tr, D), lambda i: (i, 0))
    vec = pl.BlockSpec((1, D), lambda i: (0, 0))
    acc = pl.BlockSpec((8, D), lambda i: (0, 0))
    wsp = pl.BlockSpec((ng, dg, dg), lambda i: (0, 0, 0))
    return pl.pallas_call(kern, name=name, grid=(T // tr,), in_specs=[row, row, vec, wsp],
                          out_specs=[row, row, acc, acc],
                          out_shape=[_sds((T, D), F32), _sds((T, D), BF16), _sds((8, D), F32), _sds((8, D), F32)],
                          compiler_params=_cp(("arbitrary",)))(dr, ypre, scale, w)


def _pool_bwd_win(name, dm, dr):
    T, D = dm.shape
    per = (D // len(POOL_WINDOWS)) // LANES

    def kern(dm_ref, dr_ref, du_ref):
        j = pl.program_id(0)
        for gi, win in enumerate(POOL_WINDOWS):
            @pl.when(j // per == gi)
            def _():
                dmv = dm_ref[...]
                du_ref[...] = ALPHA * dr_ref[...] + _winsum(dmv / _wincount(dmv.shape, win), win, True) - dmv

    col = pl.BlockSpec((T, LANES), lambda j: (0, j))
    return pl.pallas_call(kern, name=name, grid=(D // LANES,), in_specs=[col, col], out_specs=col,
                          out_shape=_sds((T, D), F32), compiler_params=_cp(("parallel",)))(dm, dr)


def _conv_pre(x, w_ref, b_ref):
    acc = b_ref[...] + w_ref[2:3, :] * x
    for k in (0, 1, 3, 4):
        acc = acc + w_ref[k:k + 1, :] * _shift(x, k - 2)
    return acc


def _conv_fwd(name, zx, blk0, wblk0, w, b, G, cw):
    T = zx.shape[0]
    per = cw // LANES

    def kern(x_ref, w_ref, b_ref, o_ref):
        pre = _conv_pre(x_ref[...], w_ref, b_ref)
        o_ref[0] = pre * _sigmoid(pre)

    return pl.pallas_call(
        kern, name=name, grid=(G * per,),
        in_specs=[pl.BlockSpec((T, LANES), lambda j: (0, blk0 + j)),
                  pl.BlockSpec((CONV_WIDTH, LANES), lambda j: (0, wblk0 + j)),
                  pl.BlockSpec((1, LANES), lambda j: (0, wblk0 + j))],
        out_specs=pl.BlockSpec((1, T, LANES), lambda j: (j // per, 0, j % per)),
        out_shape=_sds((G, T, cw), F32), compiler_params=_cp(("parallel",)))(zx, w, b)


def _conv_bwd(name, zx, blk0, wblk0, w, b, adds, dyd=None):
    T = zx.shape[0]
    G, _, cw = adds[0].shape
    per = cw // LANES
    na = len(adds)

    def kern(*refs):
        x_ref, w_ref, b_ref = refs[:3]
        add_refs = refs[3:3 + na]
        rest = refs[3 + na:]
        if dyd is not None:
            dy_ref, dv_ref = rest[:2]
            rest = rest[2:]
        dx_ref, db_ref = rest[0], rest[1]
        dw_refs = rest[2:]
        x = x_ref[...]
        pre = _conv_pre(x, w_ref, b_ref)
        sg = _sigmoid(pre)
        dact = add_refs[0][0]
        for r in add_refs[1:]:
            dact = dact + r[0]
        if dyd is not None:
            dact = dact + dy_ref[0] * dv_ref[...]
        dpre = dact * (sg * (1.0 + pre * (1.0 - sg)))
        row0 = lax.broadcasted_iota(jnp.int32, (8, LANES), 0) == 0

        def put(ref, v):
            ref[...] = jnp.where(row0, jnp.sum(v, axis=0, keepdims=True), 0.0)

        put(db_ref, dpre)
        dx = w_ref[2:3, :] * dpre
        put(dw_refs[2], dpre * x)
        for k in (0, 1, 3, 4):
            put(dw_refs[k], dpre * _shift(x, k - 2))
            dx = dx + w_ref[k:k + 1, :] * _shift(dpre, 2 - k)
        dx_ref[...] = _bf(dx)

    gsp = pl.BlockSpec((1, T, LANES), lambda j: (j // per, 0, j % per))
    in_specs = [pl.BlockSpec((T, LANES), lambda j: (0, blk0 + j)),
                pl.BlockSpec((CONV_WIDTH, LANES), lambda j: (0, wblk0 + j)),
                pl.BlockSpec((1, LANES), lambda j: (0, wblk0 + j))] + [gsp] * na
    args = [zx, w, b, *adds]
    if dyd is not None:
        in_specs += [gsp, pl.BlockSpec((1, LANES), lambda j: (0, j))]
        args += list(dyd)
    n = G * cw
    p8 = pl.BlockSpec((8, LANES), lambda j: (0, j))
    res = pl.pallas_call(
        kern, name=name, grid=(G * per,), in_specs=in_specs,
        out_specs=[pl.BlockSpec((T, LANES), lambda j: (0, j))] + [p8] * (1 + CONV_WIDTH),
        out_shape=[_sds((T, n), BF16)] + [_sds((8, n), F32)] * (1 + CONV_WIDTH),
        compiler_params=_cp(("parallel",)))(*args)
    return res[0], res[1], res[2:]


def _split3(x):
    x1 = _bf(x)
    r1 = x - x1.astype(F32)
    x2 = _bf(r1)
    x3 = _bf(r1 - x2.astype(F32))
    return x1, x2, x3


def _tri_dot(tri, x, dn=NN):
    a, b, c = _split3(x)
    return _dot(tri, a, dn) + _dot(tri, b, dn) + _dot(tri, c, dn)


def _tri(lower):
    i = lax.broadcasted_iota(jnp.int32, (CHUNK, CHUNK), 0)
    j = lax.broadcasted_iota(jnp.int32, (CHUNK, CHUNK), 1)
    return jnp.where((i >= j) if lower else (i <= j), 1.0, 0.0).astype(BF16)


def _softplus(x):
    return jnp.maximum(x, 0.0) + jnp.log(1.0 + jnp.exp(-jnp.abs(x)))


def _dt_prep(name, zx, dtblk, dt_bias, a_log):
    T = zx.shape[0]
    H = LANES // 2

    def kern(x_ref, bias_ref, al_ref, dt_ref, acs_ref, acst_ref):
        dt = _softplus(x_ref[...] + bias_ref[...])
        dta = dt * (-jnp.exp(al_ref[...]))
        lane = lax.broadcasted_iota(jnp.int32, dta.shape, 1)
        dt_ref[...] = dt
        acs = jnp.where(lane < H, _tri_dot(_tri(True), dta), _tri_dot(_tri(False), dta))
        acs_ref[...] = acs
        acst_ref[...] = acs.T

    blk = pl.BlockSpec((CHUNK, LANES), lambda c: (c, 0))
    vec = pl.BlockSpec((1, LANES), lambda c: (0, 0))
    return pl.pallas_call(kern, name=name, grid=(T // CHUNK,),
                          in_specs=[pl.BlockSpec((CHUNK, LANES), lambda c: (c, dtblk)), vec, vec],
                          out_specs=[blk, blk, pl.BlockSpec((LANES, CHUNK), lambda c: (0, c))],
                          out_shape=[_sds((T, LANES), F32), _sds((T, LANES), F32), _sds((LANES, T), F32)],
                          compiler_params=_cp(("parallel",)))(zx, dt_bias, a_log)


def _dt_bwd(name, ddta, ddtx, zx, dtblk, dt_bias, a_log):
    T = zx.shape[0]

    def kern(da_ref, dx_ref, x_ref, bias_ref, al_ref, draw_ref, dbias_ref, dal_ref):
        pre = x_ref[...] + bias_ref[...]
        dt = _softplus(pre)
        A = -jnp.exp(al_ref[...])
        dav = da_ref[...]
        draw = (dav * A + dx_ref[...]) * _sigmoid(pre)
        draw_ref[...] = _bf(draw)
        first = pl.program_id(0) == 0
        _acc_out(dbias_ref, _part8(draw), first)
        _acc_out(dal_ref, _part8(dav * dt) * A, first)

    blk = pl.BlockSpec((CHUNK, LANES), lambda c: (c, 0))
    vec = pl.BlockSpec((1, LANES), lambda c: (0, 0))
    acc = pl.BlockSpec((8, LANES), lambda c: (0, 0))
    return pl.pallas_call(kern, name=name, grid=(T // CHUNK,),
                          in_specs=[blk, blk, pl.BlockSpec((CHUNK, LANES), lambda c: (c, dtblk)), vec, vec],
                          out_specs=[blk, acc, acc],
                          out_shape=[_sds((T, LANES), BF16), _sds((8, LANES), F32), _sds((8, LANES), F32)],
                          compiler_params=_cp(("arbitrary",)))(ddta, ddtx, zx, dt_bias, a_log)


def _ssd_specs(T, GW, hpg, cmap):
    nc = T // CHUNK
    xs = pl.BlockSpec((1, CHUNK, GW), lambda g, c: (g, cmap(c), 0))
    bc = pl.BlockSpec((1, CHUNK, D_STATE), lambda g, c: (g, cmap(c), 0))
    nat = pl.BlockSpec((CHUNK, LANES), lambda g, c: (cmap(c), 0))
    natT = pl.BlockSpec((LANES, CHUNK), lambda g, c: (0, cmap(c)))
    st = pl.BlockSpec((1, 1, D_STATE, GW), lambda g, c: (g, cmap(c), 0, 0))
    ocol = pl.BlockSpec((1, CHUNK, hpg), lambda g, c: (g, cmap(c), 0))
    return nc, xs, bc, nat, natT, st, ocol


def _dot3(x, sel, dn=NN):
    a, b, c = _split3(x)
    return _dot(a, sel, dn) + _dot(b, sel, dn) + _dot(c, sel, dn)


def _head_select(base, GW):
    k = lax.broadcasted_iota(jnp.int32, (LANES, GW), 0)
    j = lax.broadcasted_iota(jnp.int32, (LANES, GW), 1)
    lo = (k - base) * HEAD_DIM
    return jnp.where((j >= lo) & (j < lo + HEAD_DIM), 1.0, 0.0).astype(BF16)


def _head_collect(GW):
    j = lax.broadcasted_iota(jnp.int32, (GW, LANES), 0)
    k = lax.broadcasted_iota(jnp.int32, (GW, LANES), 1)
    return jnp.where((j >= k * HEAD_DIM) & (j < (k + 1) * HEAD_DIM), 1.0, 0.0).astype(BF16)


def _chunk_mask(rev):
    li = lax.broadcasted_iota(jnp.int32, (CHUNK, CHUNK), 0)
    si = lax.broadcasted_iota(jnp.int32, (CHUNK, CHUNK), 1)
    return (li <= si) if rev else (li >= si)


def _ssd_fwd(name, xs, Bg, Cg, dt, acs, acsT, d):
    G, T, GW = xs.shape
    hpg = GW // HEAD_DIM
    rev = d == 1
    nc0 = T // CHUNK
    cmap = (lambda c: nc0 - 1 - c) if rev else (lambda c: c)
    nc, xs_s, bc_s, nat_s, natT_s, st_s, _ = _ssd_specs(T, GW, hpg, cmap)
    last = 0 if rev else CHUNK - 1

    def kern(xs_ref, b_ref, c_ref, dt_ref, ac_ref, art_ref, y_ref, st_ref, state):
        @pl.when(pl.program_id(1) == 0)
        def _():
            state[...] = jnp.zeros_like(state)

        base = d * (G * hpg) + pl.program_id(0) * hpg
        Hp = state[...]
        st_ref[0, 0] = Hp
        Bm = _bf(b_ref[0])
        Cm = _bf(c_ref[0])
        S = _dot(Cm, Bm, NT)
        mask = _chunk_mask(rev)
        sel = _head_select(base, GW)
        dt_e = _dot3(dt_ref[...], sel)
        a_e = _dot3(ac_ref[...], sel)
        xdt = xs_ref[0] * dt_e
        a_end = a_e[last:last + 1, :]
        yo = _dot(Cm, _bf(Hp)) * jnp.exp(a_e)
        for r in range(hpg):
            hs = slice(r * HEAD_DIM, (r + 1) * HEAD_DIM)
            a_col = a_e[:, r * HEAD_DIM:r * HEAD_DIM + 1]
            lam = jnp.exp(jnp.where(mask, a_col - art_ref[pl.ds(base + r, 1), :], NEG))
            y_ref[0, :, hs] = _dot(_bf(S * lam), _bf(xdt[:, hs])) + yo[:, hs]
        Hn = _dot(Bm, _bf(xdt * jnp.exp(a_end - a_e)), TN)
        state[...] = jnp.exp(a_end) * Hp + Hn

    return pl.pallas_call(
        kern, name=name, grid=(G, nc), in_specs=[xs_s, bc_s, bc_s, nat_s, nat_s, natT_s], out_specs=[xs_s, st_s],
        out_shape=[_sds((G, T, GW), F32), _sds((G, nc, D_STATE, GW), F32)],
        scratch_shapes=[pltpu.VMEM((D_STATE, GW), F32)],
        compiler_params=_cp(("parallel", "arbitrary")))(xs, Bg, Cg, dt, acs, acsT)


def _ssd_bwd(name, dy, xs, Bg, Cg, dt, acs, acsT, states, d):
    G, T, GW = xs.shape
    hpg = GW // HEAD_DIM
    rev = d == 1
    nc0 = T // CHUNK
    cmap = (lambda c: c) if rev else (lambda c: nc0 - 1 - c)
    nc, xs_s, bc_s, nat_s, natT_s, st_s, ocol_s = _ssd_specs(T, GW, hpg, cmap)
    last = 0 if rev else CHUNK - 1

    def kern(dy_ref, xs_ref, b_ref, c_ref, dt_ref, ac_ref, art_ref, st_ref,
             dxs_ref, db_ref, dc_ref, dda_ref, ddx_ref, dstate, dxq):
        @pl.when(pl.program_id(1) == 0)
        def _():
            dstate[...] = jnp.zeros_like(dstate)

        base = d * (G * hpg) + pl.program_id(0) * hpg
        Bm = _bf(b_ref[0])
        Cm = _bf(c_ref[0])
        S = _dot(Cm, Bm, NT)
        mask = _chunk_mask(rev)
        sel = _head_select(base, GW)
        col = _head_collect(GW)
        dt_e = _dot3(dt_ref[...], sel)
        a_e = _dot3(ac_ref[...], sel)
        x = xs_ref[0]
        dyv = dy_ref[0]
        xdt = x * dt_e
        a_end = a_e[last:last + 1, :]
        e_end = jnp.exp(a_end)
        dte = jnp.exp(a_end - a_e)
        Hp = st_ref[0, 0]
        dHn = dstate[...]
        Hpb, dHnb = _bf(Hp), _bf(dHn)
        BdH = _dot(Bm, dHnb)
        CHp = _dot(Cm, Hpb)
        Edy = jnp.exp(a_e) * dyv
        wv = dte * xdt
        dCa = _dot(_bf(Edy), Hpb, NT)
        dBa = _dot(_bf(wv), dHnb, NT)
        dstate[...] = e_end * dHn + _dot(Cm, _bf(Edy), TN)
        lane = lax.broadcasted_iota(jnp.int32, (CHUNK, LANES), 1)
        rowi = lax.broadcasted_iota(jnp.int32, (CHUNK, LANES), 0)
        dS = jnp.zeros((CHUNK, CHUNK), F32)
        dq = jnp.zeros((CHUNK, LANES), F32)
        for r in range(hpg):
            hs = slice(r * HEAD_DIM, (r + 1) * HEAD_DIM)
            a_col = a_e[:, r * HEAD_DIM:r * HEAD_DIM + 1]
            lam = jnp.exp(jnp.where(mask, a_col - art_ref[pl.ds(base + r, 1), :], NEG))
            Mf = S * lam
            dyr = _bf(dyv[:, hs])
            dxq[:, hs] = _dot(_bf(Mf), dyr, TN)
            dM = _dot(dyr, _bf(xdt[:, hs]), NT)
            dS = dS + dM * lam
            Gb = _bf(dM * Mf)
            oh = jnp.where(lane == r, 1.0, 0.0).astype(BF16)
            dq = dq + _dot(Gb, oh) - _dot(Gb, oh, TN)
        dxdt = dxq[...] + dte * BdH
        dxs_ref[0] = dxdt * dt_e
        ts = _dot(_bf(wv * BdH), col)
        ddx_ref[0] = _dot(_bf(dxdt * x), col)[:, :hpg]
        hh = _dot3(jnp.broadcast_to(e_end * jnp.sum(dHn * Hp, axis=0, keepdims=True), (8, GW)), col)[0:1, :]
        tot = jnp.sum(ts, axis=0, keepdims=True) + hh
        da = dq + _dot(_bf(Edy * CHp), col) - ts + jnp.where(rowi == last, tot, 0.0)
        dda_ref[0] = _tri_dot(_tri(rev), da)[:, :hpg]
        dSb = _bf(dS)
        dc_ref[0] = dCa + _dot(dSb, Bm)
        db_ref[0] = dBa + _dot(dSb, Cm, TN)

    return pl.pallas_call(
        kern, name=name, grid=(G, nc), in_specs=[xs_s, xs_s, bc_s, bc_s, nat_s, nat_s, natT_s, st_s],
        out_specs=[xs_s, bc_s, bc_s, ocol_s, ocol_s],
        out_shape=[_sds((G, T, GW), F32), _sds((G, T, D_STATE), F32), _sds((G, T, D_STATE), F32),
                   _sds((G, T, hpg), F32), _sds((G, T, hpg), F32)],
        scratch_shapes=[pltpu.VMEM((D_STATE, GW), F32), pltpu.VMEM((CHUNK, GW), F32)],
        compiler_params=_cp(("parallel", "arbitrary")))(dy, xs, Bg, Cg, dt, acs, acsT, states)


def _gate_core(yf, yb, xs, z, dv):
    y = yf + yb + xs * dv
    sg = _sigmoid(z)
    sz = z * sg
    gy = y * sz
    rstd = lax.rsqrt(jnp.mean(gy * gy, axis=-1, keepdims=True) + RMS_EPS)
    return y, sg, sz, gy, rstd


def _gate_fwd(name, yf, yb, xs, zx, dvec, nw):
    G, T, GW = xs.shape
    tr = _tile(T, 512, 16)

    def kern(yf_ref, yb_ref, xs_ref, z_ref, dv_ref, nw_ref, o_ref):
        _, _, _, gy, rstd = _gate_core(yf_ref[0], yb_ref[0], xs_ref[0], z_ref[...], dv_ref[...])
        o_ref[...] = _bf(gy * rstd * nw_ref[...])

    gsp = pl.BlockSpec((1, tr, GW), lambda g, t: (g, t, 0))
    zsp = pl.BlockSpec((tr, GW), lambda g, t: (t, g))
    vsp = pl.BlockSpec((1, GW), lambda g, t: (0, g))
    return pl.pallas_call(kern, name=name, grid=(G, T // tr), in_specs=[gsp, gsp, gsp, zsp, vsp, vsp], out_specs=zsp,
                          out_shape=_sds((T, G * GW), BF16),
                          compiler_params=_cp(("parallel", "parallel")))(yf, yb, xs, zx, dvec, nw)


def _gate_bwd(name, dgyn, yf, yb, xs, zx, dvec, nw):
    G, T, GW = xs.shape
    tr = _tile(T, 512, 16)

    def kern(dg_ref, yf_ref, yb_ref, xs_ref, z_ref, dv_ref, nw_ref, dy_ref, dz_ref, dnw_ref, ddl_ref):
        xsv = xs_ref[0]
        zv = z_ref[...]
        y, sg, sz, gy, rstd = _gate_core(yf_ref[0], yb_ref[0], xsv, zv, dv_ref[...])
        n = gy * rstd
        dgv = dg_ref[...]
        dn = dgv * nw_ref[...]
        dgy = rstd * (dn - n * jnp.mean(dn * n, axis=-1, keepdims=True))
        dyv = dgy * sz
        dy_ref[0] = dyv
        dz_ref[...] = _bf(dgy * y * (sg * (1.0 + zv * (1.0 - sg))))
        first = pl.program_id(1) == 0
        _acc_out(dnw_ref, _part8(dgv * n), first)
        _acc_out(ddl_ref, _part8(dyv * xsv), first)

    gsp = pl.BlockSpec((1, tr, GW), lambda g, t: (g, t, 0))
    zsp = pl.BlockSpec((tr, GW), lambda g, t: (t, g))
    vsp = pl.BlockSpec((1, GW), lambda g, t: (0, g))
    asp = pl.BlockSpec((8, GW), lambda g, t: (0, g))
    return pl.pallas_call(kern, name=name, grid=(G, T // tr), in_specs=[zsp, gsp, gsp, gsp, zsp, vsp, vsp],
                          out_specs=[gsp, zsp, asp, asp],
                          out_shape=[_sds((G, T, GW), F32), _sds((T, G * GW), BF16), _sds((8, G * GW), F32),
                                     _sds((8, G * GW), F32)],
                          compiler_params=_cp(("parallel", "arbitrary")))(dgyn, yf, yb, xs, zx, dvec, nw)


def _head_sum(name, v):
    n = v.shape[1]
    H = n // HEAD_DIM

    def kern(v_ref, o_ref):
        i = lax.broadcasted_iota(jnp.int32, (n, H), 0)
        j = lax.broadcasted_iota(jnp.int32, (n, H), 1)
        sel = jnp.where((i >= j * HEAD_DIM) & (i < (j + 1) * HEAD_DIM), 1.0, 0.0).astype(BF16)
        a, b, c = _split3(v_ref[...])
        o_ref[...] = _dot(a, sel) + _dot(b, sel) + _dot(c, sel)

    return pl.pallas_call(kern, name=name, out_shape=_sds((8, H), F32))(v)


def _relu2_epi(acc):
    return acc, jnp.square(jnp.maximum(acc, 0.0))


def _dh_epi(acc, h):
    return (acc * (2.0 * jnp.maximum(h, 0.0)),)


def _resid_epi(acc, e):
    return (acc + ALPHA * e,)


def _ssd_fwd_layer(tag, x, xb, W, j, mid=None):
    T, D = x.shape
    DI = 2 * D
    DBC = N_GROUPS * D_STATE
    win = W["win"][j]
    NZ = win.shape[1]
    zx = _mm_act(tag + "_inproj", xb, win, None, T, NZ, D, [F32], tn_pref=1152)[0]
    zb = DI // LANES
    cw, cb = W["conv_w"][j], W["conv_b"][j]
    xs = _conv_fwd(tag + "_convx", zx, zb, 0, cw, cb, N_GROUPS, DI // N_GROUPS)
    Bg = _conv_fwd(tag + "_convb", zx, zb + DI // LANES, DI // LANES, cw, cb, N_GROUPS, D_STATE)
    Cg = _conv_fwd(tag + "_convc", zx, zb + (DI + DBC) // LANES, (DI + DBC) // LANES, cw, cb, N_GROUPS, D_STATE)
    dtblk = (2 * DI + 2 * DBC) // LANES
    dt, acs, acsT = _dt_prep(tag + "_dtprep", zx, dtblk, W["dt_bias"][j], W["a_log"][j])
    yf, stf = _ssd_fwd(tag + "_scanf", xs, Bg, Cg, dt, acs, acsT, 0)
    yb, stb = _ssd_fwd(tag + "_scanb", xs, Bg, Cg, dt, acs, acsT, 1)
    gyn = _gate_fwd(tag + "_gate", yf, yb, xs, zx, W["dvec"][j], W["norm_w"][j])
    tok = mid(gyn) if mid is not None else 0.0
    mix = _mm_act(tag + "_outproj", gyn, W["wout"][j], _RowShard(DI // 4, D), T, D, DI, [F32])[0]
    saved = dict(zx=zx, xs=xs, Bg=Bg, Cg=Cg, dt=dt, acs=acs, acsT=acsT, stf=stf, stb=stb,
                 yf=yf, yb=yb, gyn=gyn, dtblk=dtblk)
    return mix, saved, tok


def _ssd_bwd_layer(tag, xb, dr1, dr1b, W, j, s, small):
    T, D = dr1.shape
    DI = 2 * D
    DBC = N_GROUPS * D_STATE
    win = W["win"][j]
    NZ = win.shape[1]
    zx = s["zx"]
    rs = _RowShard(DI // 4, D)
    dgyn = _mm_act(tag + "_dgyn", dr1b, W["wout"][j], rs, T, DI, D, [F32], nt=True)[0]
    p_out = _mm_wgrad(tag + "_dwout", s["gyn"], dr1b, DI, D, T, _sds((4, DI // 4, D), BF16), rs.out)
    dvec, nw = W["dvec"][j], W["norm_w"][j]
    dy, dzb, dnw, ddl = _gate_bwd(tag + "_dgate", dgyn, s["yf"], s["yb"], s["xs"], zx, dvec, nw)
    res = []
    for d, st in ((0, s["stf"]), (1, s["stb"])):
        res.append(_ssd_bwd(tag + "_dscan%d" % d, dy, s["xs"], s["Bg"], s["Cg"], s["dt"], s["acs"], s["acsT"], st, d))
    nat = lambda k: jnp.concatenate([res[d][k].transpose(1, 0, 2).reshape(T, -1) for d in (0, 1)], axis=1)
    drawb, dbias, dal = _dt_bwd(tag + "_ddt", nat(3), nat(4), zx, s["dtblk"], W["dt_bias"][j], W["a_log"][j])
    zb = DI // LANES
    cw, cb = W["conv_w"][j], W["conv_b"][j]
    dxx, dbx, dwx = _conv_bwd(tag + "_dconvx", zx, zb, 0, cw, cb, [res[0][0], res[1][0]], (dy, dvec))
    dxb, dbb, dwb = _conv_bwd(tag + "_dconvb", zx, zb + DI // LANES, DI // LANES, cw, cb, [res[0][1], res[1][1]])
    dxc, dbc, dwc = _conv_bwd(tag + "_dconvc", zx, zb + (DI + DBC) // LANES, (DI + DBC) // LANES, cw, cb,
                              [res[0][2], res[1][2]])
    dzx = jnp.concatenate([dzb, dxx, dxb, dxc, drawb], axis=1)
    dwin = _mm_wgrad(tag + "_dwin", xb, dzx, D, NZ, T, _sds((D, NZ), BF16),
                     lambda tm, tn: pl.BlockSpec((tm, tn), lambda i, jj, k: (i, jj)), tn_pref=1152)
    p_in = dwin.reshape(D, 4, NZ // 4).transpose(1, 0, 2)
    dx = _mm_act(tag + "_dxin", dzx, win, None, T, D, NZ, [F32], epi=_resid_epi, extras=(dr1,), nt=True,
                 tk_pref=1152)[0]
    small["conv_w"].append(jnp.concatenate([p[k] for k in range(CONV_WIDTH) for p in (dwx, dwb, dwc)], axis=1))
    small["conv_b"].append(jnp.concatenate([dbx, dbb, dbc], axis=1))
    small["dt_bias"].append(dbias)
    small["a_log"].append(dal)
    small["d"].append(ddl)
    small["norm_w"].append(dnw)
    return dx, p_in, p_out


def _pool_bwd_layer(tag, dr1, W, j, s, small):
    T, D = dr1.shape
    ng = len(POOL_WINDOWS)
    dg = D // ng
    dm, dypb, dsc, dbi = _pool_bwd_a(tag + "_dpool", dr1, s["ypre"], W["pool_scale"][j], W["wp"][j])
    tk = _tile(T, 1024)
    dwp = _mm(tag + "_dwp", s["m"], dypb, pl.BlockSpec((tk, dg), lambda i, jj, k: (k, i)),
              pl.BlockSpec((tk, dg), lambda i, jj, k: (k, i)), TN, (ng, 1, T // tk), dg, dg,
              [_sds((ng, dg, dg), BF16)], [pl.BlockSpec((None, dg, dg), lambda i, jj, k: (i, 0, 0))])[0]
    p_pool = dwp.reshape(ng, 4, dg // 4, dg).transpose(1, 0, 2, 3).reshape(4, dg, dg)
    dx = _pool_bwd_win(tag + "_dwin", dm, dr1)
    small["pool_b"].append(dbi)
    small["pool_scale"].append(dsc)
    return dx, p_pool


SMALL_NAMES = ("conv_w", "conv_b", "dt_bias", "a_log", "d", "norm_w", "pool_b", "pool_scale",
               "ln_mix_g", "ln_mix_b", "ln_ffn_g", "ln_ffn_b")


def _local_step(x, tgt, W, pre_fwd=None, post_bwd=None, mid_fwd=None):
    T, D = x.shape
    DFF = 4 * D
    cs, rs = _ColShard(D, DFF // 4), _RowShard(DFF // 4, D)
    saved = []
    xb = _bf(x)
    for i in range(DEPTH):
        j = i // 2
        tag = "L%d" % i
        b_mix = W["ln_mix_b"][i]
        if pre_fwd is not None:
            tok = pre_fwd(i, x)
            if i % 2 == 0:
                xb = xb + tok.astype(BF16)
            else:
                x = x + tok
        s = dict(x=x, xb=xb)
        if i % 2 == 0:
            mid = (lambda act, i=i: mid_fwd(i, act)) if mid_fwd is not None else None
            mix, ss, mtok = _ssd_fwd_layer(tag, x, xb, W, j, mid)
            s.update(ss)
            r1, x1, x1b = _resln(tag + "_lnmix", x, mix, W["ln_mix_g"][i], b_mix + mtok)
        else:
            m = _pool_m(tag + "_poolm", x)
            ypre, r1, x1, x1b = _pool_fwd(tag + "_pool", m, x, W["wp"][j], W["pool_b"][j], W["pool_scale"][j],
                                          W["ln_mix_g"][i], b_mix)
            s.update(m=m, ypre=ypre)
        h, a = _mm_act(tag + "_mlp1", x1b, W["w1"][i], cs, T, DFF, D, [F32, BF16], epi=_relu2_epi)
        mlp = _mm_act(tag + "_mlp2", a, W["w2"][i], rs, T, D, DFF, [F32])[0]
        r2, x2, x2b = _resln(tag + "_lnffn", x1, mlp, W["ln_ffn_g"][i], W["ln_ffn_b"][i])
        s.update(r1=r1, x1=x1, x1b=x1b, h=h, a=a, r2=r2)
        saved.append(s)
        x, xb = x2, x2b

    dx, loss = _loss_head("loss", x, tgt)
    small = {n: [] for n in SMALL_NAMES}
    P = dict(win=[], wout=[], w1=[], w2=[], wp=[])
    tok = 0.0
    for i in reversed(range(DEPTH)):
        j = i // 2
        tag = "L%d" % i
        s = saved[i]
        dr2, dr2b, dg2, db2 = _lnbwd(tag + "_dlnffn", dx, s["r2"], W["ln_ffn_g"][i] + tok)
        dh = _mm_act(tag + "_dh", dr2b, W["w2"][i], rs, T, DFF, D, [BF16], epi=_dh_epi, extras=(s["h"],), nt=True)[0]
        Pi = dict(w2=_mm_wgrad(tag + "_dw2", s["a"], dr2b, DFF, D, T, _sds((4, DFF // 4, D), BF16), rs.out),
                  w1=_mm_wgrad(tag + "_dw1", s["x1b"], dh, D, DFF, T, _sds((4, D, DFF // 4), BF16), cs.out))
        dx1 = _mm_act(tag + "_dx1", dh, W["w1"][i], cs, T, D, DFF, [F32], epi=_resid_epi, extras=(dr2,), nt=True)[0]
        dr1, dr1b, dg1, db1 = _lnbwd(tag + "_dlnmix", dx1, s["r1"], W["ln_mix_g"][i])
        if i % 2 == 0:
            dx, Pi["win"], Pi["wout"] = _ssd_bwd_layer(tag, s["xb"], dr1, dr1b, W, j, s, small)
        else:
            dx, Pi["wp"] = _pool_bwd_layer(tag, dr1, W, j, s, small)
        for k, v in Pi.items():
            P[k].append(v)
        small["ln_ffn_g"].append(dg2)
        small["ln_ffn_b"].append(db2)
        small["ln_mix_g"].append(dg1)
        small["ln_mix_b"].append(db1)
        if post_bwd is not None:
            tok = post_bwd(i, dx, Pi)
    P = {k: v[::-1] for k, v in P.items()}
    small = {k: jnp.concatenate(v[::-1], axis=1) for k, v in small.items()}
    small["d"] = _head_sum("dD", small["d"])
    return loss, dx, P, small


ANY = pl.BlockSpec(memory_space=pl.ANY)


def _pos():
    return lax.axis_index("x"), lax.axis_index("y"), lax.axis_index("c")


def _other_chips(x, y):
    return [(1 - x, y), (x, 1 - y), (1 - x, 1 - y)]


def _rcopy(src, dst, ssem, rsem, dev):
    return pltpu.make_async_remote_copy(src_ref=src, dst_ref=dst, send_sem=ssem, recv_sem=rsem,
                                        device_id=dev, device_id_type=MESH)


def _gather(name, slabs, split):
    n = len(slabs)

    def body(*refs):
        src, out = refs[:n], refs[n:2 * n]
        ssem, rsem, fssem, frsem, lsem = refs[2 * n:]
        x, y, c = _pos()
        chip = 2 * x + y
        chips = _other_chips(x, y)
        sib = (x, y, 1 - c)

        def mine(t, half):
            if split[t]:
                h = slabs[t].shape[0] // 2
                return src[t].at[pl.ds(half * h, h)]
            return src[t]

        def region(t, ch, half):
            if split[t]:
                h = slabs[t].shape[0] // 2
                return out[t].at[ch, pl.ds(half * h, h)]
            return out[t].at[ch]

        local = [pltpu.make_async_copy(src[t], out[t].at[chip], lsem.at[t]) for t in range(n)]
        for cp in local:
            cp.start()
        sends = []
        for t in range(n):
            for j, (px, py) in enumerate(chips):
                cp = _rcopy(mine(t, c), region(t, chip, c), ssem.at[t, j], rsem.at[t, j], (px, py, c))
                cp.start()
                sends.append(cp)
        for t in range(n):
            for j, (px, py) in enumerate(chips):
                pch = 2 * px + py
                _rcopy(mine(t, c), region(t, pch, c), ssem.at[t, j], rsem.at[t, j], (px, py, c)).wait_recv()
                if split[t]:
                    cp = _rcopy(region(t, pch, c), region(t, pch, c), fssem.at[t, j], frsem.at[t, j], sib)
                    cp.start()
                    sends.append(cp)
        for t in range(n):
            if split[t]:
                for j, (px, py) in enumerate(chips):
                    pch = 2 * px + py
                    _rcopy(region(t, pch, 1 - c), region(t, pch, 1 - c), fssem.at[t, j], frsem.at[t, j],
                           sib).wait_recv()
        for cp in sends:
            cp.wait_send()
        for cp in local:
            cp.wait()

    sem = pltpu.SemaphoreType.DMA
    return pl.pallas_call(
        body, name=name, in_specs=[ANY] * n, out_specs=[ANY] * n,
        out_shape=[_sds((4,) + s.shape, s.dtype) for s in slabs],
        scratch_shapes=[sem((n, 3)), sem((n, 3)), sem((n, 3)), sem((n, 3)), sem((n,))])(*slabs)


def _rs1(name, Ps):
    n = len(Ps)

    def body(*refs):
        src, out = refs[:n], refs[n:2 * n]
        ssem, rsem = refs[2 * n:]
        x, y, c = _pos()
        cps = []
        for t in range(n):
            h = Ps[t].shape[1] // 2
            cp = _rcopy(src[t].at[pl.ds(0, 4), pl.ds((1 - c) * h, h)], out[t], ssem.at[t], rsem.at[t], (x, y, 1 - c))
            cp.start()
            cps.append(cp)
        for cp in cps:
            cp.wait()

    sem = pltpu.SemaphoreType.DMA
    return pl.pallas_call(
        body, name=name, in_specs=[ANY] * n, out_specs=[ANY] * n,
        out_shape=[_sds((4, p.shape[1] // 2, p.shape[2]), p.dtype) for p in Ps],
        scratch_shapes=[sem((n,)), sem((n,))])(*Ps)


def _rs2(name, Qs):
    n = len(Qs)

    def body(*refs):
        src, r2, qo = refs[:n], refs[n:2 * n], refs[2 * n:3 * n]
        ssem, rsem, fssem, frsem, qssem, qrsem, lsem = refs[3 * n:]
        x, y, c = _pos()
        chip = 2 * x + y
        chips = _other_chips(x, y)
        sib = (x, y, 1 - c)
        sends, local = [], []
        for t in range(n):
            h = Qs[t].shape[1]
            lc = pltpu.make_async_copy(src[t].at[chip], qo[t].at[pl.ds(c * h, h)], lsem.at[t])
            lc.start()
            local.append(lc)
            cp = _rcopy(src[t].at[chip], qo[t].at[pl.ds(c * h, h)], qssem.at[t], qrsem.at[t], sib)
            cp.start()
            sends.append(cp)
            for j, (px, py) in enumerate(chips):
                cp = _rcopy(src[t].at[2 * px + py], r2[t].at[j, pl.ds(c * h, h)], ssem.at[t, j], rsem.at[t, j],
                            (px, py, c))
                cp.start()
                sends.append(cp)
        for t in range(n):
            h = Qs[t].shape[1]
            for j, (px, py) in enumerate(chips):
                mine = r2[t].at[j, pl.ds(c * h, h)]
                _rcopy(src[t].at[chip], mine, ssem.at[t, j], rsem.at[t, j], (px, py, c)).wait_recv()
                cp = _rcopy(mine, mine, fssem.at[t, j], frsem.at[t, j], sib)
                cp.start()
                sends.append(cp)
        for t in range(n):
            h = Qs[t].shape[1]
            _rcopy(src[t].at[chip], qo[t].at[pl.ds((1 - c) * h, h)], qssem.at[t], qrsem.at[t], sib).wait_recv()
            for j in range(3):
                theirs = r2[t].at[j, pl.ds((1 - c) * h, h)]
                _rcopy(theirs, theirs, fssem.at[t, j], frsem.at[t, j], sib).wait_recv()
        for cp in sends:
            cp.wait_send()
        for lc in local:
            lc.wait()

    sem = pltpu.SemaphoreType.DMA
    return pl.pallas_call(
        body, name=name, in_specs=[ANY] * n, out_specs=[ANY] * (2 * n),
        out_shape=[_sds((3, 2 * q.shape[1], q.shape[2]), q.dtype) for q in Qs]
        + [_sds((2 * q.shape[1], q.shape[2]), q.dtype) for q in Qs],
        scratch_shapes=[sem((n, 3)), sem((n, 3)), sem((n, 3)), sem((n, 3)), sem((n,)), sem((n,)), sem((n,))])(*Qs)


HBM = pl.BlockSpec(memory_space=pltpu.HBM)
SEMS = pl.BlockSpec(memory_space=pltpu.SEMAPHORE)
EFFECT = pltpu.SideEffectType.DATAFLOW_SIDE_EFFECTING
TOKEN = _sds((8, LANES), F32)


def _in_hbm(a):
    return pltpu.with_memory_space_constraint(a, pltpu.HBM)


def _cast_place(name, w, chip):
    A, B = w.shape
    ta = _row_tile(A, B, 16)

    def kern(s_ref, w_ref, o_ref):
        o_ref[0] = _bf(w_ref[...])

    gs = pltpu.PrefetchScalarGridSpec(
        num_scalar_prefetch=1, grid=(A // ta,), in_specs=[pl.BlockSpec((ta, B), lambda i, s_ref: (i, 0))],
        out_specs=pl.BlockSpec((1, ta, B), lambda i, s_ref: (s_ref[0], i, 0)))
    return pl.pallas_call(kern, name=name, grid_spec=gs, out_shape=_sds((4, A, B), BF16),
                          compiler_params=_cp(("parallel",)))(chip.reshape(1), w)


def _gather_start(name, lands, after):
    n = len(lands)

    def body(*refs):
        land = refs[:n]
        ssem, rsem = refs[n + 1], refs[n + 2]
        token = refs[-1]
        x, y, c = _pos()
        chip = 2 * x + y
        for t in range(n):
            h = lands[t].shape[1] // 2
            mine = land[t].at[chip, pl.ds(c * h, h)]
            for j, (px, py) in enumerate(_other_chips(x, y)):
                _rcopy(mine, mine, ssem.at[3 * t + j], rsem.at[3 * t + j], (px, py, c)).start()
        token[...] = jnp.zeros_like(token)

    sem = pltpu.SemaphoreType.DMA
    res = pl.pallas_call(
        body, name=name, in_specs=[HBM] * n + [ANY],
        out_specs=(SEMS, SEMS, *[HBM] * n, pl.BlockSpec(memory_space=pltpu.VMEM)),
        out_shape=(sem((3 * n,)), sem((3 * n,)), *[pltpu.HBM(l.shape, l.dtype) for l in lands], TOKEN),
        input_output_aliases={t: 2 + t for t in range(n)},
        compiler_params=pltpu.CompilerParams(has_side_effects=EFFECT))(*[_in_hbm(l) for l in lands], after)
    return res[0], res[1], list(res[2:2 + n]), res[-1]


def _gather_wait(name, lands, ssem, rsem, after):
    n = len(lands)

    def body(*refs):
        land = refs[:n]
        ssem_ref, rsem_ref = refs[n], refs[n + 1]
        x, y, c = _pos()
        chip = 2 * x + y
        for t in range(n):
            h = lands[t].shape[1] // 2
            for j, (px, py) in enumerate(_other_chips(x, y)):
                cp = _rcopy(land[t].at[chip, pl.ds(c * h, h)], land[t].at[2 * px + py, pl.ds(c * h, h)],
                            ssem_ref.at[3 * t + j], rsem_ref.at[3 * t + j], (px, py, c))
                cp.wait_send()
                cp.wait_recv()

    return pl.pallas_call(
        body, name=name, in_specs=[HBM] * n + [SEMS, SEMS, ANY], out_specs=[HBM] * n,
        out_shape=[pltpu.HBM(l.shape, l.dtype) for l in lands], input_output_aliases={t: t for t in range(n)},
        compiler_params=pltpu.CompilerParams(has_side_effects=EFFECT))(*lands, ssem, rsem, after)


def _sibling_fill(name, lands):
    n = len(lands)

    def body(*refs):
        src, out = refs[:n], refs[n:2 * n]
        ssem, rsem = refs[2 * n:]
        x, y, c = _pos()
        cps = []
        for t in range(n):
            h = lands[t].shape[1] // 2
            for j, (px, py) in enumerate(_other_chips(x, y)):
                pch = 2 * px + py
                cp = _rcopy(src[t].at[pch, pl.ds(c * h, h)], out[t].at[pch, pl.ds(c * h, h)], ssem.at[t, j],
                            rsem.at[t, j], (x, y, 1 - c))
                cp.start()
                cps.append(cp)
        for cp in cps:
            cp.wait()

    sem = pltpu.SemaphoreType.DMA
    return pl.pallas_call(
        body, name=name, in_specs=[ANY] * n, out_specs=[ANY] * n, out_shape=[_sds(l.shape, l.dtype) for l in lands],
        input_output_aliases={t: t for t in range(n)}, scratch_shapes=[sem((n, 3)), sem((n, 3))])(*lands)


def _rs2_start(name, Qs, after):
    n = len(Qs)
    r2s = [lax.empty((3, 2 * q.shape[1], q.shape[2]), q.dtype) for q in Qs]
    qsibs = [lax.empty(q.shape[1:], q.dtype) for q in Qs]

    def body(*refs):
        src, r2, qsib = refs[:n], refs[n:2 * n], refs[2 * n:3 * n]
        ssem, rsem, qs, qr = refs[3 * n + 1:3 * n + 5]
        token = refs[-1]
        x, y, c = _pos()
        chip = 2 * x + y
        for t in range(n):
            h = Qs[t].shape[1]
            for j, (px, py) in enumerate(_other_chips(x, y)):
                dst = r2[t].at[j, pl.ds(c * h, h)]
                s0 = 6 * t + 2 * j
                _rcopy(src[t].at[2 * px + py], dst, ssem.at[s0], rsem.at[s0], (px, py, c)).start()
                _rcopy(src[t].at[2 * px + py], dst, ssem.at[s0 + 1], rsem.at[s0 + 1], (px, py, 1 - c)).start()
            _rcopy(src[t].at[chip], qsib[t], qs.at[t], qr.at[t], (x, y, 1 - c)).start()
        token[...] = jnp.zeros_like(token)

    sem = pltpu.SemaphoreType.DMA
    bufs = list(Qs) + r2s + qsibs
    res = pl.pallas_call(
        body, name=name, in_specs=[HBM] * (3 * n) + [ANY],
        out_specs=(SEMS, SEMS, SEMS, SEMS, *[HBM] * (3 * n), pl.BlockSpec(memory_space=pltpu.VMEM)),
        out_shape=(sem((6 * n,)), sem((6 * n,)), sem((n,)), sem((n,)),
                   *[pltpu.HBM(b.shape, b.dtype) for b in bufs], TOKEN),
        input_output_aliases={t: 4 + t for t in range(3 * n)},
        compiler_params=pltpu.CompilerParams(has_side_effects=EFFECT))(*[_in_hbm(b) for b in bufs], after)
    return res[:4], list(res[4:4 + 3 * n]), res[-1]


def _rs2_wait(name, sems, bufs, after):
    n = len(bufs) // 3

    def body(*refs):
        src, r2, qsib = refs[:n], refs[n:2 * n], refs[2 * n:3 * n]
        ssem, rsem, qs, qr = refs[3 * n:3 * n + 4]
        x, y, c = _pos()
        chip = 2 * x + y
        for t in range(n):
            h = bufs[t].shape[1]
            for j, (px, py) in enumerate(_other_chips(x, y)):
                for k, pc in ((0, c), (1, 1 - c)):
                    s0 = 6 * t + 2 * j + k
                    cp = _rcopy(src[t].at[chip], r2[t].at[j, pl.ds(pc * h, h)], ssem.at[s0], rsem.at[s0],
                                (px, py, pc))
                    cp.wait_send()
                    cp.wait_recv()
            cp = _rcopy(src[t].at[chip], qsib[t], qs.at[t], qr.at[t], (x, y, 1 - c))
            cp.wait_send()
            cp.wait_recv()

    res = pl.pallas_call(
        body, name=name, in_specs=[HBM] * (3 * n) + [SEMS] * 4 + [ANY], out_specs=[HBM] * (3 * n),
        out_shape=[pltpu.HBM(b.shape, b.dtype) for b in bufs], input_output_aliases={t: t for t in range(3 * n)},
        compiler_params=pltpu.CompilerParams(has_side_effects=EFFECT))(*bufs, *sems, after)
    return list(res[:n]), list(res[n:2 * n]), list(res[2 * n:])


def _allgather_small(name, v):
    def body(v_ref, out_ref, ssem, rsem, lsem):
        x, y, c = _pos()
        me = 4 * x + 2 * y + c
        lc = pltpu.make_async_copy(v_ref, out_ref.at[me], lsem)
        lc.start()
        cps = []
        for k in range(1, 8):
            flip = lambda a, bit: (1 - a) if bit else a
            peer = (flip(x, k & 4), flip(y, k & 2), flip(c, k & 1))
            cp = _rcopy(v_ref, out_ref.at[me], ssem.at[k - 1], rsem.at[k - 1], peer)
            cp.start()
            cps.append(cp)
        for cp in cps:
            cp.wait()
        lc.wait()

    sem = pltpu.SemaphoreType.DMA
    return pl.pallas_call(body, name=name, in_specs=[ANY], out_specs=ANY, out_shape=_sds((8,) + v.shape, v.dtype),
                          scratch_shapes=[sem((7,)), sem((7,)), sem])(v)


def _row_tile(R, C, mult):
    return _tile(R, max(mult, (1 << 19) // C), mult)


def _sum1(name, P, R1, c):
    _, A, B = P.shape
    h = A // 2
    ta = _row_tile(h, B, 16)
    nb = h // ta

    def kern(c_ref, p_ref, r_ref, q_ref):
        q_ref[...] = _bf(p_ref[...].astype(F32) + r_ref[...].astype(F32))

    gs = pltpu.PrefetchScalarGridSpec(
        num_scalar_prefetch=1, grid=(4, nb),
        in_specs=[pl.BlockSpec((1, ta, B), lambda s, i, c_ref: (s, c_ref[0] * nb + i, 0)),
                  pl.BlockSpec((1, ta, B), lambda s, i, c_ref: (s, i, 0))],
        out_specs=pl.BlockSpec((1, ta, B), lambda s, i, c_ref: (s, i, 0)))
    return pl.pallas_call(kern, name=name, grid_spec=gs, out_shape=_sds((4, h, B), BF16),
                          compiler_params=_cp(("parallel", "parallel")))(c.reshape(1), P, R1)


def _adam_math(w, gv, m, v):
    mn = ADAM_B1 * m + (1.0 - ADAM_B1) * gv
    vn = ADAM_B2 * v + (1.0 - ADAM_B2) * jnp.square(gv)
    m_hat = mn / (1.0 - ADAM_B1 ** ADAM_STEP)
    v_hat = vn / (1.0 - ADAM_B2 ** ADAM_STEP)
    return -ADAM_LR * (m_hat / (jnp.sqrt(v_hat) + ADAM_EPS) + ADAM_WD * w), mn, vn


def _sum2_adam(name, qo, r2, w, m, v, l, prev):
    L, A, B = w.shape
    ta = _row_tile(A, B, 16)

    def kern(q_ref, r_ref, w_ref, m_ref, v_ref, *rest):
        g_ref, d_ref, mo_ref, vo_ref = rest[-4:]
        gv = q_ref[...].astype(F32)
        for j in range(3):
            gv = gv + r_ref[j].astype(F32)
        g_ref[...] = gv
        d_ref[...], mo_ref[...], vo_ref[...] = _adam_math(w_ref[...], gv, m_ref[...], v_ref[...])

    lay = pl.BlockSpec((None, ta, B), lambda i: (l, i, 0))
    in_specs = [pl.BlockSpec((ta, B), lambda i: (i, 0)), pl.BlockSpec((3, ta, B), lambda i: (0, i, 0)), lay, lay, lay]
    args = [qo, r2, w, m, v]
    aliases = {}
    if prev is not None:
        in_specs += [ANY] * 4
        args += list(prev)
        aliases = {5 + k: k for k in range(4)}
    return pl.pallas_call(kern, name=name, grid=(A // ta,), in_specs=in_specs, out_specs=[lay] * 4,
                          out_shape=[_sds((L, A, B), F32)] * 4, input_output_aliases=aliases,
                          compiler_params=_cp(("parallel",)))(*args)


def _adam(name, w, g, m, v):
    R, C = w.shape
    tr = _row_tile(R, C, 8)

    def kern(w_ref, g_ref, m_ref, v_ref, d_ref, mo_ref, vo_ref):
        d_ref[...], mo_ref[...], vo_ref[...] = _adam_math(w_ref[...], g_ref[...], m_ref[...], v_ref[...])

    blk = pl.BlockSpec((tr, C), lambda i: (i, 0))
    return pl.pallas_call(kern, name=name, grid=(R // tr,), in_specs=[blk] * 4, out_specs=[blk] * 3,
                          out_shape=[_sds((R, C), F32)] * 3, compiler_params=_cp(("parallel",)))(w, g, m, v)


def _rowsum8(name, v):
    n = v.shape[1]
    tn = _tile(n, 16384)

    def kern(v_ref, o_ref):
        o_ref[...] = jnp.sum(v_ref[...], axis=0, keepdims=True)

    return pl.pallas_call(kern, name=name, grid=(n // tn,), in_specs=[pl.BlockSpec((8, tn), lambda i: (0, i))],
                          out_specs=pl.BlockSpec((1, tn), lambda i: (0, i)), out_shape=_sds((1, n), F32))(v)


def _sum_devices(name, v):
    n = v.shape[2]
    tn = _tile(n, 4096)

    def kern(v_ref, o_ref):
        s = v_ref[0]
        for d in range(1, 8):
            s = s + v_ref[d]
        o_ref[...] = s

    return pl.pallas_call(kern, name=name, grid=(n // tn,), in_specs=[pl.BlockSpec((8, 8, tn), lambda i: (0, 0, i))],
                          out_specs=pl.BlockSpec((8, tn), lambda i: (0, i)), out_shape=_sds((8, n), F32))(v)


def _pack8(parts, quantum=8 * LANES):
    flat = jnp.concatenate([p.reshape(-1) for p in parts])
    n = flat.shape[0]
    npad = -n % quantum
    return jnp.pad(flat, (0, npad)).reshape(8, -1), n


def _unpack(flat, shapes):
    out, o = [], 0
    for s in shapes:
        k = 1
        for d in s:
            k *= d
        out.append(flat[o:o + k].reshape(s))
        o += k
    return out


def kernel(x, ssd_in_proj, ssd_conv_w, ssd_conv_b, ssd_dt_bias, ssd_A_log, ssd_D, ssd_norm_w, ssd_out_proj, pool_w, pool_b, pool_scale, mlp_w1, mlp_w2, ln_mix_g, ln_mix_b, ln_ffn_g, ln_ffn_b, loss_target, m_ssd_in_proj, m_ssd_conv_w, m_ssd_conv_b, m_ssd_dt_bias, m_ssd_A_log, m_ssd_D, m_ssd_norm_w, m_ssd_out_proj, m_pool_w, m_pool_b, m_pool_scale, m_mlp_w1, m_mlp_w2, m_ln_mix_g, m_ln_mix_b, m_ln_ffn_g, m_ln_ffn_b, v_ssd_in_proj, v_ssd_conv_w, v_ssd_conv_b, v_ssd_dt_bias, v_ssd_A_log, v_ssd_D, v_ssd_norm_w, v_ssd_out_proj, v_pool_w, v_pool_b, v_pool_scale, v_mlp_w1, v_mlp_w2, v_ln_mix_g, v_ln_mix_b, v_ln_ffn_g, v_ln_ffn_b):
    _, T, D = x.shape
    DI, DFF = 2 * D, 4 * D
    NZ = 4 * ssd_in_proj.shape[2]
    nssd, npool = ssd_in_proj.shape[0], pool_w.shape[0]
    ng = len(POOL_WINDOWS)
    dg = D // ng
    xi, yi, ci = _pos()
    chip = 2 * xi + yi

    def layer_slabs(i):
        j = i // 2
        if i % 2 == 0:
            return ["win", "wout", "w1", "w2"], [ssd_in_proj[j], ssd_out_proj[j], mlp_w1[i], mlp_w2[i]]
        return ["wp", "w1", "w2"], [pool_w[j].reshape(dg, dg), mlp_w1[i], mlp_w2[i]]

    def put_gathered(keys, arrs):
        for k, a in zip(keys, arrs):
            if k == "win":
                a = a.transpose(1, 0, 2).reshape(D, NZ)
            elif k == "wp":
                a = a.reshape(4, ng, dg // 4, dg).transpose(1, 0, 2, 3).reshape(ng, dg, dg)
            W[k].append(a)

    flights = {}

    def start(tag, keys, slabs, after):
        lands = [_cast_place("place_%s_%s" % (tag, k), s, chip) for k, s in zip(keys, slabs)]
        ssem, rsem, lands, token = _gather_start("gather_start_" + tag, lands, after)
        flights[tag] = (keys, lands, ssem, rsem)
        return lands[-1], token[0, 0]

    def finish(tag, after):
        keys, lands, ssem, rsem = flights.pop(tag)
        lands = _gather_wait("gather_wait_" + tag, lands, ssem, rsem, after)
        arrs = _sibling_fill("gather_fill_" + tag, lands)
        put_gathered(keys, arrs)
        return arrs[-1]

    def pre_fwd(i, xcur):
        if i == 0:
            keys, slabs = layer_slabs(0)
            arrs = _gather("gather_L0", [slabs[0].astype(BF16)], [True])
            put_gathered(keys[:1], arrs)
            last, _ = start("L0b", keys[1:], slabs[1:], arrs[0])
            return start("L1", *layer_slabs(1), last)[1]
        last = finish("L%d" % i, xcur)
        if i == 1:
            return start("L3", *layer_slabs(3), last)[1]
        return jnp.zeros((), F32)

    def mid_fwd(i, act):
        if i != 0:
            return jnp.zeros((), F32)
        last = finish("L0b", act)
        return start("L2", *layer_slabs(2), last)[1]

    g_cw, g_pb, g_ps = _gather("gather_small", [ssd_conv_w, pool_b, pool_scale], [False] * 3)
    W = dict(
        win=[], wout=[], w1=[], w2=[], wp=[],
        conv_w=[g_cw[:, j, :, 0, :].transpose(1, 0, 2).reshape(CONV_WIDTH, -1) for j in range(nssd)],
        conv_b=[ssd_conv_b[j].reshape(1, -1) for j in range(nssd)],
        dt_bias=[ssd_dt_bias[j].reshape(1, -1) for j in range(nssd)],
        a_log=[ssd_A_log[j].reshape(1, -1) for j in range(nssd)],
        dvec=[jnp.repeat(ssd_D[j], HEAD_DIM).reshape(1, -1) for j in range(nssd)],
        norm_w=[ssd_norm_w[j].reshape(1, -1) for j in range(nssd)],
        pool_b=[g_pb[:, j].transpose(1, 0, 2).reshape(1, -1) for j in range(npool)],
        pool_scale=[g_ps[:, j].reshape(1, -1) for j in range(npool)],
        ln_mix_g=[ln_mix_g[i].reshape(1, -1) for i in range(DEPTH)],
        ln_mix_b=[ln_mix_b[i].reshape(1, -1) for i in range(DEPTH)],
        ln_ffn_g=[ln_ffn_g[i].reshape(1, -1) for i in range(DEPTH)],
        ln_ffn_b=[ln_ffn_b[i].reshape(1, -1) for i in range(DEPTH)],
    )

    big = dict(win=(ssd_in_proj, m_ssd_in_proj, v_ssd_in_proj), wout=(ssd_out_proj, m_ssd_out_proj, v_ssd_out_proj),
               wp=(pool_w, m_pool_w, v_pool_w), w1=(mlp_w1, m_mlp_w1, v_mlp_w1), w2=(mlp_w2, m_mlp_w2, v_mlp_w2))
    big = {k: tuple(a.reshape(a.shape[0], -1, a.shape[-1]) for a in t) for k, t in big.items()}
    res = {}
    rflight = {}

    def finish_layer(i, keys, qos, r2s):
        for k, qo, r2 in zip(keys, qos, r2s):
            l = i // 2 if k in ("win", "wout", "wp") else i
            res[k] = _sum2_adam("adam_L%d_%s" % (i, k), qo, r2, *big[k], l, res.get(k))

    def post_bwd(i, dxcur, Pi):
        if rflight:
            li, lkeys = rflight["layer"], rflight["keys"]
            Qt, r2s, qsibs = _rs2_wait("rs2_wait_L%d" % li, rflight["sems"], rflight["bufs"], dxcur)
            qos = []
            for q, qsib in zip(Qt, qsibs):
                h = q.shape[1]
                own = lax.dynamic_index_in_dim(q, chip, 0, keepdims=False)
                qos.append(lax.dynamic_update_slice(jnp.concatenate([qsib, qsib], axis=0), own, (ci * h, 0)))
            finish_layer(li, lkeys, qos, r2s)
            rflight.clear()
        keys = ["win", "wout", "w1", "w2"] if i % 2 == 0 else ["wp", "w1", "w2"]
        units = [Pi[k] for k in keys]
        R1 = _rs1("rs1_L%d" % i, units)
        Q = [_sum1("sum1_L%d_%s" % (i, k), p, r, ci) for k, p, r in zip(keys, units, R1)]
        if i == 0:
            R2Q = _rs2("rs2_L0", Q)
            finish_layer(0, keys, R2Q[len(keys):], R2Q[:len(keys)])
            return jnp.zeros((), F32)
        sems, bufs, token = _rs2_start("rs2_start_L%d" % i, Q, Q[-1])
        rflight.update(layer=i, keys=keys, sems=sems, bufs=bufs)
        return token[0, 0]

    loss_blk, dx, P, small = _local_step(x[0], loss_target[0], W, pre_fwd, post_bwd, mid_fwd)
    loss = lax.psum(loss_blk[0, 0], ("x", "y", "c"))
    res = {k: tuple(a.reshape(s.shape) for a in res[k])
           for k, s in dict(win=ssd_in_proj, wout=ssd_out_proj, wp=pool_w, w1=mlp_w1, w2=mlp_w2).items()}

    flat8 = jnp.concatenate([small[n] for n in SMALL_NAMES], axis=1)
    ns = flat8.shape[1]
    flat8 = jnp.pad(flat8, ((0, 0), (0, -ns % (8 * LANES))))
    mine8 = _rowsum8("small_rowsum", flat8).reshape(8, -1)
    tot = _sum_devices("small_sum", _allgather_small("small_allgather", mine8)).reshape(-1)
    sw = [ssd_conv_w, ssd_conv_b, ssd_dt_bias, ssd_A_log, ssd_D, ssd_norm_w, pool_b, pool_scale,
          ln_mix_g, ln_mix_b, ln_ffn_g, ln_ffn_b]
    sm = [m_ssd_conv_w, m_ssd_conv_b, m_ssd_dt_bias, m_ssd_A_log, m_ssd_D, m_ssd_norm_w, m_pool_b, m_pool_scale,
          m_ln_mix_g, m_ln_mix_b, m_ln_ffn_g, m_ln_ffn_b]
    sv = [v_ssd_conv_w, v_ssd_conv_b, v_ssd_dt_bias, v_ssd_A_log, v_ssd_D, v_ssd_norm_w, v_pool_b, v_pool_scale,
          v_ln_mix_g, v_ln_mix_b, v_ln_ffn_g, v_ln_ffn_b]
    full_shapes = [(nssd, CONV_WIDTH, 1, DI + 2 * N_GROUPS * D_STATE)] + [w.shape for w in sw[1:6]] \
        + [(npool, ng, dg), (npool, D)] + [w.shape for w in sw[8:]]
    sg = _unpack(tot, full_shapes)
    sg[0] = lax.dynamic_slice_in_dim(sg[0], chip * sw[0].shape[3], sw[0].shape[3], axis=3)
    sg[6] = lax.dynamic_slice_in_dim(sg[6], chip * sw[6].shape[2], sw[6].shape[2], axis=2)
    sg[7] = lax.dynamic_slice_in_dim(sg[7], chip * sw[7].shape[1], sw[7].shape[1], axis=1)
    packs = [_pack8(parts)[0] for parts in (sw, sg, sm, sv)]
    sd, smn, svn = _adam("adam_small", *packs)
    shapes = [w.shape for w in sw]
    sd, smn, svn = (_unpack(a.reshape(-1), shapes) for a in (sd, smn, svn))

    order = ["win", 0, 1, 2, 3, 4, 5, "wout", "wp", 6, 7, "w1", "w2", 8, 9, 10, 11]
    outs = [loss, dx.reshape(x.shape)]
    for slot, small_vals in ((0, sg), (1, sd), (2, smn), (3, svn)):
        for o in order:
            outs.append(res[o][slot] if isinstance(o, str) else small_vals[o])
    return tuple(outs)
```

```python
import functools

import jax
import jax.numpy as jnp
from jax import lax
from jax.experimental import pallas as pl
from jax.experimental.pallas import tpu as pltpu

F32 = jnp.float32
BF16 = jnp.bfloat16

HEAD_DIM = 64
N_GROUPS = 8
D_STATE = 128
CHUNK = 128
CONV_WIDTH = 5
POOL_WINDOWS = (2, 4, 8, 16)
DEPTH = 4
ALPHA = (2.0 * DEPTH) ** 0.25
LN_EPS = 1e-5
RMS_EPS = 1e-5
ADAM_LR, ADAM_B1, ADAM_B2, ADAM_EPS, ADAM_WD, ADAM_STEP = 0.001, 0.9, 0.999, 1e-08, 0.01, 10

LANES = 128
VMEM_LIMIT = 48 * 1024 * 1024
NEG = -1e30
MESH = pl.DeviceIdType.MESH

NN = (((1,), (0,)), ((), ()))
NT = (((1,), (1,)), ((), ()))
TN = (((0,), (0,)), ((), ()))


def _tile(dim, pref, mult=LANES):
    if dim <= pref:
        return dim
    t = (pref // mult) * mult
    while t > mult and dim % t:
        t -= mult
    assert dim % t == 0, (dim, pref)
    return t


def _cp(sem):
    return pltpu.CompilerParams(dimension_semantics=sem, vmem_limit_bytes=VMEM_LIMIT)


def _sds(shape, dtype):
    return jax.ShapeDtypeStruct(tuple(shape), dtype)


def _dot(a, b, dn=NN):
    return lax.dot_general(a, b, dn, preferred_element_type=F32)


def _bf(x):
    return x.astype(BF16)


def _sigmoid(x):
    return 1.0 / (1.0 + jnp.exp(-x))


def _mm(name, a, b, a_spec, b_spec, dn, grid, tm, tn, out_shapes, out_specs, epi=None, extras=(), extra_specs=()):
    nk = grid[2]
    n_ex, n_out = len(extras), len(out_shapes)

    def finish(acc, ex, outs):
        res = epi(acc, *[e[...] for e in ex]) if epi is not None else (acc,)
        for o, r in zip(outs, res):
            o[...] = r.astype(o.dtype)

    def kern_one(*refs):
        finish(_dot(_bf(refs[0][...]), _bf(refs[1][...]), dn), refs[2:2 + n_ex], refs[2 + n_ex:2 + n_ex + n_out])

    def kern(*refs):
        a_ref, b_ref = refs[0], refs[1]
        acc = refs[-1]
        k = pl.program_id(2)

        @pl.when(k == 0)
        def _():
            acc[...] = jnp.zeros_like(acc)

        acc[...] += _dot(_bf(a_ref[...]), _bf(b_ref[...]), dn)

        @pl.when(k == nk - 1)
        def _():
            finish(acc[...], refs[2:2 + n_ex], refs[2 + n_ex:2 + n_ex + n_out])

    return pl.pallas_call(
        kern_one if nk == 1 else kern, name=name, grid=grid, in_specs=[a_spec, b_spec, *extra_specs],
        out_specs=list(out_specs), out_shape=list(out_shapes),
        scratch_shapes=[] if nk == 1 else [pltpu.VMEM((tm, tn), F32)],
        compiler_params=_cp(("parallel", "parallel", "arbitrary")))(a, b, *extras)


class _ColShard:
    def __init__(self, R, C):
        self.R, self.C = R, C

    def b_nn(self, tk, tn):
        n = self.C // tn
        return pl.BlockSpec((None, tk, tn), lambda i, j, k: (j // n, k, j % n))

    def b_nt(self, tn, tk):
        n = self.C // tk
        return pl.BlockSpec((None, tn, tk), lambda i, j, k: (k // n, j, k % n))

    def out(self, tm, tn):
        n = self.C // tn
        return pl.BlockSpec((None, tm, tn), lambda i, j, k: (j // n, i, j % n))


class _RowShard:
    def __init__(self, R, C):
        self.R, self.C = R, C

    def b_nn(self, tk, tn):
        n = self.R // tk
        return pl.BlockSpec((None, tk, tn), lambda i, j, k: (k // n, k % n, j))

    def b_nt(self, tn, tk):
        n = self.R // tn
        return pl.BlockSpec((None, tn, tk), lambda i, j, k: (j // n, j % n, k))

    def out(self, tm, tn):
        n = self.R // tm
        return pl.BlockSpec((None, tm, tn), lambda i, j, k: (i // n, i % n, j))


def _a_nn(tm, tk):
    return pl.BlockSpec((tm, tk), lambda i, j, k: (i, k))


def _a_tn(tk, tm):
    return pl.BlockSpec((tk, tm), lambda i, j, k: (k, i))


def _b_tn(tk, tn):
    return pl.BlockSpec((tk, tn), lambda i, j, k: (k, j))


def _o_ij(tm, tn):
    return pl.BlockSpec((tm, tn), lambda i, j, k: (i, j))


def _mm_act(name, a, w, wspec, M, N, K, out_dtypes, epi=None, extras=(), nt=False, tn_pref=1024, tk_pref=2048):
    tm = _tile(M, 1024)
    if isinstance(wspec, (_ColShard, _RowShard)):
        nlim, klim = (wspec.R, wspec.C) if nt else (wspec.C, wspec.R)
        tn = _tile(nlim, tn_pref)
        tk = _tile(klim, tk_pref)
        b_spec = wspec.b_nt(tn, tk) if nt else wspec.b_nn(tk, tn)
    else:
        tn = _tile(N, tn_pref)
        tk = _tile(K, tk_pref)
        b_spec = (pl.BlockSpec((tn, tk), lambda i, j, k: (j, k)) if nt
                  else pl.BlockSpec((tk, tn), lambda i, j, k: (k, j)))
    grid = (M // tm, N // tn, K // tk)
    outs = [_sds((M, N), dt) for dt in out_dtypes]
    return _mm(name, a, w, _a_nn(tm, tk), b_spec, NT if nt else NN, grid, tm, tn, outs,
               [_o_ij(tm, tn)] * len(outs), epi, extras, [_o_ij(tm, tn)] * len(extras))


def _mm_wgrad(name, a, b, M, N, K, out_shape, out_spec_fn, tm_pref=1024, tn_pref=1024):
    tm = _tile(M, tm_pref)
    tn = _tile(N, tn_pref)
    tk = _tile(K, 2048)
    grid = (M // tm, N // tn, K // tk)
    return _mm(name, a, b, _a_tn(tk, tm), _b_tn(tk, tn), TN, grid, tm, tn, [out_shape], [out_spec_fn(tm, tn)])[0]


def _ln_stats(r):
    mu = jnp.mean(r, axis=-1, keepdims=True)
    xc = r - mu
    var = jnp.mean(xc * xc, axis=-1, keepdims=True)
    return xc, lax.rsqrt(var + LN_EPS)


def _part8(v):
    return v.reshape(v.shape[0] // 8, 8, v.shape[1]).sum(axis=0)


def _acc_out(ref, val, first):
    @pl.when(first)
    def _():
        ref[...] = val

    @pl.when(jnp.logical_not(first))
    def _():
        ref[...] += val


def _resln(name, x, mix, g, b):
    T, D = x.shape
    tr = _tile(T, 256, 8)

    def kern(x_ref, m_ref, g_ref, b_ref, r_ref, y_ref, yb_ref):
        r = ALPHA * x_ref[...] + m_ref[...]
        xc, rstd = _ln_stats(r)
        y = xc * rstd * g_ref[...] + b_ref[...]
        r_ref[...] = r
        y_ref[...] = y
        yb_ref[...] = _bf(y)

    row = pl.BlockSpec((tr, D), lambda i: (i, 0))
    vec = pl.BlockSpec((1, D), lambda i: (0, 0))
    return pl.pallas_call(kern, name=name, grid=(T // tr,), in_specs=[row, row, vec, vec], out_specs=[row, row, row],
                          out_shape=[_sds((T, D), F32), _sds((T, D), F32), _sds((T, D), BF16)],
                          compiler_params=_cp(("parallel",)))(x, mix, g, b)


def _lnbwd(name, dy, r, g):
    T, D = r.shape
    tr = _tile(T, 256, 8)

    def kern(dy_ref, r_ref, g_ref, dr_ref, drb_ref, dg_ref, db_ref):
        dyv = dy_ref[...]
        xc, rstd = _ln_stats(r_ref[...])
        xh = xc * rstd
        dxh = dyv * g_ref[...]
        m1 = jnp.mean(dxh, axis=-1, keepdims=True)
        m2 = jnp.mean(dxh * xh, axis=-1, keepdims=True)
        dr = rstd * (dxh - m1 - xh * m2)
        dr_ref[...] = dr
        drb_ref[...] = _bf(dr)
        first = pl.program_id(0) == 0
        _acc_out(dg_ref, _part8(dyv * xh), first)
        _acc_out(db_ref, _part8(dyv), first)

    row = pl.BlockSpec((tr, D), lambda i: (i, 0))
    vec = pl.BlockSpec((1, D), lambda i: (0, 0))
    acc = pl.BlockSpec((8, D), lambda i: (0, 0))
    return pl.pallas_call(kern, name=name, grid=(T // tr,), in_specs=[row, row, vec], out_specs=[row, row, acc, acc],
                          out_shape=[_sds((T, D), F32), _sds((T, D), BF16), _sds((8, D), F32), _sds((8, D), F32)],
                          compiler_params=_cp(("arbitrary",)))(dy, r, g)


def _loss_head(name, y, tgt):
    T, D = y.shape
    tr = _tile(T, 256, 8)
    nt = T // tr

    def kern(y_ref, t_ref, dy_ref, loss_ref, acc):
        i = pl.program_id(0)
        e = y_ref[...] - t_ref[...]
        dy_ref[...] = e * (1.0 / D)
        _acc_out(acc, _part8(e * e), i == 0)

        @pl.when(i == nt - 1)
        def _():
            tot = jnp.sum(jnp.sum(acc[...], axis=1, keepdims=True), axis=0, keepdims=True)
            loss_ref[...] = jnp.broadcast_to(tot * (0.5 / D), loss_ref.shape)

    row = pl.BlockSpec((tr, D), lambda i: (i, 0))
    return pl.pallas_call(kern, name=name, grid=(nt,), in_specs=[row, row],
                          out_specs=[row, pl.BlockSpec((8, LANES), lambda i: (0, 0))],
                          out_shape=[_sds((T, D), F32), _sds((8, LANES), F32)],
                          scratch_shapes=[pltpu.VMEM((8, D), F32)], compiler_params=_cp(("arbitrary",)))(y, tgt)


def _shift(x, o):
    if o == 0:
        return x
    T = x.shape[0]
    rolled = pltpu.roll(x, (-o) % T, 0)
    t = lax.broadcasted_iota(jnp.int32, x.shape, 0)
    return jnp.where((t + o >= 0) & (t + o < T), rolled, 0.0)


def _run(u, h, step):
    s, k = u, 1
    while k < h:
        s = s + _shift(s, step * k)
        k *= 2
    return s


def _winsum(u, win, transposed):
    h = win // 2
    if not transposed:
        return _run(u, h, 1) + _shift(_run(u, h, -1), -1)
    return _run(u, h, -1) + _shift(_run(u, h, 1), 1)


def _wincount(shape, win):
    t = lax.broadcasted_iota(jnp.int32, shape, 0)
    T = shape[0]
    lo = jnp.maximum(t - win // 2, 0)
    hi = jnp.minimum(t - win // 2 + win, T)
    return (hi - lo).astype(F32)


def _pool_m(name, u):
    T, D = u.shape
    per = (D // len(POOL_WINDOWS)) // LANES

    def kern(u_ref, m_ref):
        j = pl.program_id(0)
        for gi, win in enumerate(POOL_WINDOWS):
            @pl.when(j // per == gi)
            def _():
                uv = u_ref[...]
                m_ref[...] = _bf(_winsum(uv, win, False) / _wincount(uv.shape, win) - uv)

    col = pl.BlockSpec((T, LANES), lambda j: (0, j))
    return pl.pallas_call(kern, name=name, grid=(D // LANES,), in_specs=[col], out_specs=col,
                          out_shape=_sds((T, D), BF16), compiler_params=_cp(("parallel",)))(u)


def _pool_fwd(name, m, x, w, bias, scale, g, b):
    T, D = x.shape
    ng = len(POOL_WINDOWS)
    dg = D // ng
    tr = _tile(T, 256, 16)

    def kern(m_ref, x_ref, w_ref, bias_ref, sc_ref, g_ref, b_ref, yp_ref, r_ref, y_ref, yb_ref):
        for gi in range(ng):
            sl = slice(gi * dg, (gi + 1) * dg)
            yp_ref[:, sl] = _dot(m_ref[:, sl], w_ref[gi]) + bias_ref[:, sl]
        r = ALPHA * x_ref[...] + yp_ref[...] * sc_ref[...]
        xc, rstd = _ln_stats(r)
        y = xc * rstd * g_ref[...] + b_ref[...]
        r_ref[...] = r
        y_ref[...] = y
        yb_ref[...] = _bf(y)

    row = pl.BlockSpec((tr, D), lambda i: (i, 0))
    vec = pl.BlockSpec((1, D), lambda i: (0, 0))
    wsp = pl.BlockSpec((ng, dg, dg), lambda i: (0, 0, 0))
    return pl.pallas_call(kern, name=name, grid=(T // tr,), in_specs=[row, row, wsp, vec, vec, vec, vec],
                          out_specs=[row, row, row, row],
                          out_shape=[_sds((T, D), F32), _sds((T, D), F32), _sds((T, D), F32), _sds((T, D), BF16)],
                          compiler_params=_cp(("parallel",)))(m, x, w, bias, scale, g, b)


def _pool_bwd_a(name, dr, ypre, scale, w):
    T, D = dr.shape
    ng = len(POOL_WINDOWS)
    dg = D // ng
    tr = _tile(T, 256, 16)

    def kern(dr_ref, yp_ref, sc_ref, w_ref, dm_ref, dyp_ref, dsc_ref, dbi_ref):
        drv = dr_ref[...]
        dyp = drv * sc_ref[...]
        dyp_ref[...] = _bf(dyp)
        for gi in range(ng):
            sl = slice(gi * dg, (gi + 1) * dg)
            dm_ref[:, sl] = _dot(dyp_ref[:, sl], w_ref[gi], NT)
        first = pl.program_id(0) == 0
        _acc_out(dsc_ref, _part8(drv * yp_ref[...]), first)
        _acc_out(dbi_ref, _part8(dyp), first)

    row = pl.BlockSpec((tr, D), lambda i: (i, 0))
    vec = pl.BlockSpec((1, D), lambda i: (0, 0))
    acc = pl.BlockSpec((8, D), lambda i: (0, 0))
    wsp = pl.BlockSpec((ng, dg, dg), lambda i: (0, 0, 0))
    return pl.pallas_call(kern, name=name, grid=(T // tr,), in_specs=[row, row, vec, wsp],
                          out_specs=[row, row, acc, acc],
                          out_shape=[_sds((T, D), F32), _sds((T, D), BF16), _sds((8, D), F32), _sds((8, D), F32)],
                          compiler_params=_cp(("arbitrary",)))(dr, ypre, scale, w)


def _pool_bwd_win(name, dm, dr):
    T, D = dm.shape
    per = (D // len(POOL_WINDOWS)) // LANES

    def kern(dm_ref, dr_ref, du_ref):
        j = pl.program_id(0)
        for gi, win in enumerate(POOL_WINDOWS):
            @pl.when(j // per == gi)
            def _():
                dmv = dm_ref[...]
                du_ref[...] = ALPHA * dr_ref[...] + _winsum(dmv / _wincount(dmv.shape, win), win, True) - dmv

    col = pl.BlockSpec((T, LANES), lambda j: (0, j))
    return pl.pallas_call(kern, name=name, grid=(D // LANES,), in_specs=[col, col], out_specs=col,
                          out_shape=_sds((T, D), F32), compiler_params=_cp(("parallel",)))(dm, dr)


def _conv_pre(x, w_ref, b_ref):
    acc = b_ref[...] + w_ref[2:3, :] * x
    for k in (0, 1, 3, 4):
        acc = acc + w_ref[k:k + 1, :] * _shift(x, k - 2)
    return acc


def _conv_fwd(name, zx, blk0, wblk0, w, b, G, cw):
    T = zx.shape[0]
    per = cw // LANES

    def kern(x_ref, w_ref, b_ref, o_ref):
        pre = _conv_pre(x_ref[...], w_ref, b_ref)
        o_ref[0] = pre * _sigmoid(pre)

    return pl.pallas_call(
        kern, name=name, grid=(G * per,),
        in_specs=[pl.BlockSpec((T, LANES), lambda j: (0, blk0 + j)),
                  pl.BlockSpec((CONV_WIDTH, LANES), lambda j: (0, wblk0 + j)),
                  pl.BlockSpec((1, LANES), lambda j: (0, wblk0 + j))],
        out_specs=pl.BlockSpec((1, T, LANES), lambda j: (j // per, 0, j % per)),
        out_shape=_sds((G, T, cw), F32), compiler_params=_cp(("parallel",)))(zx, w, b)


def _conv_bwd(name, zx, blk0, wblk0, w, b, adds, dyd=None):
    T = zx.shape[0]
    G, _, cw = adds[0].shape
    per = cw // LANES
    na = len(adds)

    def kern(*refs):
        x_ref, w_ref, b_ref = refs[:3]
        add_refs = refs[3:3 + na]
        rest = refs[3 + na:]
        if dyd is not None:
            dy_ref, dv_ref = rest[:2]
            rest = rest[2:]
        dx_ref, db_ref = rest[0], rest[1]
        dw_refs = rest[2:]
        x = x_ref[...]
        pre = _conv_pre(x, w_ref, b_ref)
        sg = _sigmoid(pre)
        dact = add_refs[0][0]
        for r in add_refs[1:]:
            dact = dact + r[0]
        if dyd is not None:
            dact = dact + dy_ref[0] * dv_ref[...]
        dpre = dact * (sg * (1.0 + pre * (1.0 - sg)))
        row0 = lax.broadcasted_iota(jnp.int32, (8, LANES), 0) == 0

        def put(ref, v):
            ref[...] = jnp.where(row0, jnp.sum(v, axis=0, keepdims=True), 0.0)

        put(db_ref, dpre)
        dx = w_ref[2:3, :] * dpre
        put(dw_refs[2], dpre * x)
        for k in (0, 1, 3, 4):
            put(dw_refs[k], dpre * _shift(x, k - 2))
            dx = dx + w_ref[k:k + 1, :] * _shift(dpre, 2 - k)
        dx_ref[...] = _bf(dx)

    gsp = pl.BlockSpec((1, T, LANES), lambda j: (j // per, 0, j % per))
    in_specs = [pl.BlockSpec((T, LANES), lambda j: (0, blk0 + j)),
                pl.BlockSpec((CONV_WIDTH, LANES), lambda j: (0, wblk0 + j)),
                pl.BlockSpec((1, LANES), lambda j: (0, wblk0 + j))] + [gsp] * na
    args = [zx, w, b, *adds]
    if dyd is not None:
        in_specs += [gsp, pl.BlockSpec((1, LANES), lambda j: (0, j))]
        args += list(dyd)
    n = G * cw
    p8 = pl.BlockSpec((8, LANES), lambda j: (0, j))
    res = pl.pallas_call(
        kern, name=name, grid=(G * per,), in_specs=in_specs,
        out_specs=[pl.BlockSpec((T, LANES), lambda j: (0, j))] + [p8] * (1 + CONV_WIDTH),
        out_shape=[_sds((T, n), BF16)] + [_sds((8, n), F32)] * (1 + CONV_WIDTH),
        compiler_params=_cp(("parallel",)))(*args)
    return res[0], res[1], res[2:]


def _split3(x):
    x1 = _bf(x)
    r1 = x - x1.astype(F32)
    x2 = _bf(r1)
    x3 = _bf(r1 - x2.astype(F32))
    return x1, x2, x3


def _tri_dot(tri, x, dn=NN):
    a, b, c = _split3(x)
    return _dot(tri, a, dn) + _dot(tri, b, dn) + _dot(tri, c, dn)


def _tri(lower):
    i = lax.broadcasted_iota(jnp.int32, (CHUNK, CHUNK), 0)
    j = lax.broadcasted_iota(jnp.int32, (CHUNK, CHUNK), 1)
    return jnp.where((i >= j) if lower else (i <= j), 1.0, 0.0).astype(BF16)


def _softplus(x):
    return jnp.maximum(x, 0.0) + jnp.log(1.0 + jnp.exp(-jnp.abs(x)))


def _dt_prep(name, zx, dtblk, dt_bias, a_log):
    T = zx.shape[0]
    H = LANES // 2

    def kern(x_ref, bias_ref, al_ref, dt_ref, acs_ref, acst_ref):
        dt = _softplus(x_ref[...] + bias_ref[...])
        dta = dt * (-jnp.exp(al_ref[...]))
        lane = lax.broadcasted_iota(jnp.int32, dta.shape, 1)
        dt_ref[...] = dt
        acs = jnp.where(lane < H, _tri_dot(_tri(True), dta), _tri_dot(_tri(False), dta))
        acs_ref[...] = acs
        acst_ref[...] = acs.T

    blk = pl.BlockSpec((CHUNK, LANES), lambda c: (c, 0))
    vec = pl.BlockSpec((1, LANES), lambda c: (0, 0))
    return pl.pallas_call(kern, name=name, grid=(T // CHUNK,),
                          in_specs=[pl.BlockSpec((CHUNK, LANES), lambda c: (c, dtblk)), vec, vec],
                          out_specs=[blk, blk, pl.BlockSpec((LANES, CHUNK), lambda c: (0, c))],
                          out_shape=[_sds((T, LANES), F32), _sds((T, LANES), F32), _sds((LANES, T), F32)],
                          compiler_params=_cp(("parallel",)))(zx, dt_bias, a_log)


def _dt_bwd(name, ddta, ddtx, zx, dtblk, dt_bias, a_log):
    T = zx.shape[0]

    def kern(da_ref, dx_ref, x_ref, bias_ref, al_ref, draw_ref, dbias_ref, dal_ref):
        pre = x_ref[...] + bias_ref[...]
        dt = _softplus(pre)
        A = -jnp.exp(al_ref[...])
        dav = da_ref[...]
        draw = (dav * A + dx_ref[...]) * _sigmoid(pre)
        draw_ref[...] = _bf(draw)
        first = pl.program_id(0) == 0
        _acc_out(dbias_ref, _part8(draw), first)
        _acc_out(dal_ref, _part8(dav * dt) * A, first)

    blk = pl.BlockSpec((CHUNK, LANES), lambda c: (c, 0))
    vec = pl.BlockSpec((1, LANES), lambda c: (0, 0))
    acc = pl.BlockSpec((8, LANES), lambda c: (0, 0))
    return pl.pallas_call(kern, name=name, grid=(T // CHUNK,),
                          in_specs=[blk, blk, pl.BlockSpec((CHUNK, LANES), lambda c: (c, dtblk)), vec, vec],
                          out_specs=[blk, acc, acc],
                          out_shape=[_sds((T, LANES), BF16), _sds((8, LANES), F32), _sds((8, LANES), F32)],
                          compiler_params=_cp(("arbitrary",)))(ddta, ddtx, zx, dt_bias, a_log)


def _ssd_specs(T, GW, hpg, cmap):
    nc = T // CHUNK
    xs = pl.BlockSpec((1, CHUNK, GW), lambda g, c: (g, cmap(c), 0))
    bc = pl.BlockSpec((1, CHUNK, D_STATE), lambda g, c: (g, cmap(c), 0))
    nat = pl.BlockSpec((CHUNK, LANES), lambda g, c: (cmap(c), 0))
    natT = pl.BlockSpec((LANES, CHUNK), lambda g, c: (0, cmap(c)))
    st = pl.BlockSpec((1, 1, D_STATE, GW), lambda g, c: (g, cmap(c), 0, 0))
    ocol = pl.BlockSpec((1, CHUNK, hpg), lambda g, c: (g, cmap(c), 0))
    return nc, xs, bc, nat, natT, st, ocol


def _dot3(x, sel, dn=NN):
    a, b, c = _split3(x)
    return _dot(a, sel, dn) + _dot(b, sel, dn) + _dot(c, sel, dn)


def _head_select(base, GW):
    k = lax.broadcasted_iota(jnp.int32, (LANES, GW), 0)
    j = lax.broadcasted_iota(jnp.int32, (LANES, GW), 1)
    lo = (k - base) * HEAD_DIM
    return jnp.where((j >= lo) & (j < lo + HEAD_DIM), 1.0, 0.0).astype(BF16)


def _head_collect(GW):
    j = lax.broadcasted_iota(jnp.int32, (GW, LANES), 0)
    k = lax.broadcasted_iota(jnp.int32, (GW, LANES), 1)
    return jnp.where((j >= k * HEAD_DIM) & (j < (k + 1) * HEAD_DIM), 1.0, 0.0).astype(BF16)


def _chunk_mask(rev):
    li = lax.broadcasted_iota(jnp.int32, (CHUNK, CHUNK), 0)
    si = lax.broadcasted_iota(jnp.int32, (CHUNK, CHUNK), 1)
    return (li <= si) if rev else (li >= si)


def _ssd_fwd(name, xs, Bg, Cg, dt, acs, acsT, d):
    G, T, GW = xs.shape
    hpg = GW // HEAD_DIM
    rev = d == 1
    nc0 = T // CHUNK
    cmap = (lambda c: nc0 - 1 - c) if rev else (lambda c: c)
    nc, xs_s, bc_s, nat_s, natT_s, st_s, _ = _ssd_specs(T, GW, hpg, cmap)
    last = 0 if rev else CHUNK - 1

    def kern(xs_ref, b_ref, c_ref, dt_ref, ac_ref, art_ref, y_ref, st_ref, state):
        @pl.when(pl.program_id(1) == 0)
        def _():
            state[...] = jnp.zeros_like(state)

        base = d * (G * hpg) + pl.program_id(0) * hpg
        Hp = state[...]
        st_ref[0, 0] = Hp
        Bm = _bf(b_ref[0])
        Cm = _bf(c_ref[0])
        S = _dot(Cm, Bm, NT)
        mask = _chunk_mask(rev)
        sel = _head_select(base, GW)
        dt_e = _dot3(dt_ref[...], sel)
        a_e = _dot3(ac_ref[...], sel)
        xdt = xs_ref[0] * dt_e
        a_end = a_e[last:last + 1, :]
        yo = _dot(Cm, _bf(Hp)) * jnp.exp(a_e)
        for r in range(hpg):
            hs = slice(r * HEAD_DIM, (r + 1) * HEAD_DIM)
            a_col = a_e[:, r * HEAD_DIM:r * HEAD_DIM + 1]
            lam = jnp.exp(jnp.where(mask, a_col - art_ref[pl.ds(base + r, 1), :], NEG))
            y_ref[0, :, hs] = _dot(_bf(S * lam), _bf(xdt[:, hs])) + yo[:, hs]
        Hn = _dot(Bm, _bf(xdt * jnp.exp(a_end - a_e)), TN)
        state[...] = jnp.exp(a_end) * Hp + Hn

    return pl.pallas_call(
        kern, name=name, grid=(G, nc), in_specs=[xs_s, bc_s, bc_s, nat_s, nat_s, natT_s], out_specs=[xs_s, st_s],
        out_shape=[_sds((G, T, GW), F32), _sds((G, nc, D_STATE, GW), F32)],
        scratch_shapes=[pltpu.VMEM((D_STATE, GW), F32)],
        compiler_params=_cp(("parallel", "arbitrary")))(xs, Bg, Cg, dt, acs, acsT)


def _ssd_bwd(name, dy, xs, Bg, Cg, dt, acs, acsT, states, d):
    G, T, GW = xs.shape
    hpg = GW // HEAD_DIM
    rev = d == 1
    nc0 = T // CHUNK
    cmap = (lambda c: c) if rev else (lambda c: nc0 - 1 - c)
    nc, xs_s, bc_s, nat_s, natT_s, st_s, ocol_s = _ssd_specs(T, GW, hpg, cmap)
    last = 0 if rev else CHUNK - 1

    def kern(dy_ref, xs_ref, b_ref, c_ref, dt_ref, ac_ref, art_ref, st_ref,
             dxs_ref, db_ref, dc_ref, dda_ref, ddx_ref, dstate, dxq):
        @pl.when(pl.program_id(1) == 0)
        def _():
            dstate[...] = jnp.zeros_like(dstate)

        base = d * (G * hpg) + pl.program_id(0) * hpg
        Bm = _bf(b_ref[0])
        Cm = _bf(c_ref[0])
        S = _dot(Cm, Bm, NT)
        mask = _chunk_mask(rev)
        sel = _head_select(base, GW)
        col = _head_collect(GW)
        dt_e = _dot3(dt_ref[...], sel)
        a_e = _dot3(ac_ref[...], sel)
        x = xs_ref[0]
        dyv = dy_ref[0]
        xdt = x * dt_e
        a_end = a_e[last:last + 1, :]
        e_end = jnp.exp(a_end)
        dte = jnp.exp(a_end - a_e)
        Hp = st_ref[0, 0]
        dHn = dstate[...]
        Hpb, dHnb = _bf(Hp), _bf(dHn)
        BdH = _dot(Bm, dHnb)
        CHp = _dot(Cm, Hpb)
        Edy = jnp.exp(a_e) * dyv
        wv = dte * xdt
        dCa = _dot(_bf(Edy), Hpb, NT)
        dBa = _dot(_bf(wv), dHnb, NT)
        dstate[...] = e_end * dHn + _dot(Cm, _bf(Edy), TN)
        lane = lax.broadcasted_iota(jnp.int32, (CHUNK, LANES), 1)
        rowi = lax.broadcasted_iota(jnp.int32, (CHUNK, LANES), 0)
        dS = jnp.zeros((CHUNK, CHUNK), F32)
        dq = jnp.zeros((CHUNK, LANES), F32)
        for r in range(hpg):
            hs = slice(r * HEAD_DIM, (r + 1) * HEAD_DIM)
            a_col = a_e[:, r * HEAD_DIM:r * HEAD_DIM + 1]
            lam = jnp.exp(jnp.where(mask, a_col - art_ref[pl.ds(base + r, 1), :], NEG))
            Mf = S * lam
            dyr = _bf(dyv[:, hs])
            dxq[:, hs] = _dot(_bf(Mf), dyr, TN)
            dM = _dot(dyr, _bf(xdt[:, hs]), NT)
            dS = dS + dM * lam
            Gb = _bf(dM * Mf)
            oh = jnp.where(lane == r, 1.0, 0.0).astype(BF16)
            dq = dq + _dot(Gb, oh) - _dot(Gb, oh, TN)
        dxdt = dxq[...] + dte * BdH
        dxs_ref[0] = dxdt * dt_e
        ts = _dot(_bf(wv * BdH), col)
        ddx_ref[0] = _dot(_bf(dxdt * x), col)[:, :hpg]
        hh = _dot3(jnp.broadcast_to(e_end * jnp.sum(dHn * Hp, axis=0, keepdims=True), (8, GW)), col)[0:1, :]
        tot = jnp.sum(ts, axis=0, keepdims=True) + hh
        da = dq + _dot(_bf(Edy * CHp), col) - ts + jnp.where(rowi == last, tot, 0.0)
        dda_ref[0] = _tri_dot(_tri(rev), da)[:, :hpg]
        dSb = _bf(dS)
        dc_ref[0] = dCa + _dot(dSb, Bm)
        db_ref[0] = dBa + _dot(dSb, Cm, TN)

    return pl.pallas_call(
        kern, name=name, grid=(G, nc), in_specs=[xs_s, xs_s, bc_s, bc_s, nat_s, nat_s, natT_s, st_s],
        out_specs=[xs_s, bc_s, bc_s, ocol_s, ocol_s],
        out_shape=[_sds((G, T, GW), F32), _sds((G, T, D_STATE), F32), _sds((G, T, D_STATE), F32),
                   _sds((G, T, hpg), F32), _sds((G, T, hpg), F32)],
        scratch_shapes=[pltpu.VMEM((D_STATE, GW), F32), pltpu.VMEM((CHUNK, GW), F32)],
        compiler_params=_cp(("parallel", "arbitrary")))(dy, xs, Bg, Cg, dt, acs, acsT, states)


def _gate_core(yf, yb, xs, z, dv):
    y = yf + yb + xs * dv
    sg = _sigmoid(z)
    sz = z * sg
    gy = y * sz
    rstd = lax.rsqrt(jnp.mean(gy * gy, axis=-1, keepdims=True) + RMS_EPS)
    return y, sg, sz, gy, rstd


def _gate_fwd(name, yf, yb, xs, zx, dvec, nw):
    G, T, GW = xs.shape
    tr = _tile(T, 512, 16)

    def kern(yf_ref, yb_ref, xs_ref, z_ref, dv_ref, nw_ref, o_ref):
        _, _, _, gy, rstd = _gate_core(yf_ref[0], yb_ref[0], xs_ref[0], z_ref[...], dv_ref[...])
        o_ref[...] = _bf(gy * rstd * nw_ref[...])

    gsp = pl.BlockSpec((1, tr, GW), lambda g, t: (g, t, 0))
    zsp = pl.BlockSpec((tr, GW), lambda g, t: (t, g))
    vsp = pl.BlockSpec((1, GW), lambda g, t: (0, g))
    return pl.pallas_call(kern, name=name, grid=(G, T // tr), in_specs=[gsp, gsp, gsp, zsp, vsp, vsp], out_specs=zsp,
                          out_shape=_sds((T, G * GW), BF16),
                          compiler_params=_cp(("parallel", "parallel")))(yf, yb, xs, zx, dvec, nw)


def _gate_bwd(name, dgyn, yf, yb, xs, zx, dvec, nw):
    G, T, GW = xs.shape
    tr = _tile(T, 512, 16)

    def kern(dg_ref, yf_ref, yb_ref, xs_ref, z_ref, dv_ref, nw_ref, dy_ref, dz_ref, dnw_ref, ddl_ref):
        xsv = xs_ref[0]
        zv = z_ref[...]
        y, sg, sz, gy, rstd = _gate_core(yf_ref[0], yb_ref[0], xsv, zv, dv_ref[...])
        n = gy * rstd
        dgv = dg_ref[...]
        dn = dgv * nw_ref[...]
        dgy = rstd * (dn - n * jnp.mean(dn * n, axis=-1, keepdims=True))
        dyv = dgy * sz
        dy_ref[0] = dyv
        dz_ref[...] = _bf(dgy * y * (sg * (1.0 + zv * (1.0 - sg))))
        first = pl.program_id(1) == 0
        _acc_out(dnw_ref, _part8(dgv * n), first)
        _acc_out(ddl_ref, _part8(dyv * xsv), first)

    gsp = pl.BlockSpec((1, tr, GW), lambda g, t: (g, t, 0))
    zsp = pl.BlockSpec((tr, GW), lambda g, t: (t, g))
    vsp = pl.BlockSpec((1, GW), lambda g, t: (0, g))
    asp = pl.BlockSpec((8, GW), lambda g, t: (0, g))
    return pl.pallas_call(kern, name=name, grid=(G, T // tr), in_specs=[zsp, gsp, gsp, gsp, zsp, vsp, vsp],
                          out_specs=[gsp, zsp, asp, asp],
                          out_shape=[_sds((G, T, GW), F32), _sds((T, G * GW), BF16), _sds((8, G * GW), F32),
                                     _sds((8, G * GW), F32)],
                          compiler_params=_cp(("parallel", "arbitrary")))(dgyn, yf, yb, xs, zx, dvec, nw)


def _head_sum(name, v):
    n = v.shape[1]
    H = n // HEAD_DIM

    def kern(v_ref, o_ref):
        i = lax.broadcasted_iota(jnp.int32, (n, H), 0)
        j = lax.broadcasted_iota(jnp.int32, (n, H), 1)
        sel = jnp.where((i >= j * HEAD_DIM) & (i < (j + 1) * HEAD_DIM), 1.0, 0.0).astype(BF16)
        a, b, c = _split3(v_ref[...])
        o_ref[...] = _dot(a, sel) + _dot(b, sel) + _dot(c, sel)

    return pl.pallas_call(kern, name=name, out_shape=_sds((8, H), F32))(v)


def _relu2_epi(acc):
    return acc, jnp.square(jnp.maximum(acc, 0.0))


def _dh_epi(acc, h):
    return (acc * (2.0 * jnp.maximum(h, 0.0)),)


def _resid_epi(acc, e):
    return (acc + ALPHA * e,)


def _ssd_fwd_layer(tag, x, xb, W, j, mid=None):
    T, D = x.shape
    DI = 2 * D
    DBC = N_GROUPS * D_STATE
    win = W["win"][j]
    NZ = win.shape[1]
    zx = _mm_act(tag + "_inproj", xb, win, None, T, NZ, D, [F32], tn_pref=1152)[0]
    zb = DI // LANES
    cw, cb = W["conv_w"][j], W["conv_b"][j]
    xs = _conv_fwd(tag + "_convx", zx, zb, 0, cw, cb, N_GROUPS, DI // N_GROUPS)
    Bg = _conv_fwd(tag + "_convb", zx, zb + DI // LANES, DI // LANES, cw, cb, N_GROUPS, D_STATE)
    Cg = _conv_fwd(tag + "_convc", zx, zb + (DI + DBC) // LANES, (DI + DBC) // LANES, cw, cb, N_GROUPS, D_STATE)
    dtblk = (2 * DI + 2 * DBC) // LANES
    dt, acs, acsT = _dt_prep(tag + "_dtprep", zx, dtblk, W["dt_bias"][j], W["a_log"][j])
    yf, stf = _ssd_fwd(tag + "_scanf", xs, Bg, Cg, dt, acs, acsT, 0)
    yb, stb = _ssd_fwd(tag + "_scanb", xs, Bg, Cg, dt, acs, acsT, 1)
    gyn = _gate_fwd(tag + "_gate", yf, yb, xs, zx, W["dvec"][j], W["norm_w"][j])
    tok = mid(gyn) if mid is not None else 0.0
    mix = _mm_act(tag + "_outproj", gyn, W["wout"][j], _RowShard(DI // 4, D), T, D, DI, [F32])[0]
    saved = dict(zx=zx, xs=xs, Bg=Bg, Cg=Cg, dt=dt, acs=acs, acsT=acsT, stf=stf, stb=stb,
                 yf=yf, yb=yb, gyn=gyn, dtblk=dtblk)
    return mix, saved, tok


def _ssd_bwd_layer(tag, xb, dr1, dr1b, W, j, s, small):
    T, D = dr1.shape
    DI = 2 * D
    DBC = N_GROUPS * D_STATE
    win = W["win"][j]
    NZ = win.shape[1]
    zx = s["zx"]
    rs = _RowShard(DI // 4, D)
    dgyn = _mm_act(tag + "_dgyn", dr1b, W["wout"][j], rs, T, DI, D, [F32], nt=True)[0]
    p_out = _mm_wgrad(tag + "_dwout", s["gyn"], dr1b, DI, D, T, _sds((4, DI // 4, D), BF16), rs.out)
    dvec, nw = W["dvec"][j], W["norm_w"][j]
    dy, dzb, dnw, ddl = _gate_bwd(tag + "_dgate", dgyn, s["yf"], s["yb"], s["xs"], zx, dvec, nw)
    res = []
    for d, st in ((0, s["stf"]), (1, s["stb"])):
        res.append(_ssd_bwd(tag + "_dscan%d" % d, dy, s["xs"], s["Bg"], s["Cg"], s["dt"], s["acs"], s["acsT"], st, d))
    nat = lambda k: jnp.concatenate([res[d][k].transpose(1, 0, 2).reshape(T, -1) for d in (0, 1)], axis=1)
    drawb, dbias, dal = _dt_bwd(tag + "_ddt", nat(3), nat(4), zx, s["dtblk"], W["dt_bias"][j], W["a_log"][j])
    zb = DI // LANES
    cw, cb = W["conv_w"][j], W["conv_b"][j]
    dxx, dbx, dwx = _conv_bwd(tag + "_dconvx", zx, zb, 0, cw, cb, [res[0][0], res[1][0]], (dy, dvec))
    dxb, dbb, dwb = _conv_bwd(tag + "_dconvb", zx, zb + DI // LANES, DI // LANES, cw, cb, [res[0][1], res[1][1]])
    dxc, dbc, dwc = _conv_bwd(tag + "_dconvc", zx, zb + (DI + DBC) // LANES, (DI + DBC) // LANES, cw, cb,
                              [res[0][2], res[1][2]])
    dzx = jnp.concatenate([dzb, dxx, dxb, dxc, drawb], axis=1)
    dwin = _mm_wgrad(tag + "_dwin", xb, dzx, D, NZ, T, _sds((D, NZ), BF16),
                     lambda tm, tn: pl.BlockSpec((tm, tn), lambda i, jj, k: (i, jj)), tn_pref=1152)
    p_in = dwin.reshape(D, 4, NZ // 4).transpose(1, 0, 2)
    dx = _mm_act(tag + "_dxin", dzx, win, None, T, D, NZ, [F32], epi=_resid_epi, extras=(dr1,), nt=True,
                 tk_pref=1152)[0]
    small["conv_w"].append(jnp.concatenate([p[k] for k in range(CONV_WIDTH) for p in (dwx, dwb, dwc)], axis=1))
    small["conv_b"].append(jnp.concatenate([dbx, dbb, dbc], axis=1))
    small["dt_bias"].append(dbias)
    small["a_log"].append(dal)
    small["d"].append(ddl)
    small["norm_w"].append(dnw)
    return dx, p_in, p_out


def _pool_bwd_layer(tag, dr1, W, j, s, small):
    T, D = dr1.shape
    ng = len(POOL_WINDOWS)
    dg = D // ng
    dm, dypb, dsc, dbi = _pool_bwd_a(tag + "_dpool", dr1, s["ypre"], W["pool_scale"][j], W["wp"][j])
    tk = _tile(T, 1024)
    dwp = _mm(tag + "_dwp", s["m"], dypb, pl.BlockSpec((tk, dg), lambda i, jj, k: (k, i)),
              pl.BlockSpec((tk, dg), lambda i, jj, k: (k, i)), TN, (ng, 1, T // tk), dg, dg,
              [_sds((ng, dg, dg), BF16)], [pl.BlockSpec((None, dg, dg), lambda i, jj, k: (i, 0, 0))])[0]
    p_pool = dwp.reshape(ng, 4, dg // 4, dg).transpose(1, 0, 2, 3).reshape(4, dg, dg)
    dx = _pool_bwd_win(tag + "_dwin", dm, dr1)
    small["pool_b"].append(dbi)
    small["pool_scale"].append(dsc)
    return dx, p_pool


SMALL_NAMES = ("conv_w", "conv_b", "dt_bias", "a_log", "d", "norm_w", "pool_b", "pool_scale",
               "ln_mix_g", "ln_mix_b", "ln_ffn_g", "ln_ffn_b")


def _local_step(x, tgt, W, pre_fwd=None, post_bwd=None, mid_fwd=None, mid_bwd=None):
    T, D = x.shape
    DFF = 4 * D
    cs, rs = _ColShard(D, DFF // 4), _RowShard(DFF // 4, D)
    saved = []
    xb = _bf(x)
    for i in range(DEPTH):
        j = i // 2
        tag = "L%d" % i
        b_mix = W["ln_mix_b"][i]
        if pre_fwd is not None:
            tok = pre_fwd(i, x)
            if i % 2 == 0:
                xb = xb + tok.astype(BF16)
            else:
                x = x + tok
        s = dict(x=x, xb=xb)
        if i % 2 == 0:
            mid = (lambda act, i=i: mid_fwd(i, act)) if mid_fwd is not None else None
            mix, ss, mtok = _ssd_fwd_layer(tag, x, xb, W, j, mid)
            s.update(ss)
            r1, x1, x1b = _resln(tag + "_lnmix", x, mix, W["ln_mix_g"][i], b_mix + mtok)
        else:
            m = _pool_m(tag + "_poolm", x)
            ypre, r1, x1, x1b = _pool_fwd(tag + "_pool", m, x, W["wp"][j], W["pool_b"][j], W["pool_scale"][j],
                                          W["ln_mix_g"][i], b_mix)
            s.update(m=m, ypre=ypre)
        h, a = _mm_act(tag + "_mlp1", x1b, W["w1"][i], cs, T, DFF, D, [F32, BF16], epi=_relu2_epi)
        mlp = _mm_act(tag + "_mlp2", a, W["w2"][i], rs, T, D, DFF, [F32])[0]
        r2, x2, x2b = _resln(tag + "_lnffn", x1, mlp, W["ln_ffn_g"][i], W["ln_ffn_b"][i])
        s.update(r1=r1, x1=x1, x1b=x1b, h=h, a=a, r2=r2)
        saved.append(s)
        x, xb = x2, x2b

    dx, loss = _loss_head("loss", x, tgt)
    small = {n: [] for n in SMALL_NAMES}
    P = dict(win=[], wout=[], w1=[], w2=[], wp=[])
    tok = 0.0
    for i in reversed(range(DEPTH)):
        j = i // 2
        tag = "L%d" % i
        s = saved[i]
        dr2, dr2b, dg2, db2 = _lnbwd(tag + "_dlnffn", dx, s["r2"], W["ln_ffn_g"][i] + tok)
        dh = _mm_act(tag + "_dh", dr2b, W["w2"][i], rs, T, DFF, D, [BF16], epi=_dh_epi, extras=(s["h"],), nt=True)[0]
        Pi = dict(w2=_mm_wgrad(tag + "_dw2", s["a"], dr2b, DFF, D, T, _sds((4, DFF // 4, D), BF16), rs.out),
                  w1=_mm_wgrad(tag + "_dw1", s["x1b"], dh, D, DFF, T, _sds((4, D, DFF // 4), BF16), cs.out))
        dx1 = _mm_act(tag + "_dx1", dh, W["w1"][i], cs, T, D, DFF, [F32], epi=_resid_epi, extras=(dr2,), nt=True)[0]
        mtok = mid_bwd(i, Pi) if mid_bwd is not None else 0.0
        dr1, dr1b, dg1, db1 = _lnbwd(tag + "_dlnmix", dx1, s["r1"], W["ln_mix_g"][i] + mtok)
        if i % 2 == 0:
            dx, Pi["win"], Pi["wout"] = _ssd_bwd_layer(tag, s["xb"], dr1, dr1b, W, j, s, small)
        else:
            dx, Pi["wp"] = _pool_bwd_layer(tag, dr1, W, j, s, small)
        for k, v in Pi.items():
            P[k].append(v)
        small["ln_ffn_g"].append(dg2)
        small["ln_ffn_b"].append(db2)
        small["ln_mix_g"].append(dg1)
        small["ln_mix_b"].append(db1)
        if post_bwd is not None:
            tok = post_bwd(i, dx, Pi)
    P = {k: v[::-1] for k, v in P.items()}
    small = {k: jnp.concatenate(v[::-1], axis=1) for k, v in small.items()}
    small["d"] = _head_sum("dD", small["d"])
    return loss, dx, P, small


ANY = pl.BlockSpec(memory_space=pl.ANY)


def _pos():
    return lax.axis_index("x"), lax.axis_index("y"), lax.axis_index("c")


def _other_chips(x, y):
    return [(1 - x, y), (x, 1 - y), (1 - x, 1 - y)]


def _rcopy(src, dst, ssem, rsem, dev):
    return pltpu.make_async_remote_copy(src_ref=src, dst_ref=dst, send_sem=ssem, recv_sem=rsem,
                                        device_id=dev, device_id_type=MESH)


def _gather(name, slabs, split):
    n = len(slabs)

    def body(*refs):
        src, out = refs[:n], refs[n:2 * n]
        ssem, rsem, fssem, frsem, lsem = refs[2 * n:]
        x, y, c = _pos()
        chip = 2 * x + y
        chips = _other_chips(x, y)
        sib = (x, y, 1 - c)

        def mine(t, half):
            if split[t]:
                h = slabs[t].shape[0] // 2
                return src[t].at[pl.ds(half * h, h)]
            return src[t]

        def region(t, ch, half):
            if split[t]:
                h = slabs[t].shape[0] // 2
                return out[t].at[ch, pl.ds(half * h, h)]
            return out[t].at[ch]

        local = [pltpu.make_async_copy(src[t], out[t].at[chip], lsem.at[t]) for t in range(n)]
        for cp in local:
            cp.start()
        sends = []
        for t in range(n):
            for j, (px, py) in enumerate(chips):
                cp = _rcopy(mine(t, c), region(t, chip, c), ssem.at[t, j], rsem.at[t, j], (px, py, c))
                cp.start()
                sends.append(cp)
        for t in range(n):
            for j, (px, py) in enumerate(chips):
                pch = 2 * px + py
                _rcopy(mine(t, c), region(t, pch, c), ssem.at[t, j], rsem.at[t, j], (px, py, c)).wait_recv()
                if split[t]:
                    cp = _rcopy(region(t, pch, c), region(t, pch, c), fssem.at[t, j], frsem.at[t, j], sib)
                    cp.start()
                    sends.append(cp)
        for t in range(n):
            if split[t]:
                for j, (px, py) in enumerate(chips):
                    pch = 2 * px + py
                    _rcopy(region(t, pch, 1 - c), region(t, pch, 1 - c), fssem.at[t, j], frsem.at[t, j],
                           sib).wait_recv()
        for cp in sends:
            cp.wait_send()
        for cp in local:
            cp.wait()

    sem = pltpu.SemaphoreType.DMA
    return pl.pallas_call(
        body, name=name, in_specs=[ANY] * n, out_specs=[ANY] * n,
        out_shape=[_sds((4,) + s.shape, s.dtype) for s in slabs],
        scratch_shapes=[sem((n, 3)), sem((n, 3)), sem((n, 3)), sem((n, 3)), sem((n,))])(*slabs)


def _rs1(name, Ps):
    n = len(Ps)

    def body(*refs):
        src, out = refs[:n], refs[n:2 * n]
        ssem, rsem = refs[2 * n:]
        x, y, c = _pos()
        cps = []
        for t in range(n):
            h = Ps[t].shape[1] // 2
            cp = _rcopy(src[t].at[pl.ds(0, 4), pl.ds((1 - c) * h, h)], out[t], ssem.at[t], rsem.at[t], (x, y, 1 - c))
            cp.start()
            cps.append(cp)
        for cp in cps:
            cp.wait()

    sem = pltpu.SemaphoreType.DMA
    return pl.pallas_call(
        body, name=name, in_specs=[ANY] * n, out_specs=[ANY] * n,
        out_shape=[_sds((4, p.shape[1] // 2, p.shape[2]), p.dtype) for p in Ps],
        scratch_shapes=[sem((n,)), sem((n,))])(*Ps)


def _rs2(name, Qs):
    n = len(Qs)

    def body(*refs):
        src, r2, qo = refs[:n], refs[n:2 * n], refs[2 * n:3 * n]
        ssem, rsem, fssem, frsem, qssem, qrsem, lsem = refs[3 * n:]
        x, y, c = _pos()
        chip = 2 * x + y
        chips = _other_chips(x, y)
        sib = (x, y, 1 - c)
        sends, local = [], []
        for t in range(n):
            h = Qs[t].shape[1]
            lc = pltpu.make_async_copy(src[t].at[chip], qo[t].at[pl.ds(c * h, h)], lsem.at[t])
            lc.start()
            local.append(lc)
            cp = _rcopy(src[t].at[chip], qo[t].at[pl.ds(c * h, h)], qssem.at[t], qrsem.at[t], sib)
            cp.start()
            sends.append(cp)
            for j, (px, py) in enumerate(chips):
                cp = _rcopy(src[t].at[2 * px + py], r2[t].at[j, pl.ds(c * h, h)], ssem.at[t, j], rsem.at[t, j],
                            (px, py, c))
                cp.start()
                sends.append(cp)
        for t in range(n):
            h = Qs[t].shape[1]
            for j, (px, py) in enumerate(chips):
                mine = r2[t].at[j, pl.ds(c * h, h)]
                _rcopy(src[t].at[chip], mine, ssem.at[t, j], rsem.at[t, j], (px, py, c)).wait_recv()
                cp = _rcopy(mine, mine, fssem.at[t, j], frsem.at[t, j], sib)
                cp.start()
                sends.append(cp)
        for t in range(n):
            h = Qs[t].shape[1]
            _rcopy(src[t].at[chip], qo[t].at[pl.ds((1 - c) * h, h)], qssem.at[t], qrsem.at[t], sib).wait_recv()
            for j in range(3):
                theirs = r2[t].at[j, pl.ds((1 - c) * h, h)]
                _rcopy(theirs, theirs, fssem.at[t, j], frsem.at[t, j], sib).wait_recv()
        for cp in sends:
            cp.wait_send()
        for lc in local:
            lc.wait()

    sem = pltpu.SemaphoreType.DMA
    return pl.pallas_call(
        body, name=name, in_specs=[ANY] * n, out_specs=[ANY] * (2 * n),
        out_shape=[_sds((3, 2 * q.shape[1], q.shape[2]), q.dtype) for q in Qs]
        + [_sds((2 * q.shape[1], q.shape[2]), q.dtype) for q in Qs],
        scratch_shapes=[sem((n, 3)), sem((n, 3)), sem((n, 3)), sem((n, 3)), sem((n,)), sem((n,)), sem((n,))])(*Qs)


HBM = pl.BlockSpec(memory_space=pltpu.HBM)
SEMS = pl.BlockSpec(memory_space=pltpu.SEMAPHORE)
EFFECT = pltpu.SideEffectType.DATAFLOW_SIDE_EFFECTING
TOKEN = _sds((8, LANES), F32)


def _in_hbm(a):
    return pltpu.with_memory_space_constraint(a, pltpu.HBM)


def _cast_place(name, w, chip):
    A, B = w.shape
    ta = _row_tile(A, B, 16)

    def kern(s_ref, w_ref, o_ref):
        o_ref[0] = _bf(w_ref[...])

    gs = pltpu.PrefetchScalarGridSpec(
        num_scalar_prefetch=1, grid=(A // ta,), in_specs=[pl.BlockSpec((ta, B), lambda i, s_ref: (i, 0))],
        out_specs=pl.BlockSpec((1, ta, B), lambda i, s_ref: (s_ref[0], i, 0)))
    return pl.pallas_call(kern, name=name, grid_spec=gs, out_shape=_sds((4, A, B), BF16),
                          compiler_params=_cp(("parallel",)))(chip.reshape(1), w)


def _gather_start(name, lands, after):
    n = len(lands)

    def body(*refs):
        land = refs[:n]
        ssem, rsem = refs[n + 1], refs[n + 2]
        token = refs[-1]
        x, y, c = _pos()
        chip = 2 * x + y
        for t in range(n):
            h = lands[t].shape[1] // 2
            mine = land[t].at[chip, pl.ds(c * h, h)]
            for j, (px, py) in enumerate(_other_chips(x, y)):
                _rcopy(mine, mine, ssem.at[3 * t + j], rsem.at[3 * t + j], (px, py, c)).start()
        token[...] = jnp.zeros_like(token)

    sem = pltpu.SemaphoreType.DMA
    res = pl.pallas_call(
        body, name=name, in_specs=[HBM] * n + [ANY],
        out_specs=(SEMS, SEMS, *[HBM] * n, pl.BlockSpec(memory_space=pltpu.VMEM)),
        out_shape=(sem((3 * n,)), sem((3 * n,)), *[pltpu.HBM(l.shape, l.dtype) for l in lands], TOKEN),
        input_output_aliases={t: 2 + t for t in range(n)},
        compiler_params=pltpu.CompilerParams(has_side_effects=EFFECT))(*[_in_hbm(l) for l in lands], after)
    return res[0], res[1], list(res[2:2 + n]), res[-1]


def _gather_wait(name, lands, ssem, rsem, after):
    n = len(lands)

    def body(*refs):
        land = refs[:n]
        ssem_ref, rsem_ref = refs[n], refs[n + 1]
        x, y, c = _pos()
        chip = 2 * x + y
        for t in range(n):
            h = lands[t].shape[1] // 2
            for j, (px, py) in enumerate(_other_chips(x, y)):
                cp = _rcopy(land[t].at[chip, pl.ds(c * h, h)], land[t].at[2 * px + py, pl.ds(c * h, h)],
                            ssem_ref.at[3 * t + j], rsem_ref.at[3 * t + j], (px, py, c))
                cp.wait_send()
                cp.wait_recv()

    return pl.pallas_call(
        body, name=name, in_specs=[HBM] * n + [SEMS, SEMS, ANY], out_specs=[HBM] * n,
        out_shape=[pltpu.HBM(l.shape, l.dtype) for l in lands], input_output_aliases={t: t for t in range(n)},
        compiler_params=pltpu.CompilerParams(has_side_effects=EFFECT))(*lands, ssem, rsem, after)


def _sibling_fill(name, lands):
    n = len(lands)

    def body(*refs):
        src, out = refs[:n], refs[n:2 * n]
        ssem, rsem = refs[2 * n:]
        x, y, c = _pos()
        cps = []
        for t in range(n):
            h = lands[t].shape[1] // 2
            for j, (px, py) in enumerate(_other_chips(x, y)):
                pch = 2 * px + py
                cp = _rcopy(src[t].at[pch, pl.ds(c * h, h)], out[t].at[pch, pl.ds(c * h, h)], ssem.at[t, j],
                            rsem.at[t, j], (x, y, 1 - c))
                cp.start()
                cps.append(cp)
        for cp in cps:
            cp.wait()

    sem = pltpu.SemaphoreType.DMA
    return pl.pallas_call(
        body, name=name, in_specs=[ANY] * n, out_specs=[ANY] * n, out_shape=[_sds(l.shape, l.dtype) for l in lands],
        input_output_aliases={t: t for t in range(n)}, scratch_shapes=[sem((n, 3)), sem((n, 3))])(*lands)


def _rs2_start(name, Qs, after):
    n = len(Qs)
    r2s = [lax.empty((3, 2 * q.shape[1], q.shape[2]), q.dtype) for q in Qs]
    qsibs = [lax.empty(q.shape[1:], q.dtype) for q in Qs]

    def body(*refs):
        src, r2, qsib = refs[:n], refs[n:2 * n], refs[2 * n:3 * n]
        ssem, rsem, qs, qr = refs[3 * n + 1:3 * n + 5]
        token = refs[-1]
        x, y, c = _pos()
        chip = 2 * x + y
        for t in range(n):
            h = Qs[t].shape[1]
            for j, (px, py) in enumerate(_other_chips(x, y)):
                dst = r2[t].at[j, pl.ds(c * h, h)]
                s0 = 6 * t + 2 * j
                _rcopy(src[t].at[2 * px + py], dst, ssem.at[s0], rsem.at[s0], (px, py, c)).start()
                _rcopy(src[t].at[2 * px + py], dst, ssem.at[s0 + 1], rsem.at[s0 + 1], (px, py, 1 - c)).start()
            _rcopy(src[t].at[chip], qsib[t], qs.at[t], qr.at[t], (x, y, 1 - c)).start()
        token[...] = jnp.zeros_like(token)

    sem = pltpu.SemaphoreType.DMA
    bufs = list(Qs) + r2s + qsibs
    res = pl.pallas_call(
        body, name=name, in_specs=[HBM] * (3 * n) + [ANY],
        out_specs=(SEMS, SEMS, SEMS, SEMS, *[HBM] * (3 * n), pl.BlockSpec(memory_space=pltpu.VMEM)),
        out_shape=(sem((6 * n,)), sem((6 * n,)), sem((n,)), sem((n,)),
                   *[pltpu.HBM(b.shape, b.dtype) for b in bufs], TOKEN),
        input_output_aliases={t: 4 + t for t in range(3 * n)},
        compiler_params=pltpu.CompilerParams(has_side_effects=EFFECT))(*[_in_hbm(b) for b in bufs], after)
    return res[:4], list(res[4:4 + 3 * n]), res[-1]


def _rs2_wait(name, sems, bufs, after):
    n = len(bufs) // 3

    def body(*refs):
        src, r2, qsib = refs[:n], refs[n:2 * n], refs[2 * n:3 * n]
        ssem, rsem, qs, qr = refs[3 * n:3 * n + 4]
        x, y, c = _pos()
        chip = 2 * x + y
        for t in range(n):
            h = bufs[t].shape[1]
            for j, (px, py) in enumerate(_other_chips(x, y)):
                for k, pc in ((0, c), (1, 1 - c)):
                    s0 = 6 * t + 2 * j + k
                    cp = _rcopy(src[t].at[chip], r2[t].at[j, pl.ds(pc * h, h)], ssem.at[s0], rsem.at[s0],
                                (px, py, pc))
                    cp.wait_send()
                    cp.wait_recv()
            cp = _rcopy(src[t].at[chip], qsib[t], qs.at[t], qr.at[t], (x, y, 1 - c))
            cp.wait_send()
            cp.wait_recv()

    res = pl.pallas_call(
        body, name=name, in_specs=[HBM] * (3 * n) + [SEMS] * 4 + [ANY], out_specs=[HBM] * (3 * n),
        out_shape=[pltpu.HBM(b.shape, b.dtype) for b in bufs], input_output_aliases={t: t for t in range(3 * n)},
        compiler_params=pltpu.CompilerParams(has_side_effects=EFFECT))(*bufs, *sems, after)
    return list(res[:n]), list(res[n:2 * n]), list(res[2 * n:])


def _allgather_small(name, v):
    def body(v_ref, out_ref, ssem, rsem, lsem):
        x, y, c = _pos()
        me = 4 * x + 2 * y + c
        lc = pltpu.make_async_copy(v_ref, out_ref.at[me], lsem)
        lc.start()
        cps = []
        for k in range(1, 8):
            flip = lambda a, bit: (1 - a) if bit else a
            peer = (flip(x, k & 4), flip(y, k & 2), flip(c, k & 1))
            cp = _rcopy(v_ref, out_ref.at[me], ssem.at[k - 1], rsem.at[k - 1], peer)
            cp.start()
            cps.append(cp)
        for cp in cps:
            cp.wait()
        lc.wait()

    sem = pltpu.SemaphoreType.DMA
    return pl.pallas_call(body, name=name, in_specs=[ANY], out_specs=ANY, out_shape=_sds((8,) + v.shape, v.dtype),
                          scratch_shapes=[sem((7,)), sem((7,)), sem])(v)


def _row_tile(R, C, mult):
    return _tile(R, max(mult, (1 << 19) // C), mult)


def _sum1(name, P, R1, c):
    _, A, B = P.shape
    h = A // 2
    ta = _row_tile(h, B, 16)
    nb = h // ta

    def kern(c_ref, p_ref, r_ref, q_ref):
        q_ref[...] = _bf(p_ref[...].astype(F32) + r_ref[...].astype(F32))

    gs = pltpu.PrefetchScalarGridSpec(
        num_scalar_prefetch=1, grid=(4, nb),
        in_specs=[pl.BlockSpec((1, ta, B), lambda s, i, c_ref: (s, c_ref[0] * nb + i, 0)),
                  pl.BlockSpec((1, ta, B), lambda s, i, c_ref: (s, i, 0))],
        out_specs=pl.BlockSpec((1, ta, B), lambda s, i, c_ref: (s, i, 0)))
    return pl.pallas_call(kern, name=name, grid_spec=gs, out_shape=_sds((4, h, B), BF16),
                          compiler_params=_cp(("parallel", "parallel")))(c.reshape(1), P, R1)


def _adam_math(w, gv, m, v):
    mn = ADAM_B1 * m + (1.0 - ADAM_B1) * gv
    vn = ADAM_B2 * v + (1.0 - ADAM_B2) * jnp.square(gv)
    m_hat = mn / (1.0 - ADAM_B1 ** ADAM_STEP)
    v_hat = vn / (1.0 - ADAM_B2 ** ADAM_STEP)
    return -ADAM_LR * (m_hat / (jnp.sqrt(v_hat) + ADAM_EPS) + ADAM_WD * w), mn, vn


def _sum2_adam(name, qo, r2, w, m, v, l, prev):
    L, A, B = w.shape
    ta = _row_tile(A, B, 16)

    def kern(q_ref, r_ref, w_ref, m_ref, v_ref, *rest):
        g_ref, d_ref, mo_ref, vo_ref = rest[-4:]
        gv = q_ref[...].astype(F32)
        for j in range(3):
            gv = gv + r_ref[j].astype(F32)
        g_ref[...] = gv
        d_ref[...], mo_ref[...], vo_ref[...] = _adam_math(w_ref[...], gv, m_ref[...], v_ref[...])

    lay = pl.BlockSpec((None, ta, B), lambda i: (l, i, 0))
    in_specs = [pl.BlockSpec((ta, B), lambda i: (i, 0)), pl.BlockSpec((3, ta, B), lambda i: (0, i, 0)), lay, lay, lay]
    args = [qo, r2, w, m, v]
    aliases = {}
    if prev is not None:
        in_specs += [ANY] * 4
        args += list(prev)
        aliases = {5 + k: k for k in range(4)}
    return pl.pallas_call(kern, name=name, grid=(A // ta,), in_specs=in_specs, out_specs=[lay] * 4,
                          out_shape=[_sds((L, A, B), F32)] * 4, input_output_aliases=aliases,
                          compiler_params=_cp(("parallel",)))(*args)


def _adam(name, w, g, m, v):
    R, C = w.shape
    tr = _row_tile(R, C, 8)

    def kern(w_ref, g_ref, m_ref, v_ref, d_ref, mo_ref, vo_ref):
        d_ref[...], mo_ref[...], vo_ref[...] = _adam_math(w_ref[...], g_ref[...], m_ref[...], v_ref[...])

    blk = pl.BlockSpec((tr, C), lambda i: (i, 0))
    return pl.pallas_call(kern, name=name, grid=(R // tr,), in_specs=[blk] * 4, out_specs=[blk] * 3,
                          out_shape=[_sds((R, C), F32)] * 3, compiler_params=_cp(("parallel",)))(w, g, m, v)


def _rowsum8(name, v):
    n = v.shape[1]
    tn = _tile(n, 16384)

    def kern(v_ref, o_ref):
        o_ref[...] = jnp.sum(v_ref[...], axis=0, keepdims=True)

    return pl.pallas_call(kern, name=name, grid=(n // tn,), in_specs=[pl.BlockSpec((8, tn), lambda i: (0, i))],
                          out_specs=pl.BlockSpec((1, tn), lambda i: (0, i)), out_shape=_sds((1, n), F32))(v)


def _sum_devices(name, v):
    n = v.shape[2]
    tn = _tile(n, 4096)

    def kern(v_ref, o_ref):
        s = v_ref[0]
        for d in range(1, 8):
            s = s + v_ref[d]
        o_ref[...] = s

    return pl.pallas_call(kern, name=name, grid=(n // tn,), in_specs=[pl.BlockSpec((8, 8, tn), lambda i: (0, 0, i))],
                          out_specs=pl.BlockSpec((8, tn), lambda i: (0, i)), out_shape=_sds((8, n), F32))(v)


def _pack8(parts, quantum=8 * LANES):
    flat = jnp.concatenate([p.reshape(-1) for p in parts])
    n = flat.shape[0]
    npad = -n % quantum
    return jnp.pad(flat, (0, npad)).reshape(8, -1), n


def _unpack(flat, shapes):
    out, o = [], 0
    for s in shapes:
        k = 1
        for d in s:
            k *= d
        out.append(flat[o:o + k].reshape(s))
        o += k
    return out


def kernel(x, ssd_in_proj, ssd_conv_w, ssd_conv_b, ssd_dt_bias, ssd_A_log, ssd_D, ssd_norm_w, ssd_out_proj, pool_w, pool_b, pool_scale, mlp_w1, mlp_w2, ln_mix_g, ln_mix_b, ln_ffn_g, ln_ffn_b, loss_target, m_ssd_in_proj, m_ssd_conv_w, m_ssd_conv_b, m_ssd_dt_bias, m_ssd_A_log, m_ssd_D, m_ssd_norm_w, m_ssd_out_proj, m_pool_w, m_pool_b, m_pool_scale, m_mlp_w1, m_mlp_w2, m_ln_mix_g, m_ln_mix_b, m_ln_ffn_g, m_ln_ffn_b, v_ssd_in_proj, v_ssd_conv_w, v_ssd_conv_b, v_ssd_dt_bias, v_ssd_A_log, v_ssd_D, v_ssd_norm_w, v_ssd_out_proj, v_pool_w, v_pool_b, v_pool_scale, v_mlp_w1, v_mlp_w2, v_ln_mix_g, v_ln_mix_b, v_ln_ffn_g, v_ln_ffn_b):
    _, T, D = x.shape
    DI, DFF = 2 * D, 4 * D
    NZ = 4 * ssd_in_proj.shape[2]
    nssd, npool = ssd_in_proj.shape[0], pool_w.shape[0]
    ng = len(POOL_WINDOWS)
    dg = D // ng
    xi, yi, ci = _pos()
    chip = 2 * xi + yi

    def layer_slabs(i):
        j = i // 2
        if i % 2 == 0:
            return ["win", "wout", "w1", "w2"], [ssd_in_proj[j], ssd_out_proj[j], mlp_w1[i], mlp_w2[i]]
        return ["wp", "w1", "w2"], [pool_w[j].reshape(dg, dg), mlp_w1[i], mlp_w2[i]]

    def put_gathered(keys, arrs):
        for k, a in zip(keys, arrs):
            if k == "win":
                a = a.transpose(1, 0, 2).reshape(D, NZ)
            elif k == "wp":
                a = a.reshape(4, ng, dg // 4, dg).transpose(1, 0, 2, 3).reshape(ng, dg, dg)
            W[k].append(a)

    flights = {}

    def start(tag, keys, slabs, after):
        lands = [_cast_place("place_%s_%s" % (tag, k), s, chip) for k, s in zip(keys, slabs)]
        ssem, rsem, lands, token = _gather_start("gather_start_" + tag, lands, after)
        flights[tag] = (keys, lands, ssem, rsem)
        return lands[-1], token[0, 0]

    def finish(tag, after):
        keys, lands, ssem, rsem = flights.pop(tag)
        lands = _gather_wait("gather_wait_" + tag, lands, ssem, rsem, after)
        arrs = _sibling_fill("gather_fill_" + tag, lands)
        put_gathered(keys, arrs)
        return arrs[-1]

    def pre_fwd(i, xcur):
        if i == 0:
            keys, slabs = layer_slabs(0)
            arrs = _gather("gather_L0", [slabs[0].astype(BF16)], [True])
            put_gathered(keys[:1], arrs)
            last, _ = start("L0b", keys[1:], slabs[1:], arrs[0])
            return start("L1", *layer_slabs(1), last)[1]
        last = finish("L%d" % i, xcur)
        if i == 1:
            return start("L3", *layer_slabs(3), last)[1]
        return jnp.zeros((), F32)

    def mid_fwd(i, act):
        if i != 0:
            return jnp.zeros((), F32)
        last = finish("L0b", act)
        return start("L2", *layer_slabs(2), last)[1]

    g_cw, g_pb, g_ps = _gather("gather_small", [ssd_conv_w, pool_b, pool_scale], [False] * 3)
    W = dict(
        win=[], wout=[], w1=[], w2=[], wp=[],
        conv_w=[g_cw[:, j, :, 0, :].transpose(1, 0, 2).reshape(CONV_WIDTH, -1) for j in range(nssd)],
        conv_b=[ssd_conv_b[j].reshape(1, -1) for j in range(nssd)],
        dt_bias=[ssd_dt_bias[j].reshape(1, -1) for j in range(nssd)],
        a_log=[ssd_A_log[j].reshape(1, -1) for j in range(nssd)],
        dvec=[jnp.repeat(ssd_D[j], HEAD_DIM).reshape(1, -1) for j in range(nssd)],
        norm_w=[ssd_norm_w[j].reshape(1, -1) for j in range(nssd)],
        pool_b=[g_pb[:, j].transpose(1, 0, 2).reshape(1, -1) for j in range(npool)],
        pool_scale=[g_ps[:, j].reshape(1, -1) for j in range(npool)],
        ln_mix_g=[ln_mix_g[i].reshape(1, -1) for i in range(DEPTH)],
        ln_mix_b=[ln_mix_b[i].reshape(1, -1) for i in range(DEPTH)],
        ln_ffn_g=[ln_ffn_g[i].reshape(1, -1) for i in range(DEPTH)],
        ln_ffn_b=[ln_ffn_b[i].reshape(1, -1) for i in range(DEPTH)],
    )

    big = dict(win=(ssd_in_proj, m_ssd_in_proj, v_ssd_in_proj), wout=(ssd_out_proj, m_ssd_out_proj, v_ssd_out_proj),
               wp=(pool_w, m_pool_w, v_pool_w), w1=(mlp_w1, m_mlp_w1, v_mlp_w1), w2=(mlp_w2, m_mlp_w2, v_mlp_w2))
    big = {k: tuple(a.reshape(a.shape[0], -1, a.shape[-1]) for a in t) for k, t in big.items()}
    res = {}
    rflight = {}

    def finish_layer(i, keys, qos, r2s):
        for k, qo, r2 in zip(keys, qos, r2s):
            l = i // 2 if k in ("win", "wout", "wp") else i
            res[k] = _sum2_adam("adam_L%d_%s" % (i, k), qo, r2, *big[k], l, res.get(k))

    def pair_sums(tag, keys, Pi):
        units = [Pi[k] for k in keys]
        R1 = _rs1("rs1_" + tag, units)
        return [_sum1("sum1_%s_%s" % (tag, k), p, r, ci) for k, p, r in zip(keys, units, R1)]

    def exchange_start(tag, layer, keys, Pi):
        sems, bufs, token = _rs2_start("rs2_start_" + tag, pair_sums(tag, keys, Pi), Pi[keys[-1]])
        rflight[tag] = (layer, keys, sems, bufs)
        return token[0, 0]

    def exchange_finish(tag, after):
        layer, keys, sems, bufs = rflight.pop(tag)
        Qt, r2s, qsibs = _rs2_wait("rs2_wait_" + tag, sems, bufs, after)
        qos = []
        for q, qsib in zip(Qt, qsibs):
            h = q.shape[1]
            own = lax.dynamic_index_in_dim(q, chip, 0, keepdims=False)
            qos.append(lax.dynamic_update_slice(jnp.concatenate([qsib, qsib], axis=0), own, (ci * h, 0)))
        finish_layer(layer, keys, qos, r2s)

    def mid_bwd(i, Pi):
        if i != 0:
            return jnp.zeros((), F32)
        return exchange_start("L0a", 0, ["w1", "w2"], Pi)

    def post_bwd(i, dxcur, Pi):
        if i + 1 < DEPTH:
            exchange_finish("L%d" % (i + 1), dxcur)
        if i > 0:
            return exchange_start("L%d" % i, i, ["win", "wout", "w1", "w2"] if i % 2 == 0 else ["wp", "w1", "w2"], Pi)
        keys = ["win", "wout"]
        R2Q = _rs2("rs2_L0", pair_sums("L0", keys, Pi))
        finish_layer(0, keys, R2Q[len(keys):], R2Q[:len(keys)])
        exchange_finish("L0a", R2Q[0])
        return jnp.zeros((), F32)

    loss_blk, dx, P, small = _local_step(x[0], loss_target[0], W, pre_fwd, post_bwd, mid_fwd, mid_bwd)
    loss = lax.psum(loss_blk[0, 0], ("x", "y", "c"))
    res = {k: tuple(a.reshape(s.shape) for a in res[k])
           for k, s in dict(win=ssd_in_proj, wout=ssd_out_proj, wp=pool_w, w1=mlp_w1, w2=mlp_w2).items()}

    flat8 = jnp.concatenate([small[n] for n in SMALL_NAMES], axis=1)
    ns = flat8.shape[1]
    flat8 = jnp.pad(flat8, ((0, 0), (0, -ns % (8 * LANES))))
    mine8 = _rowsum8("small_rowsum", flat8).reshape(8, -1)
    tot = _sum_devices("small_sum", _allgather_small("small_allgather", mine8)).reshape(-1)
    sw = [ssd_conv_w, ssd_conv_b, ssd_dt_bias, ssd_A_log, ssd_D, ssd_norm_w, pool_b, pool_scale,
          ln_mix_g, ln_mix_b, ln_ffn_g, ln_ffn_b]
    sm = [m_ssd_conv_w, m_ssd_conv_b, m_ssd_dt_bias, m_ssd_A_log, m_ssd_D, m_ssd_norm_w, m_pool_b, m_pool_scale,
          m_ln_mix_g, m_ln_mix_b, m_ln_ffn_g, m_ln_ffn_b]
    sv = [v_ssd_conv_w, v_ssd_conv_b, v_ssd_dt_bias, v_ssd_A_log, v_ssd_D, v_ssd_norm_w, v_pool_b, v_pool_scale,
          v_ln_mix_g, v_ln_mix_b, v_ln_ffn_g, v_ln_ffn_b]
    full_shapes = [(nssd, CONV_WIDTH, 1, DI + 2 * N_GROUPS * D_STATE)] + [w.shape for w in sw[1:6]] \
        + [(npool, ng, dg), (npool, D)] + [w.shape for w in sw[8:]]
    sg = _unpack(tot, full_shapes)
    sg[0] = lax.dynamic_slice_in_dim(sg[0], chip * sw[0].shape[3], sw[0].shape[3], axis=3)
    sg[6] = lax.dynamic_slice_in_dim(sg[6], chip * sw[6].shape[2], sw[6].shape[2], axis=2)
    sg[7] = lax.dynamic_slice_in_dim(sg[7], chip * sw[7].shape[1], sw[7].shape[1], axis=1)
    packs = [_pack8(parts)[0] for parts in (sw, sg, sm, sv)]
    sd, smn, svn = _adam("adam_small", *packs)
    shapes = [w.shape for w in sw]
    sd, smn, svn = (_unpack(a.reshape(-1), shapes) for a in (sd, smn, svn))

    order = ["win", 0, 1, 2, 3, 4, 5, "wout", "wp", 6, 7, "w1", "w2", 8, 9, 10, 11]
    outs = [loss, dx.reshape(x.shape)]
    for slot, small_vals in ((0, sg), (1, sd), (2, smn), (3, svn)):
        for o in order:
            outs.append(res[o][slot] if isinstance(o, str) else small_vals[o])
    return tuple(outs)
```

```python
import functools

import jax
import jax.numpy as jnp
from jax import lax
from jax.experimental import pallas as pl
from jax.experimental.pallas import tpu as pltpu

F32 = jnp.float32
BF16 = jnp.bfloat16

HEAD_DIM = 64
N_GROUPS = 8
D_STATE = 128
CHUNK = 128
CONV_WIDTH = 5
POOL_WINDOWS = (2, 4, 8, 16)
DEPTH = 4
ALPHA = (2.0 * DEPTH) ** 0.25
LN_EPS = 1e-5
RMS_EPS = 1e-5
ADAM_LR, ADAM_B1, ADAM_B2, ADAM_EPS, ADAM_WD, ADAM_STEP = 0.001, 0.9, 0.999, 1e-08, 0.01, 10

LANES = 128
VMEM_LIMIT = 48 * 1024 * 1024
NEG = -1e30
MESH = pl.DeviceIdType.MESH

NN = (((1,), (0,)), ((), ()))
NT = (((1,), (1,)), ((), ()))
TN = (((0,), (0,)), ((), ()))


def _tile(dim, pref, mult=LANES):
    if dim <= pref:
        return dim
    t = (pref // mult) * mult
    while t > mult and dim % t:
        t -= mult
    assert dim % t == 0, (dim, pref)
    return t


def _cp(sem):
    return pltpu.CompilerParams(dimension_semantics=sem, vmem_limit_bytes=VMEM_LIMIT)


def _sds(shape, dtype):
    return jax.ShapeDtypeStruct(tuple(shape), dtype)


def _dot(a, b, dn=NN):
    return lax.dot_general(a, b, dn, preferred_element_type=F32)


def _bf(x):
    return x.astype(BF16)


def _sigmoid(x):
    return 1.0 / (1.0 + jnp.exp(-x))


def _mm(name, a, b, a_spec, b_spec, dn, grid, tm, tn, out_shapes, out_specs, epi=None, extras=(), extra_specs=()):
    nk = grid[2]
    n_ex, n_out = len(extras), len(out_shapes)

    def finish(acc, ex, outs):
        res = epi(acc, *[e[...] for e in ex]) if epi is not None else (acc,)
        for o, r in zip(outs, res):
            o[...] = r.astype(o.dtype)

    def kern_one(*refs):
        finish(_dot(_bf(refs[0][...]), _bf(refs[1][...]), dn), refs[2:2 + n_ex], refs[2 + n_ex:2 + n_ex + n_out])

    def kern(*refs):
        a_ref, b_ref = refs[0], refs[1]
        acc = refs[-1]
        k = pl.program_id(2)

        @pl.when(k == 0)
        def _():
            acc[...] = jnp.zeros_like(acc)

        acc[...] += _dot(_bf(a_ref[...]), _bf(b_ref[...]), dn)

        @pl.when(k == nk - 1)
        def _():
            finish(acc[...], refs[2:2 + n_ex], refs[2 + n_ex:2 + n_ex + n_out])

    return pl.pallas_call(
        kern_one if nk == 1 else kern, name=name, grid=grid, in_specs=[a_spec, b_spec, *extra_specs],
        out_specs=list(out_specs), out_shape=list(out_shapes),
        scratch_shapes=[] if nk == 1 else [pltpu.VMEM((tm, tn), F32)],
        compiler_params=_cp(("parallel", "parallel", "arbitrary")))(a, b, *extras)


class _ColShard:
    def __init__(self, R, C):
        self.R, self.C = R, C

    def b_nn(self, tk, tn):
        n = self.C // tn
        return pl.BlockSpec((None, tk, tn), lambda i, j, k: (j // n, k, j % n))

    def b_nt(self, tn, tk):
        n = self.C // tk
        return pl.BlockSpec((None, tn, tk), lambda i, j, k: (k // n, j, k % n))

    def out(self, tm, tn):
        n = self.C // tn
        return pl.BlockSpec((None, tm, tn), lambda i, j, k: (j // n, i, j % n))


class _RowShard:
    def __init__(self, R, C):
        self.R, self.C = R, C

    def b_nn(self, tk, tn):
        n = self.R // tk
        return pl.BlockSpec((None, tk, tn), lambda i, j, k: (k // n, k % n, j))

    def b_nt(self, tn, tk):
        n = self.R // tn
        return pl.BlockSpec((None, tn, tk), lambda i, j, k: (j // n, j % n, k))

    def out(self, tm, tn):
        n = self.R // tm
        return pl.BlockSpec((None, tm, tn), lambda i, j, k: (i // n, i % n, j))


def _a_nn(tm, tk):
    return pl.BlockSpec((tm, tk), lambda i, j, k: (i, k))


def _a_tn(tk, tm):
    return pl.BlockSpec((tk, tm), lambda i, j, k: (k, i))


def _b_tn(tk, tn):
    return pl.BlockSpec((tk, tn), lambda i, j, k: (k, j))


def _o_ij(tm, tn):
    return pl.BlockSpec((tm, tn), lambda i, j, k: (i, j))


def _mm_act(name, a, w, wspec, M, N, K, out_dtypes, epi=None, extras=(), nt=False, tn_pref=1024, tk_pref=2048):
    tm = _tile(M, 1024)
    if isinstance(wspec, (_ColShard, _RowShard)):
        nlim, klim = (wspec.R, wspec.C) if nt else (wspec.C, wspec.R)
        tn = _tile(nlim, tn_pref)
        tk = _tile(klim, tk_pref)
        b_spec = wspec.b_nt(tn, tk) if nt else wspec.b_nn(tk, tn)
    else:
        tn = _tile(N, tn_pref)
        tk = _tile(K, tk_pref)
        b_spec = (pl.BlockSpec((tn, tk), lambda i, j, k: (j, k)) if nt
                  else pl.BlockSpec((tk, tn), lambda i, j, k: (k, j)))
    grid = (M // tm, N // tn, K // tk)
    outs = [_sds((M, N), dt) for dt in out_dtypes]
    return _mm(name, a, w, _a_nn(tm, tk), b_spec, NT if nt else NN, grid, tm, tn, outs,
               [_o_ij(tm, tn)] * len(outs), epi, extras, [_o_ij(tm, tn)] * len(extras))


def _mm_wgrad(name, a, b, M, N, K, out_shape, out_spec_fn, tm_pref=1024, tn_pref=1024):
    tm = _tile(M, tm_pref)
    tn = _tile(N, tn_pref)
    tk = _tile(K, 4096)
    grid = (M // tm, N // tn, K // tk)
    return _mm(name, a, b, _a_tn(tk, tm), _b_tn(tk, tn), TN, grid, tm, tn, [out_shape], [out_spec_fn(tm, tn)])[0]


def _ln_stats(r):
    mu = jnp.mean(r, axis=-1, keepdims=True)
    xc = r - mu
    var = jnp.mean(xc * xc, axis=-1, keepdims=True)
    return xc, lax.rsqrt(var + LN_EPS)


def _part8(v):
    return v.reshape(v.shape[0] // 8, 8, v.shape[1]).sum(axis=0)


def _acc_out(ref, val, first):
    @pl.when(first)
    def _():
        ref[...] = val

    @pl.when(jnp.logical_not(first))
    def _():
        ref[...] += val


def _resln(name, x, mix, g, b):
    T, D = x.shape
    tr = _tile(T, 256, 8)

    def kern(x_ref, m_ref, g_ref, b_ref, r_ref, y_ref, yb_ref):
        r = ALPHA * x_ref[...] + m_ref[...]
        xc, rstd = _ln_stats(r)
        y = xc * rstd * g_ref[...] + b_ref[...]
        r_ref[...] = r
        y_ref[...] = y
        yb_ref[...] = _bf(y)

    row = pl.BlockSpec((tr, D), lambda i: (i, 0))
    vec = pl.BlockSpec((1, D), lambda i: (0, 0))
    return pl.pallas_call(kern, name=name, grid=(T // tr,), in_specs=[row, row, vec, vec], out_specs=[row, row, row],
                          out_shape=[_sds((T, D), F32), _sds((T, D), F32), _sds((T, D), BF16)],
                          compiler_params=_cp(("parallel",)))(x, mix, g, b)


def _lnbwd(name, dy, r, g):
    T, D = r.shape
    tr = _tile(T, 256, 8)

    def kern(dy_ref, r_ref, g_ref, dr_ref, drb_ref, dg_ref, db_ref):
        dyv = dy_ref[...]
        xc, rstd = _ln_stats(r_ref[...])
        xh = xc * rstd
        dxh = dyv * g_ref[...]
        m1 = jnp.mean(dxh, axis=-1, keepdims=True)
        m2 = jnp.mean(dxh * xh, axis=-1, keepdims=True)
        dr = rstd * (dxh - m1 - xh * m2)
        dr_ref[...] = dr
        drb_ref[...] = _bf(dr)
        first = pl.program_id(0) == 0
        _acc_out(dg_ref, _part8(dyv * xh), first)
        _acc_out(db_ref, _part8(dyv), first)

    row = pl.BlockSpec((tr, D), lambda i: (i, 0))
    vec = pl.BlockSpec((1, D), lambda i: (0, 0))
    acc = pl.BlockSpec((8, D), lambda i: (0, 0))
    return pl.pallas_call(kern, name=name, grid=(T // tr,), in_specs=[row, row, vec], out_specs=[row, row, acc, acc],
                          out_shape=[_sds((T, D), F32), _sds((T, D), BF16), _sds((8, D), F32), _sds((8, D), F32)],
                          compiler_params=_cp(("arbitrary",)))(dy, r, g)


def _loss_head(name, y, tgt):
    T, D = y.shape
    tr = _tile(T, 256, 8)
    nt = T // tr

    def kern(y_ref, t_ref, dy_ref, loss_ref, acc):
        i = pl.program_id(0)
        e = y_ref[...] - t_ref[...]
        dy_ref[...] = e * (1.0 / D)
        _acc_out(acc, _part8(e * e), i == 0)

        @pl.when(i == nt - 1)
        def _():
            tot = jnp.sum(jnp.sum(acc[...], axis=1, keepdims=True), axis=0, keepdims=True)
            loss_ref[...] = jnp.broadcast_to(tot * (0.5 / D), loss_ref.shape)

    row = pl.BlockSpec((tr, D), lambda i: (i, 0))
    return pl.pallas_call(kern, name=name, grid=(nt,), in_specs=[row, row],
                          out_specs=[row, pl.BlockSpec((8, LANES), lambda i: (0, 0))],
                          out_shape=[_sds((T, D), F32), _sds((8, LANES), F32)],
                          scratch_shapes=[pltpu.VMEM((8, D), F32)], compiler_params=_cp(("arbitrary",)))(y, tgt)


def _shift(x, o):
    if o == 0:
        return x
    T = x.shape[0]
    rolled = pltpu.roll(x, (-o) % T, 0)
    t = lax.broadcasted_iota(jnp.int32, x.shape, 0)
    return jnp.where((t + o >= 0) & (t + o < T), rolled, 0.0)


def _run(u, h, step):
    s, k = u, 1
    while k < h:
        s = s + _shift(s, step * k)
        k *= 2
    return s


def _winsum(u, win, transposed):
    h = win // 2
    if not transposed:
        return _run(u, h, 1) + _shift(_run(u, h, -1), -1)
    return _run(u, h, -1) + _shift(_run(u, h, 1), 1)


def _wincount(shape, win):
    t = lax.broadcasted_iota(jnp.int32, shape, 0)
    T = shape[0]
    lo = jnp.maximum(t - win // 2, 0)
    hi = jnp.minimum(t - win // 2 + win, T)
    return (hi - lo).astype(F32)


def _pool_m(name, u):
    T, D = u.shape
    per = (D // len(POOL_WINDOWS)) // LANES

    def kern(u_ref, m_ref):
        j = pl.program_id(0)
        for gi, win in enumerate(POOL_WINDOWS):
            @pl.when(j // per == gi)
            def _():
                uv = u_ref[...]
                m_ref[...] = _bf(_winsum(uv, win, False) / _wincount(uv.shape, win) - uv)

    col = pl.BlockSpec((T, LANES), lambda j: (0, j))
    return pl.pallas_call(kern, name=name, grid=(D // LANES,), in_specs=[col], out_specs=col,
                          out_shape=_sds((T, D), BF16), compiler_params=_cp(("parallel",)))(u)


def _pool_fwd(name, m, x, w, bias, scale, g, b):
    T, D = x.shape
    ng = len(POOL_WINDOWS)
    dg = D // ng
    tr = _tile(T, 256, 16)

    def kern(m_ref, x_ref, w_ref, bias_ref, sc_ref, g_ref, b_ref, yp_ref, r_ref, y_ref, yb_ref):
        for gi in range(ng):
            sl = slice(gi * dg, (gi + 1) * dg)
            yp_ref[:, sl] = _dot(m_ref[:, sl], w_ref[gi]) + bias_ref[:, sl]
        r = ALPHA * x_ref[...] + yp_ref[...] * sc_ref[...]
        xc, rstd = _ln_stats(r)
        y = xc * rstd * g_ref[...] + b_ref[...]
        r_ref[...] = r
        y_ref[...] = y
        yb_ref[...] = _bf(y)

    row = pl.BlockSpec((tr, D), lambda i: (i, 0))
    vec = pl.BlockSpec((1, D), lambda i: (0, 0))
    wsp = pl.BlockSpec((ng, dg, dg), lambda i: (0, 0, 0))
    return pl.pallas_call(kern, name=name, grid=(T // tr,), in_specs=[row, row, wsp, vec, vec, vec, vec],
                          out_specs=[row, row, row, row],
                          out_shape=[_sds((T, D), F32), _sds((T, D), F32), _sds((T, D), F32), _sds((T, D), BF16)],
                          compiler_params=_cp(("parallel",)))(m, x, w, bias, scale, g, b)


def _pool_bwd_a(name, dr, ypre, scale, w):
    T, D = dr.shape
    ng = len(POOL_WINDOWS)
    dg = D // ng
    tr = _tile(T, 256, 16)

    def kern(dr_ref, yp_ref, sc_ref, w_ref, dm_ref, dyp_ref, dsc_ref, dbi_ref):
        drv = dr_ref[...]
        dyp = drv * sc_ref[...]
        dyp_ref[...] = _bf(dyp)
        for gi in range(ng):
            sl = slice(gi * dg, (gi + 1) * dg)
            dm_ref[:, sl] = _dot(dyp_ref[:, sl], w_ref[gi], NT)
        first = pl.program_id(0) == 0
        _acc_out(dsc_ref, _part8(drv * yp_ref[...]), first)
        _acc_out(dbi_ref, _part8(dyp), first)

    row = pl.BlockSpec((tr, D), lambda i: (i, 0))
    vec = pl.BlockSpec((1, D), lambda i: (0, 0))
    acc = pl.BlockSpec((8, D), lambda i: (0, 0))
    wsp = pl.BlockSpec((ng, dg, dg), lambda i: (0, 0, 0))
    return pl.pallas_call(kern, name=name, grid=(T // tr,), in_specs=[row, row, vec, wsp],
                          out_specs=[row, row, acc, acc],
                          out_shape=[_sds((T, D), F32), _sds((T, D), BF16), _sds((8, D), F32), _sds((8, D), F32)],
                          compiler_params=_cp(("arbitrary",)))(dr, ypre, scale, w)


def _pool_bwd_win(name, dm, dr):
    T, D = dm.shape
    per = (D // len(POOL_WINDOWS)) // LANES

    def kern(dm_ref, dr_ref, du_ref):
        j = pl.program_id(0)
        for gi, win in enumerate(POOL_WINDOWS):
            @pl.when(j // per == gi)
            def _():
                dmv = dm_ref[...]
                du_ref[...] = ALPHA * dr_ref[...] + _winsum(dmv / _wincount(dmv.shape, win), win, True) - dmv

    col = pl.BlockSpec((T, LANES), lambda j: (0, j))
    return pl.pallas_call(kern, name=name, grid=(D // LANES,), in_specs=[col, col], out_specs=col,
                          out_shape=_sds((T, D), F32), compiler_params=_cp(("parallel",)))(dm, dr)


def _conv_pre(x, w_ref, b_ref):
    acc = b_ref[...] + w_ref[2:3, :] * x
    for k in (0, 1, 3, 4):
        acc = acc + w_ref[k:k + 1, :] * _shift(x, k - 2)
    return acc


def _conv_fwd(name, zx, blk0, wblk0, w, b, G, cw):
    T = zx.shape[0]
    per = cw // LANES

    def kern(x_ref, w_ref, b_ref, o_ref):
        pre = _conv_pre(x_ref[...], w_ref, b_ref)
        o_ref[0] = pre * _sigmoid(pre)

    return pl.pallas_call(
        kern, name=name, grid=(G * per,),
        in_specs=[pl.BlockSpec((T, LANES), lambda j: (0, blk0 + j)),
                  pl.BlockSpec((CONV_WIDTH, LANES), lambda j: (0, wblk0 + j)),
                  pl.BlockSpec((1, LANES), lambda j: (0, wblk0 + j))],
        out_specs=pl.BlockSpec((1, T, LANES), lambda j: (j // per, 0, j % per)),
        out_shape=_sds((G, T, cw), F32), compiler_params=_cp(("parallel",)))(zx, w, b)


def _conv_bwd(name, zx, blk0, wblk0, w, b, adds, dyd=None):
    T = zx.shape[0]
    G, _, cw = adds[0].shape
    per = cw // LANES
    na = len(adds)

    def kern(*refs):
        x_ref, w_ref, b_ref = refs[:3]
        add_refs = refs[3:3 + na]
        rest = refs[3 + na:]
        if dyd is not None:
            dy_ref, dv_ref = rest[:2]
            rest = rest[2:]
        dx_ref, db_ref = rest[0], rest[1]
        dw_refs = rest[2:]
        x = x_ref[...]
        pre = _conv_pre(x, w_ref, b_ref)
        sg = _sigmoid(pre)
        dact = add_refs[0][0]
        for r in add_refs[1:]:
            dact = dact + r[0]
        if dyd is not None:
            dact = dact + dy_ref[0] * dv_ref[...]
        dpre = dact * (sg * (1.0 + pre * (1.0 - sg)))
        row0 = lax.broadcasted_iota(jnp.int32, (8, LANES), 0) == 0

        def put(ref, v):
            ref[...] = jnp.where(row0, jnp.sum(v, axis=0, keepdims=True), 0.0)

        put(db_ref, dpre)
        dx = w_ref[2:3, :] * dpre
        put(dw_refs[2], dpre * x)
        for k in (0, 1, 3, 4):
            put(dw_refs[k], dpre * _shift(x, k - 2))
            dx = dx + w_ref[k:k + 1, :] * _shift(dpre, 2 - k)
        dx_ref[...] = _bf(dx)

    gsp = pl.BlockSpec((1, T, LANES), lambda j: (j // per, 0, j % per))
    in_specs = [pl.BlockSpec((T, LANES), lambda j: (0, blk0 + j)),
                pl.BlockSpec((CONV_WIDTH, LANES), lambda j: (0, wblk0 + j)),
                pl.BlockSpec((1, LANES), lambda j: (0, wblk0 + j))] + [gsp] * na
    args = [zx, w, b, *adds]
    if dyd is not None:
        in_specs += [gsp, pl.BlockSpec((1, LANES), lambda j: (0, j))]
        args += list(dyd)
    n = G * cw
    p8 = pl.BlockSpec((8, LANES), lambda j: (0, j))
    res = pl.pallas_call(
        kern, name=name, grid=(G * per,), in_specs=in_specs,
        out_specs=[pl.BlockSpec((T, LANES), lambda j: (0, j))] + [p8] * (1 + CONV_WIDTH),
        out_shape=[_sds((T, n), BF16)] + [_sds((8, n), F32)] * (1 + CONV_WIDTH),
        compiler_params=_cp(("parallel",)))(*args)
    return res[0], res[1], res[2:]


def _split3(x):
    x1 = _bf(x)
    r1 = x - x1.astype(F32)
    x2 = _bf(r1)
    x3 = _bf(r1 - x2.astype(F32))
    return x1, x2, x3


def _tri_dot(tri, x, dn=NN):
    a, b, c = _split3(x)
    return _dot(tri, a, dn) + _dot(tri, b, dn) + _dot(tri, c, dn)


def _tri(lower):
    i = lax.broadcasted_iota(jnp.int32, (CHUNK, CHUNK), 0)
    j = lax.broadcasted_iota(jnp.int32, (CHUNK, CHUNK), 1)
    return jnp.where((i >= j) if lower else (i <= j), 1.0, 0.0).astype(BF16)


def _softplus(x):
    return jnp.maximum(x, 0.0) + jnp.log(1.0 + jnp.exp(-jnp.abs(x)))


def _dt_prep(name, zx, dtblk, dt_bias, a_log):
    T = zx.shape[0]
    H = LANES // 2

    def kern(x_ref, bias_ref, al_ref, dt_ref, acs_ref, acst_ref):
        dt = _softplus(x_ref[...] + bias_ref[...])
        dta = dt * (-jnp.exp(al_ref[...]))
        lane = lax.broadcasted_iota(jnp.int32, dta.shape, 1)
        dt_ref[...] = dt
        acs = jnp.where(lane < H, _tri_dot(_tri(True), dta), _tri_dot(_tri(False), dta))
        acs_ref[...] = acs
        acst_ref[...] = acs.T

    blk = pl.BlockSpec((CHUNK, LANES), lambda c: (c, 0))
    vec = pl.BlockSpec((1, LANES), lambda c: (0, 0))
    return pl.pallas_call(kern, name=name, grid=(T // CHUNK,),
                          in_specs=[pl.BlockSpec((CHUNK, LANES), lambda c: (c, dtblk)), vec, vec],
                          out_specs=[blk, blk, pl.BlockSpec((LANES, CHUNK), lambda c: (0, c))],
                          out_shape=[_sds((T, LANES), F32), _sds((T, LANES), F32), _sds((LANES, T), F32)],
                          compiler_params=_cp(("parallel",)))(zx, dt_bias, a_log)


def _dt_bwd(name, ddta, ddtx, zx, dtblk, dt_bias, a_log):
    T = zx.shape[0]

    def kern(da_ref, dx_ref, x_ref, bias_ref, al_ref, draw_ref, dbias_ref, dal_ref):
        pre = x_ref[...] + bias_ref[...]
        dt = _softplus(pre)
        A = -jnp.exp(al_ref[...])
        dav = da_ref[...]
        draw = (dav * A + dx_ref[...]) * _sigmoid(pre)
        draw_ref[...] = _bf(draw)
        first = pl.program_id(0) == 0
        _acc_out(dbias_ref, _part8(draw), first)
        _acc_out(dal_ref, _part8(dav * dt) * A, first)

    blk = pl.BlockSpec((CHUNK, LANES), lambda c: (c, 0))
    vec = pl.BlockSpec((1, LANES), lambda c: (0, 0))
    acc = pl.BlockSpec((8, LANES), lambda c: (0, 0))
    return pl.pallas_call(kern, name=name, grid=(T // CHUNK,),
                          in_specs=[blk, blk, pl.BlockSpec((CHUNK, LANES), lambda c: (c, dtblk)), vec, vec],
                          out_specs=[blk, acc, acc],
                          out_shape=[_sds((T, LANES), BF16), _sds((8, LANES), F32), _sds((8, LANES), F32)],
                          compiler_params=_cp(("arbitrary",)))(ddta, ddtx, zx, dt_bias, a_log)


def _ssd_specs(T, GW, hpg, cmap):
    nc = T // CHUNK
    xs = pl.BlockSpec((1, CHUNK, GW), lambda g, c: (g, cmap(c), 0))
    bc = pl.BlockSpec((1, CHUNK, D_STATE), lambda g, c: (g, cmap(c), 0))
    nat = pl.BlockSpec((CHUNK, LANES), lambda g, c: (cmap(c), 0))
    natT = pl.BlockSpec((LANES, CHUNK), lambda g, c: (0, cmap(c)))
    st = pl.BlockSpec((1, 1, D_STATE, GW), lambda g, c: (g, cmap(c), 0, 0))
    ocol = pl.BlockSpec((1, CHUNK, hpg), lambda g, c: (g, cmap(c), 0))
    return nc, xs, bc, nat, natT, st, ocol


def _dot3(x, sel, dn=NN):
    a, b, c = _split3(x)
    return _dot(a, sel, dn) + _dot(b, sel, dn) + _dot(c, sel, dn)


def _head_select(base, GW):
    k = lax.broadcasted_iota(jnp.int32, (LANES, GW), 0)
    j = lax.broadcasted_iota(jnp.int32, (LANES, GW), 1)
    lo = (k - base) * HEAD_DIM
    return jnp.where((j >= lo) & (j < lo + HEAD_DIM), 1.0, 0.0).astype(BF16)


def _head_collect(GW):
    j = lax.broadcasted_iota(jnp.int32, (GW, LANES), 0)
    k = lax.broadcasted_iota(jnp.int32, (GW, LANES), 1)
    return jnp.where((j >= k * HEAD_DIM) & (j < (k + 1) * HEAD_DIM), 1.0, 0.0).astype(BF16)


def _chunk_mask(rev):
    li = lax.broadcasted_iota(jnp.int32, (CHUNK, CHUNK), 0)
    si = lax.broadcasted_iota(jnp.int32, (CHUNK, CHUNK), 1)
    return (li <= si) if rev else (li >= si)


def _ssd_fwd(name, xs, Bg, Cg, dt, acs, acsT, d):
    G, T, GW = xs.shape
    hpg = GW // HEAD_DIM
    rev = d == 1
    nc0 = T // CHUNK
    cmap = (lambda c: nc0 - 1 - c) if rev else (lambda c: c)
    nc, xs_s, bc_s, nat_s, natT_s, st_s, _ = _ssd_specs(T, GW, hpg, cmap)
    last = 0 if rev else CHUNK - 1

    def kern(xs_ref, b_ref, c_ref, dt_ref, ac_ref, art_ref, y_ref, st_ref, state):
        @pl.when(pl.program_id(1) == 0)
        def _():
            state[...] = jnp.zeros_like(state)

        base = d * (G * hpg) + pl.program_id(0) * hpg
        Hp = state[...]
        st_ref[0, 0] = Hp
        Bm = _bf(b_ref[0])
        Cm = _bf(c_ref[0])
        S = _dot(Cm, Bm, NT)
        mask = _chunk_mask(rev)
        sel = _head_select(base, GW)
        both = _dot3(jnp.concatenate([dt_ref[...], ac_ref[...]], axis=0), sel)
        dt_e, a_e = both[:CHUNK], both[CHUNK:]
        xdt = xs_ref[0] * dt_e
        a_end = a_e[last:last + 1, :]
        yo = _dot(Cm, _bf(Hp)) * jnp.exp(a_e)
        for r in range(hpg):
            hs = slice(r * HEAD_DIM, (r + 1) * HEAD_DIM)
            a_col = a_e[:, r * HEAD_DIM:r * HEAD_DIM + 1]
            lam = jnp.exp(jnp.where(mask, a_col - art_ref[pl.ds(base + r, 1), :], NEG))
            y_ref[0, :, hs] = _dot(_bf(S * lam), _bf(xdt[:, hs])) + yo[:, hs]
        Hn = _dot(Bm, _bf(xdt * jnp.exp(a_end - a_e)), TN)
        state[...] = jnp.exp(a_end) * Hp + Hn

    return pl.pallas_call(
        kern, name=name, grid=(G, nc), in_specs=[xs_s, bc_s, bc_s, nat_s, nat_s, natT_s], out_specs=[xs_s, st_s],
        out_shape=[_sds((G, T, GW), F32), _sds((G, nc, D_STATE, GW), F32)],
        scratch_shapes=[pltpu.VMEM((D_STATE, GW), F32)],
        compiler_params=_cp(("parallel", "arbitrary")))(xs, Bg, Cg, dt, acs, acsT)


def _ssd_bwd(name, dy, xs, Bg, Cg, dt, acs, acsT, states, d):
    G, T, GW = xs.shape
    hpg = GW // HEAD_DIM
    rev = d == 1
    nc0 = T // CHUNK
    cmap = (lambda c: c) if rev else (lambda c: nc0 - 1 - c)
    nc, xs_s, bc_s, nat_s, natT_s, st_s, ocol_s = _ssd_specs(T, GW, hpg, cmap)
    last = 0 if rev else CHUNK - 1

    def kern(dy_ref, xs_ref, b_ref, c_ref, dt_ref, ac_ref, art_ref, st_ref,
             dxs_ref, db_ref, dc_ref, dda_ref, ddx_ref, dstate, dxq):
        @pl.when(pl.program_id(1) == 0)
        def _():
            dstate[...] = jnp.zeros_like(dstate)

        base = d * (G * hpg) + pl.program_id(0) * hpg
        Bm = _bf(b_ref[0])
        Cm = _bf(c_ref[0])
        S = _dot(Cm, Bm, NT)
        mask = _chunk_mask(rev)
        sel = _head_select(base, GW)
        col = _head_collect(GW)
        both = _dot3(jnp.concatenate([dt_ref[...], ac_ref[...]], axis=0), sel)
        dt_e, a_e = both[:CHUNK], both[CHUNK:]
        x = xs_ref[0]
        dyv = dy_ref[0]
        xdt = x * dt_e
        a_end = a_e[last:last + 1, :]
        e_end = jnp.exp(a_end)
        dte = jnp.exp(a_end - a_e)
        Hp = st_ref[0, 0]
        dHn = dstate[...]
        Hpb, dHnb = _bf(Hp), _bf(dHn)
        BdH = _dot(Bm, dHnb)
        CHp = _dot(Cm, Hpb)
        Edy = jnp.exp(a_e) * dyv
        wv = dte * xdt
        dCa = _dot(_bf(Edy), Hpb, NT)
        dBa = _dot(_bf(wv), dHnb, NT)
        dstate[...] = e_end * dHn + _dot(Cm, _bf(Edy), TN)
        rowi = lax.broadcasted_iota(jnp.int32, (CHUNK, LANES), 0)
        dS = jnp.zeros((CHUNK, CHUNK), F32)
        Gd = []
        for r in range(hpg):
            hs = slice(r * HEAD_DIM, (r + 1) * HEAD_DIM)
            a_col = a_e[:, r * HEAD_DIM:r * HEAD_DIM + 1]
            lam = jnp.exp(jnp.where(mask, a_col - art_ref[pl.ds(base + r, 1), :], NEG))
            Mf = S * lam
            dyr = _bf(dyv[:, hs])
            dxq[:, hs] = _dot(_bf(Mf), dyr, TN)
            dM = _dot(dyr, _bf(xdt[:, hs]), NT)
            dS = dS + dM * lam
            Gm = dM * Mf
            Gd.append(_bf(Gm - Gm.T))
        jj = lax.broadcasted_iota(jnp.int32, (hpg * CHUNK, LANES), 0)
        kk = lax.broadcasted_iota(jnp.int32, (hpg * CHUNK, LANES), 1)
        per_head = jnp.where((jj >= kk * CHUNK) & (jj < (kk + 1) * CHUNK), 1.0, 0.0).astype(BF16)
        dq = _dot(jnp.concatenate(Gd, axis=1), per_head)
        dxdt = dxq[...] + dte * BdH
        dxs_ref[0] = dxdt * dt_e
        ts = _dot(_bf(wv * BdH), col)
        ddx_ref[0] = _dot(_bf(dxdt * x), col)[:, :hpg]
        hh = _dot3(jnp.broadcast_to(e_end * jnp.sum(dHn * Hp, axis=0, keepdims=True), (8, GW)), col)[0:1, :]
        tot = jnp.sum(ts, axis=0, keepdims=True) + hh
        da = dq + _dot(_bf(Edy * CHp), col) - ts + jnp.where(rowi == last, tot, 0.0)
        dda_ref[0] = _tri_dot(_tri(rev), da)[:, :hpg]
        dSb = _bf(dS)
        dc_ref[0] = dCa + _dot(dSb, Bm)
        db_ref[0] = dBa + _dot(dSb, Cm, TN)

    return pl.pallas_call(
        kern, name=name, grid=(G, nc), in_specs=[xs_s, xs_s, bc_s, bc_s, nat_s, nat_s, natT_s, st_s],
        out_specs=[xs_s, bc_s, bc_s, ocol_s, ocol_s],
        out_shape=[_sds((G, T, GW), F32), _sds((G, T, D_STATE), F32), _sds((G, T, D_STATE), F32),
                   _sds((G, T, hpg), F32), _sds((G, T, hpg), F32)],
        scratch_shapes=[pltpu.VMEM((D_STATE, GW), F32), pltpu.VMEM((CHUNK, GW), F32)],
        compiler_params=_cp(("parallel", "arbitrary")))(dy, xs, Bg, Cg, dt, acs, acsT, states)


def _gate_core(yf, yb, xs, z, dv):
    y = yf + yb + xs * dv
    sg = _sigmoid(z)
    sz = z * sg
    gy = y * sz
    rstd = lax.rsqrt(jnp.mean(gy * gy, axis=-1, keepdims=True) + RMS_EPS)
    return y, sg, sz, gy, rstd


def _gate_fwd(name, yf, yb, xs, zx, dvec, nw):
    G, T, GW = xs.shape
    tr = _tile(T, 512, 16)

    def kern(yf_ref, yb_ref, xs_ref, z_ref, dv_ref, nw_ref, o_ref):
        _, _, _, gy, rstd = _gate_core(yf_ref[0], yb_ref[0], xs_ref[0], z_ref[...], dv_ref[...])
        o_ref[...] = _bf(gy * rstd * nw_ref[...])

    gsp = pl.BlockSpec((1, tr, GW), lambda g, t: (g, t, 0))
    zsp = pl.BlockSpec((tr, GW), lambda g, t: (t, g))
    vsp = pl.BlockSpec((1, GW), lambda g, t: (0, g))
    return pl.pallas_call(kern, name=name, grid=(G, T // tr), in_specs=[gsp, gsp, gsp, zsp, vsp, vsp], out_specs=zsp,
                          out_shape=_sds((T, G * GW), BF16),
                          compiler_params=_cp(("parallel", "parallel")))(yf, yb, xs, zx, dvec, nw)


def _gate_bwd(name, dgyn, yf, yb, xs, zx, dvec, nw):
    G, T, GW = xs.shape
    tr = _tile(T, 512, 16)

    def kern(dg_ref, yf_ref, yb_ref, xs_ref, z_ref, dv_ref, nw_ref, dy_ref, dz_ref, dnw_ref, ddl_ref):
        xsv = xs_ref[0]
        zv = z_ref[...]
        y, sg, sz, gy, rstd = _gate_core(yf_ref[0], yb_ref[0], xsv, zv, dv_ref[...])
        n = gy * rstd
        dgv = dg_ref[...]
        dn = dgv * nw_ref[...]
        dgy = rstd * (dn - n * jnp.mean(dn * n, axis=-1, keepdims=True))
        dyv = dgy * sz
        dy_ref[0] = dyv
        dz_ref[...] = _bf(dgy * y * (sg * (1.0 + zv * (1.0 - sg))))
        first = pl.program_id(1) == 0
        _acc_out(dnw_ref, _part8(dgv * n), first)
        _acc_out(ddl_ref, _part8(dyv * xsv), first)

    gsp = pl.BlockSpec((1, tr, GW), lambda g, t: (g, t, 0))
    zsp = pl.BlockSpec((tr, GW), lambda g, t: (t, g))
    vsp = pl.BlockSpec((1, GW), lambda g, t: (0, g))
    asp = pl.BlockSpec((8, GW), lambda g, t: (0, g))
    return pl.pallas_call(kern, name=name, grid=(G, T // tr), in_specs=[zsp, gsp, gsp, gsp, zsp, vsp, vsp],
                          out_specs=[gsp, zsp, asp, asp],
                          out_shape=[_sds((G, T, GW), F32), _sds((T, G * GW), BF16), _sds((8, G * GW), F32),
                                     _sds((8, G * GW), F32)],
                          compiler_params=_cp(("parallel", "arbitrary")))(dgyn, yf, yb, xs, zx, dvec, nw)


def _head_sum(name, v):
    n = v.shape[1]
    H = n // HEAD_DIM

    def kern(v_ref, o_ref):
        i = lax.broadcasted_iota(jnp.int32, (n, H), 0)
        j = lax.broadcasted_iota(jnp.int32, (n, H), 1)
        sel = jnp.where((i >= j * HEAD_DIM) & (i < (j + 1) * HEAD_DIM), 1.0, 0.0).astype(BF16)
        a, b, c = _split3(v_ref[...])
        o_ref[...] = _dot(a, sel) + _dot(b, sel) + _dot(c, sel)

    return pl.pallas_call(kern, name=name, out_shape=_sds((8, H), F32))(v)


def _relu2_epi(acc):
    return acc, jnp.square(jnp.maximum(acc, 0.0))


def _dh_epi(acc, h):
    return (acc * (2.0 * jnp.maximum(h, 0.0)),)


def _resid_epi(acc, e):
    return (acc + ALPHA * e,)


def _ssd_fwd_layer(tag, x, xb, W, j, mid=None):
    T, D = x.shape
    DI = 2 * D
    DBC = N_GROUPS * D_STATE
    win = W["win"][j]
    NZ = win.shape[1]
    zx = _mm_act(tag + "_inproj", xb, win, None, T, NZ, D, [F32], tn_pref=1152)[0]
    zb = DI // LANES
    cw, cb = W["conv_w"][j], W["conv_b"][j]
    xs = _conv_fwd(tag + "_convx", zx, zb, 0, cw, cb, N_GROUPS, DI // N_GROUPS)
    Bg = _conv_fwd(tag + "_convb", zx, zb + DI // LANES, DI // LANES, cw, cb, N_GROUPS, D_STATE)
    Cg = _conv_fwd(tag + "_convc", zx, zb + (DI + DBC) // LANES, (DI + DBC) // LANES, cw, cb, N_GROUPS, D_STATE)
    dtblk = (2 * DI + 2 * DBC) // LANES
    dt, acs, acsT = _dt_prep(tag + "_dtprep", zx, dtblk, W["dt_bias"][j], W["a_log"][j])
    yf, stf = _ssd_fwd(tag + "_scanf", xs, Bg, Cg, dt, acs, acsT, 0)
    yb, stb = _ssd_fwd(tag + "_scanb", xs, Bg, Cg, dt, acs, acsT, 1)
    gyn = _gate_fwd(tag + "_gate", yf, yb, xs, zx, W["dvec"][j], W["norm_w"][j])
    tok = mid(gyn) if mid is not None else 0.0
    mix = _mm_act(tag + "_outproj", gyn, W["wout"][j], _RowShard(DI // 4, D), T, D, DI, [F32])[0]
    saved = dict(zx=zx, xs=xs, Bg=Bg, Cg=Cg, dt=dt, acs=acs, acsT=acsT, stf=stf, stb=stb,
                 yf=yf, yb=yb, gyn=gyn, dtblk=dtblk)
    return mix, saved, tok


def _ssd_bwd_layer(tag, xb, dr1, dr1b, W, j, s, small):
    T, D = dr1.shape
    DI = 2 * D
    DBC = N_GROUPS * D_STATE
    win = W["win"][j]
    NZ = win.shape[1]
    zx = s["zx"]
    rs = _RowShard(DI // 4, D)
    dgyn = _mm_act(tag + "_dgyn", dr1b, W["wout"][j], rs, T, DI, D, [F32], nt=True)[0]
    p_out = _mm_wgrad(tag + "_dwout", s["gyn"], dr1b, DI, D, T, _sds((4, DI // 4, D), BF16), rs.out)
    dvec, nw = W["dvec"][j], W["norm_w"][j]
    dy, dzb, dnw, ddl = _gate_bwd(tag + "_dgate", dgyn, s["yf"], s["yb"], s["xs"], zx, dvec, nw)
    res = []
    for d, st in ((0, s["stf"]), (1, s["stb"])):
        res.append(_ssd_bwd(tag + "_dscan%d" % d, dy, s["xs"], s["Bg"], s["Cg"], s["dt"], s["acs"], s["acsT"], st, d))
    nat = lambda k: jnp.concatenate([res[d][k].transpose(1, 0, 2).reshape(T, -1) for d in (0, 1)], axis=1)
    drawb, dbias, dal = _dt_bwd(tag + "_ddt", nat(3), nat(4), zx, s["dtblk"], W["dt_bias"][j], W["a_log"][j])
    zb = DI // LANES
    cw, cb = W["conv_w"][j], W["conv_b"][j]
    dxx, dbx, dwx = _conv_bwd(tag + "_dconvx", zx, zb, 0, cw, cb, [res[0][0], res[1][0]], (dy, dvec))
    dxb, dbb, dwb = _conv_bwd(tag + "_dconvb", zx, zb + DI // LANES, DI // LANES, cw, cb, [res[0][1], res[1][1]])
    dxc, dbc, dwc = _conv_bwd(tag + "_dconvc", zx, zb + (DI + DBC) // LANES, (DI + DBC) // LANES, cw, cb,
                              [res[0][2], res[1][2]])
    dzx = jnp.concatenate([dzb, dxx, dxb, dxc, drawb], axis=1)
    dwin = _mm_wgrad(tag + "_dwin", xb, dzx, D, NZ, T, _sds((D, NZ), BF16),
                     lambda tm, tn: pl.BlockSpec((tm, tn), lambda i, jj, k: (i, jj)), tn_pref=1152)
    p_in = dwin.reshape(D, 4, NZ // 4).transpose(1, 0, 2)
    dx = _mm_act(tag + "_dxin", dzx, win, None, T, D, NZ, [F32], epi=_resid_epi, extras=(dr1,), nt=True,
                 tk_pref=1152)[0]
    small["conv_w"].append(jnp.concatenate([p[k] for k in range(CONV_WIDTH) for p in (dwx, dwb, dwc)], axis=1))
    small["conv_b"].append(jnp.concatenate([dbx, dbb, dbc], axis=1))
    small["dt_bias"].append(dbias)
    small["a_log"].append(dal)
    small["d"].append(ddl)
    small["norm_w"].append(dnw)
    return dx, p_in, p_out


def _pool_bwd_layer(tag, dr1, W, j, s, small):
    T, D = dr1.shape
    ng = len(POOL_WINDOWS)
    dg = D // ng
    dm, dypb, dsc, dbi = _pool_bwd_a(tag + "_dpool", dr1, s["ypre"], W["pool_scale"][j], W["wp"][j])
    tk = _tile(T, 1024)
    dwp = _mm(tag + "_dwp", s["m"], dypb, pl.BlockSpec((tk, dg), lambda i, jj, k: (k, i)),
              pl.BlockSpec((tk, dg), lambda i, jj, k: (k, i)), TN, (ng, 1, T // tk), dg, dg,
              [_sds((ng, dg, dg), BF16)], [pl.BlockSpec((None, dg, dg), lambda i, jj, k: (i, 0, 0))])[0]
    p_pool = dwp.reshape(ng, 4, dg // 4, dg).transpose(1, 0, 2, 3).reshape(4, dg, dg)
    dx = _pool_bwd_win(tag + "_dwin", dm, dr1)
    small["pool_b"].append(dbi)
    small["pool_scale"].append(dsc)
    return dx, p_pool


SMALL_NAMES = ("conv_w", "conv_b", "dt_bias", "a_log", "d", "norm_w", "pool_b", "pool_scale",
               "ln_mix_g", "ln_mix_b", "ln_ffn_g", "ln_ffn_b")


def _local_step(x, tgt, W, pre_fwd=None, post_bwd=None, mid_fwd=None, mid_bwd=None):
    T, D = x.shape
    DFF = 4 * D
    cs, rs = _ColShard(D, DFF // 4), _RowShard(DFF // 4, D)
    saved = []
    xb = _bf(x)
    for i in range(DEPTH):
        j = i // 2
        tag = "L%d" % i
        b_mix = W["ln_mix_b"][i]
        if pre_fwd is not None:
            tok = pre_fwd(i, x)
            if i % 2 == 0:
                xb = xb + tok.astype(BF16)
            else:
                x = x + tok
        s = dict(x=x, xb=xb)
        if i % 2 == 0:
            mid = (lambda act, i=i: mid_fwd(i, act)) if mid_fwd is not None else None
            mix, ss, mtok = _ssd_fwd_layer(tag, x, xb, W, j, mid)
            s.update(ss)
            r1, x1, x1b = _resln(tag + "_lnmix", x, mix, W["ln_mix_g"][i], b_mix + mtok)
        else:
            m = _pool_m(tag + "_poolm", x)
            ypre, r1, x1, x1b = _pool_fwd(tag + "_pool", m, x, W["wp"][j], W["pool_b"][j], W["pool_scale"][j],
                                          W["ln_mix_g"][i], b_mix)
            s.update(m=m, ypre=ypre)
        h, a = _mm_act(tag + "_mlp1", x1b, W["w1"][i], cs, T, DFF, D, [F32, BF16], epi=_relu2_epi)
        mlp = _mm_act(tag + "_mlp2", a, W["w2"][i], rs, T, D, DFF, [F32])[0]
        r2, x2, x2b = _resln(tag + "_lnffn", x1, mlp, W["ln_ffn_g"][i], W["ln_ffn_b"][i])
        s.update(r1=r1, x1=x1, x1b=x1b, h=h, a=a, r2=r2)
        saved.append(s)
        x, xb = x2, x2b

    dx, loss = _loss_head("loss", x, tgt)
    small = {n: [] for n in SMALL_NAMES}
    P = dict(win=[], wout=[], w1=[], w2=[], wp=[])
    tok = 0.0
    for i in reversed(range(DEPTH)):
        j = i // 2
        tag = "L%d" % i
        s = saved[i]
        dr2, dr2b, dg2, db2 = _lnbwd(tag + "_dlnffn", dx, s["r2"], W["ln_ffn_g"][i] + tok)
        dh = _mm_act(tag + "_dh", dr2b, W["w2"][i], rs, T, DFF, D, [BF16], epi=_dh_epi, extras=(s["h"],), nt=True)[0]
        Pi = dict(w2=_mm_wgrad(tag + "_dw2", s["a"], dr2b, DFF, D, T, _sds((4, DFF // 4, D), BF16), rs.out),
                  w1=_mm_wgrad(tag + "_dw1", s["x1b"], dh, D, DFF, T, _sds((4, D, DFF // 4), BF16), cs.out))
        dx1 = _mm_act(tag + "_dx1", dh, W["w1"][i], cs, T, D, DFF, [F32], epi=_resid_epi, extras=(dr2,), nt=True)[0]
        mtok = mid_bwd(i, Pi) if mid_bwd is not None else 0.0
        dr1, dr1b, dg1, db1 = _lnbwd(tag + "_dlnmix", dx1, s["r1"], W["ln_mix_g"][i] + mtok)
        if i % 2 == 0:
            dx, Pi["win"], Pi["wout"] = _ssd_bwd_layer(tag, s["xb"], dr1, dr1b, W, j, s, small)
        else:
            dx, Pi["wp"] = _pool_bwd_layer(tag, dr1, W, j, s, small)
        for k, v in Pi.items():
            P[k].append(v)
        small["ln_ffn_g"].append(dg2)
        small["ln_ffn_b"].append(db2)
        small["ln_mix_g"].append(dg1)
        small["ln_mix_b"].append(db1)
        if post_bwd is not None:
            tok = post_bwd(i, dx, Pi)
    P = {k: v[::-1] for k, v in P.items()}
    small = {k: jnp.concatenate(v[::-1], axis=1) for k, v in small.items()}
    small["d"] = _head_sum("dD", small["d"])
    return loss, dx, P, small


ANY = pl.BlockSpec(memory_space=pl.ANY)


def _pos():
    return lax.axis_index("x"), lax.axis_index("y"), lax.axis_index("c")


def _other_chips(x, y):
    return [(1 - x, y), (x, 1 - y), (1 - x, 1 - y)]


def _rcopy(src, dst, ssem, rsem, dev):
    return pltpu.make_async_remote_copy(src_ref=src, dst_ref=dst, send_sem=ssem, recv_sem=rsem,
                                        device_id=dev, device_id_type=MESH)


def _gather(name, slabs, split):
    n = len(slabs)

    def body(*refs):
        src, out = refs[:n], refs[n:2 * n]
        ssem, rsem, fssem, frsem, lsem = refs[2 * n:]
        x, y, c = _pos()
        chip = 2 * x + y
        chips = _other_chips(x, y)
        sib = (x, y, 1 - c)

        def mine(t, half):
            if split[t]:
                h = slabs[t].shape[0] // 2
                return src[t].at[pl.ds(half * h, h)]
            return src[t]

        def region(t, ch, half):
            if split[t]:
                h = slabs[t].shape[0] // 2
                return out[t].at[ch, pl.ds(half * h, h)]
            return out[t].at[ch]

        local = [pltpu.make_async_copy(src[t], out[t].at[chip], lsem.at[t]) for t in range(n)]
        for cp in local:
            cp.start()
        sends = []
        for t in range(n):
            for j, (px, py) in enumerate(chips):
                cp = _rcopy(mine(t, c), region(t, chip, c), ssem.at[t, j], rsem.at[t, j], (px, py, c))
                cp.start()
                sends.append(cp)
        for t in range(n):
            for j, (px, py) in enumerate(chips):
                pch = 2 * px + py
                _rcopy(mine(t, c), region(t, pch, c), ssem.at[t, j], rsem.at[t, j], (px, py, c)).wait_recv()
                if split[t]:
                    cp = _rcopy(region(t, pch, c), region(t, pch, c), fssem.at[t, j], frsem.at[t, j], sib)
                    cp.start()
                    sends.append(cp)
        for t in range(n):
            if split[t]:
                for j, (px, py) in enumerate(chips):
                    pch = 2 * px + py
                    _rcopy(region(t, pch, 1 - c), region(t, pch, 1 - c), fssem.at[t, j], frsem.at[t, j],
                           sib).wait_recv()
        for cp in sends:
            cp.wait_send()
        for cp in local:
            cp.wait()

    sem = pltpu.SemaphoreType.DMA
    return pl.pallas_call(
        body, name=name, in_specs=[ANY] * n, out_specs=[ANY] * n,
        out_shape=[_sds((4,) + s.shape, s.dtype) for s in slabs],
        scratch_shapes=[sem((n, 3)), sem((n, 3)), sem((n, 3)), sem((n, 3)), sem((n,))])(*slabs)


def _rs1(name, Ps):
    n = len(Ps)

    def body(*refs):
        src, out = refs[:n], refs[n:2 * n]
        ssem, rsem = refs[2 * n:]
        x, y, c = _pos()
        cps = []
        for t in range(n):
            h = Ps[t].shape[1] // 2
            cp = _rcopy(src[t].at[pl.ds(0, 4), pl.ds((1 - c) * h, h)], out[t], ssem.at[t], rsem.at[t], (x, y, 1 - c))
            cp.start()
            cps.append(cp)
        for cp in cps:
            cp.wait()

    sem = pltpu.SemaphoreType.DMA
    return pl.pallas_call(
        body, name=name, in_specs=[ANY] * n, out_specs=[ANY] * n,
        out_shape=[_sds((4, p.shape[1] // 2, p.shape[2]), p.dtype) for p in Ps],
        scratch_shapes=[sem((n,)), sem((n,))])(*Ps)


def _rs2(name, Qs):
    n = len(Qs)

    def body(*refs):
        src, r2, qo = refs[:n], refs[n:2 * n], refs[2 * n:3 * n]
        ssem, rsem, fssem, frsem, qssem, qrsem, lsem = refs[3 * n:]
        x, y, c = _pos()
        chip = 2 * x + y
        chips = _other_chips(x, y)
        sib = (x, y, 1 - c)
        sends, local = [], []
        for t in range(n):
            h = Qs[t].shape[1]
            lc = pltpu.make_async_copy(src[t].at[chip], qo[t].at[pl.ds(c * h, h)], lsem.at[t])
            lc.start()
            local.append(lc)
            cp = _rcopy(src[t].at[chip], qo[t].at[pl.ds(c * h, h)], qssem.at[t], qrsem.at[t], sib)
            cp.start()
            sends.append(cp)
            for j, (px, py) in enumerate(chips):
                cp = _rcopy(src[t].at[2 * px + py], r2[t].at[j, pl.ds(c * h, h)], ssem.at[t, j], rsem.at[t, j],
                            (px, py, c))
                cp.start()
                sends.append(cp)
        for t in range(n):
            h = Qs[t].shape[1]
            for j, (px, py) in enumerate(chips):
                mine = r2[t].at[j, pl.ds(c * h, h)]
                _rcopy(src[t].at[chip], mine, ssem.at[t, j], rsem.at[t, j], (px, py, c)).wait_recv()
                cp = _rcopy(mine, mine, fssem.at[t, j], frsem.at[t, j], sib)
                cp.start()
                sends.append(cp)
        for t in range(n):
            h = Qs[t].shape[1]
            _rcopy(src[t].at[chip], qo[t].at[pl.ds((1 - c) * h, h)], qssem.at[t], qrsem.at[t], sib).wait_recv()
            for j in range(3):
                theirs = r2[t].at[j, pl.ds((1 - c) * h, h)]
                _rcopy(theirs, theirs, fssem.at[t, j], frsem.at[t, j], sib).wait_recv()
        for cp in sends:
            cp.wait_send()
        for lc in local:
            lc.wait()

    sem = pltpu.SemaphoreType.DMA
    return pl.pallas_call(
        body, name=name, in_specs=[ANY] * n, out_specs=[ANY] * (2 * n),
        out_shape=[_sds((3, 2 * q.shape[1], q.shape[2]), q.dtype) for q in Qs]
        + [_sds((2 * q.shape[1], q.shape[2]), q.dtype) for q in Qs],
        scratch_shapes=[sem((n, 3)), sem((n, 3)), sem((n, 3)), sem((n, 3)), sem((n,)), sem((n,)), sem((n,))])(*Qs)


HBM = pl.BlockSpec(memory_space=pltpu.HBM)
SEMS = pl.BlockSpec(memory_space=pltpu.SEMAPHORE)
EFFECT = pltpu.SideEffectType.DATAFLOW_SIDE_EFFECTING
TOKEN = _sds((8, LANES), F32)


def _in_hbm(a):
    return pltpu.with_memory_space_constraint(a, pltpu.HBM)


def _cast_place(name, w, chip):
    A, B = w.shape
    ta = _row_tile(A, B, 16)

    def kern(s_ref, w_ref, o_ref):
        o_ref[0] = _bf(w_ref[...])

    gs = pltpu.PrefetchScalarGridSpec(
        num_scalar_prefetch=1, grid=(A // ta,), in_specs=[pl.BlockSpec((ta, B), lambda i, s_ref: (i, 0))],
        out_specs=pl.BlockSpec((1, ta, B), lambda i, s_ref: (s_ref[0], i, 0)))
    return pl.pallas_call(kern, name=name, grid_spec=gs, out_shape=_sds((4, A, B), BF16),
                          compiler_params=_cp(("parallel",)))(chip.reshape(1), w)


def _gather_start(name, lands, after):
    n = len(lands)

    def body(*refs):
        land = refs[:n]
        ssem, rsem = refs[n + 1], refs[n + 2]
        token = refs[-1]
        x, y, c = _pos()
        chip = 2 * x + y
        for t in range(n):
            h = lands[t].shape[1] // 2
            mine = land[t].at[chip, pl.ds(c * h, h)]
            for j, (px, py) in enumerate(_other_chips(x, y)):
                _rcopy(mine, mine, ssem.at[3 * t + j], rsem.at[3 * t + j], (px, py, c)).start()
        token[...] = jnp.zeros_like(token)

    sem = pltpu.SemaphoreType.DMA
    res = pl.pallas_call(
        body, name=name, in_specs=[HBM] * n + [ANY],
        out_specs=(SEMS, SEMS, *[HBM] * n, pl.BlockSpec(memory_space=pltpu.VMEM)),
        out_shape=(sem((3 * n,)), sem((3 * n,)), *[pltpu.HBM(l.shape, l.dtype) for l in lands], TOKEN),
        input_output_aliases={t: 2 + t for t in range(n)},
        compiler_params=pltpu.CompilerParams(has_side_effects=EFFECT))(*[_in_hbm(l) for l in lands], after)
    return res[0], res[1], list(res[2:2 + n]), res[-1]


def _gather_wait(name, lands, ssem, rsem, after):
    n = len(lands)

    def body(*refs):
        land = refs[:n]
        ssem_ref, rsem_ref = refs[n], refs[n + 1]
        x, y, c = _pos()
        chip = 2 * x + y
        for t in range(n):
            h = lands[t].shape[1] // 2
            for j, (px, py) in enumerate(_other_chips(x, y)):
                cp = _rcopy(land[t].at[chip, pl.ds(c * h, h)], land[t].at[2 * px + py, pl.ds(c * h, h)],
                            ssem_ref.at[3 * t + j], rsem_ref.at[3 * t + j], (px, py, c))
                cp.wait_send()
                cp.wait_recv()

    return pl.pallas_call(
        body, name=name, in_specs=[HBM] * n + [SEMS, SEMS, ANY], out_specs=[HBM] * n,
        out_shape=[pltpu.HBM(l.shape, l.dtype) for l in lands], input_output_aliases={t: t for t in range(n)},
        compiler_params=pltpu.CompilerParams(has_side_effects=EFFECT))(*lands, ssem, rsem, after)


def _sibling_fill(name, lands):
    n = len(lands)

    def body(*refs):
        src, out = refs[:n], refs[n:2 * n]
        ssem, rsem = refs[2 * n:]
        x, y, c = _pos()
        cps = []
        for t in range(n):
            h = lands[t].shape[1] // 2
            for j, (px, py) in enumerate(_other_chips(x, y)):
                pch = 2 * px + py
                cp = _rcopy(src[t].at[pch, pl.ds(c * h, h)], out[t].at[pch, pl.ds(c * h, h)], ssem.at[t, j],
                            rsem.at[t, j], (x, y, 1 - c))
                cp.start()
                cps.append(cp)
        for cp in cps:
            cp.wait()

    sem = pltpu.SemaphoreType.DMA
    return pl.pallas_call(
        body, name=name, in_specs=[ANY] * n, out_specs=[ANY] * n, out_shape=[_sds(l.shape, l.dtype) for l in lands],
        input_output_aliases={t: t for t in range(n)}, scratch_shapes=[sem((n, 3)), sem((n, 3))])(*lands)


def _rs2_start(name, Qs, after):
    n = len(Qs)
    r2s = [lax.empty((3, 2 * q.shape[1], q.shape[2]), q.dtype) for q in Qs]
    qsibs = [lax.empty(q.shape[1:], q.dtype) for q in Qs]

    def body(*refs):
        src, r2, qsib = refs[:n], refs[n:2 * n], refs[2 * n:3 * n]
        ssem, rsem, qs, qr = refs[3 * n + 1:3 * n + 5]
        token = refs[-1]
        x, y, c = _pos()
        chip = 2 * x + y
        for t in range(n):
            h = Qs[t].shape[1]
            for j, (px, py) in enumerate(_other_chips(x, y)):
                dst = r2[t].at[j, pl.ds(c * h, h)]
                s0 = 6 * t + 2 * j
                _rcopy(src[t].at[2 * px + py], dst, ssem.at[s0], rsem.at[s0], (px, py, c)).start()
                _rcopy(src[t].at[2 * px + py], dst, ssem.at[s0 + 1], rsem.at[s0 + 1], (px, py, 1 - c)).start()
            _rcopy(src[t].at[chip], qsib[t], qs.at[t], qr.at[t], (x, y, 1 - c)).start()
        token[...] = jnp.zeros_like(token)

    sem = pltpu.SemaphoreType.DMA
    bufs = list(Qs) + r2s + qsibs
    res = pl.pallas_call(
        body, name=name, in_specs=[HBM] * (3 * n) + [ANY],
        out_specs=(SEMS, SEMS, SEMS, SEMS, *[HBM] * (3 * n), pl.BlockSpec(memory_space=pltpu.VMEM)),
        out_shape=(sem((6 * n,)), sem((6 * n,)), sem((n,)), sem((n,)),
                   *[pltpu.HBM(b.shape, b.dtype) for b in bufs], TOKEN),
        input_output_aliases={t: 4 + t for t in range(3 * n)},
        compiler_params=pltpu.CompilerParams(has_side_effects=EFFECT))(*[_in_hbm(b) for b in bufs], after)
    return res[:4], list(res[4:4 + 3 * n]), res[-1]


def _rs2_wait(name, sems, bufs, after):
    n = len(bufs) // 3

    def body(*refs):
        src, r2, qsib = refs[:n], refs[n:2 * n], refs[2 * n:3 * n]
        ssem, rsem, qs, qr = refs[3 * n:3 * n + 4]
        x, y, c = _pos()
        chip = 2 * x + y
        for t in range(n):
            h = bufs[t].shape[1]
            for j, (px, py) in enumerate(_other_chips(x, y)):
                for k, pc in ((0, c), (1, 1 - c)):
                    s0 = 6 * t + 2 * j + k
                    cp = _rcopy(src[t].at[chip], r2[t].at[j, pl.ds(pc * h, h)], ssem.at[s0], rsem.at[s0],
                                (px, py, pc))
                    cp.wait_send()
                    cp.wait_recv()
            cp = _rcopy(src[t].at[chip], qsib[t], qs.at[t], qr.at[t], (x, y, 1 - c))
            cp.wait_send()
            cp.wait_recv()

    res = pl.pallas_call(
        body, name=name, in_specs=[HBM] * (3 * n) + [SEMS] * 4 + [ANY], out_specs=[HBM] * (3 * n),
        out_shape=[pltpu.HBM(b.shape, b.dtype) for b in bufs], input_output_aliases={t: t for t in range(3 * n)},
        compiler_params=pltpu.CompilerParams(has_side_effects=EFFECT))(*bufs, *sems, after)
    return list(res[:n]), list(res[n:2 * n]), list(res[2 * n:])


def _allgather_small(name, v):
    def body(v_ref, out_ref, ssem, rsem, lsem):
        x, y, c = _pos()
        me = 4 * x + 2 * y + c
        lc = pltpu.make_async_copy(v_ref, out_ref.at[me], lsem)
        lc.start()
        cps = []
        for k in range(1, 8):
            flip = lambda a, bit: (1 - a) if bit else a
            peer = (flip(x, k & 4), flip(y, k & 2), flip(c, k & 1))
            cp = _rcopy(v_ref, out_ref.at[me], ssem.at[k - 1], rsem.at[k - 1], peer)
            cp.start()
            cps.append(cp)
        for cp in cps:
            cp.wait()
        lc.wait()

    sem = pltpu.SemaphoreType.DMA
    return pl.pallas_call(body, name=name, in_specs=[ANY], out_specs=ANY, out_shape=_sds((8,) + v.shape, v.dtype),
                          scratch_shapes=[sem((7,)), sem((7,)), sem])(v)


def _row_tile(R, C, mult):
    return _tile(R, max(mult, (1 << 19) // C), mult)


def _sum1(name, P, R1, c):
    _, A, B = P.shape
    h = A // 2
    ta = _row_tile(h, B, 16)
    nb = h // ta

    def kern(c_ref, p_ref, r_ref, q_ref):
        q_ref[...] = _bf(p_ref[...].astype(F32) + r_ref[...].astype(F32))

    gs = pltpu.PrefetchScalarGridSpec(
        num_scalar_prefetch=1, grid=(4, nb),
        in_specs=[pl.BlockSpec((1, ta, B), lambda s, i, c_ref: (s, c_ref[0] * nb + i, 0)),
                  pl.BlockSpec((1, ta, B), lambda s, i, c_ref: (s, i, 0))],
        out_specs=pl.BlockSpec((1, ta, B), lambda s, i, c_ref: (s, i, 0)))
    return pl.pallas_call(kern, name=name, grid_spec=gs, out_shape=_sds((4, h, B), BF16),
                          compiler_params=_cp(("parallel", "parallel")))(c.reshape(1), P, R1)


def _adam_math(w, gv, m, v):
    mn = ADAM_B1 * m + (1.0 - ADAM_B1) * gv
    vn = ADAM_B2 * v + (1.0 - ADAM_B2) * jnp.square(gv)
    m_hat = mn / (1.0 - ADAM_B1 ** ADAM_STEP)
    v_hat = vn / (1.0 - ADAM_B2 ** ADAM_STEP)
    return -ADAM_LR * (m_hat / (jnp.sqrt(v_hat) + ADAM_EPS) + ADAM_WD * w), mn, vn


def _sum2_adam(name, qo, r2, w, m, v, l, prev):
    L, A, B = w.shape
    ta = _row_tile(A, B, 16)

    def kern(q_ref, r_ref, w_ref, m_ref, v_ref, *rest):
        g_ref, d_ref, mo_ref, vo_ref = rest[-4:]
        gv = q_ref[...].astype(F32)
        for j in range(3):
            gv = gv + r_ref[j].astype(F32)
        g_ref[...] = gv
        d_ref[...], mo_ref[...], vo_ref[...] = _adam_math(w_ref[...], gv, m_ref[...], v_ref[...])

    lay = pl.BlockSpec((None, ta, B), lambda i: (l, i, 0))
    in_specs = [pl.BlockSpec((ta, B), lambda i: (i, 0)), pl.BlockSpec((3, ta, B), lambda i: (0, i, 0)), lay, lay, lay]
    args = [qo, r2, w, m, v]
    aliases = {}
    if prev is not None:
        in_specs += [ANY] * 4
        args += list(prev)
        aliases = {5 + k: k for k in range(4)}
    return pl.pallas_call(kern, name=name, grid=(A // ta,), in_specs=in_specs, out_specs=[lay] * 4,
                          out_shape=[_sds((L, A, B), F32)] * 4, input_output_aliases=aliases,
                          compiler_params=_cp(("parallel",)))(*args)


def _adam(name, w, g, m, v):
    R, C = w.shape
    tr = _row_tile(R, C, 8)

    def kern(w_ref, g_ref, m_ref, v_ref, d_ref, mo_ref, vo_ref):
        d_ref[...], mo_ref[...], vo_ref[...] = _adam_math(w_ref[...], g_ref[...], m_ref[...], v_ref[...])

    blk = pl.BlockSpec((tr, C), lambda i: (i, 0))
    return pl.pallas_call(kern, name=name, grid=(R // tr,), in_specs=[blk] * 4, out_specs=[blk] * 3,
                          out_shape=[_sds((R, C), F32)] * 3, compiler_params=_cp(("parallel",)))(w, g, m, v)


def _rowsum8(name, v):
    n = v.shape[1]
    tn = _tile(n, 16384)

    def kern(v_ref, o_ref):
        o_ref[...] = jnp.sum(v_ref[...], axis=0, keepdims=True)

    return pl.pallas_call(kern, name=name, grid=(n // tn,), in_specs=[pl.BlockSpec((8, tn), lambda i: (0, i))],
                          out_specs=pl.BlockSpec((1, tn), lambda i: (0, i)), out_shape=_sds((1, n), F32))(v)


def _sum_devices(name, v):
    n = v.shape[2]
    tn = _tile(n, 4096)

    def kern(v_ref, o_ref):
        s = v_ref[0]
        for d in range(1, 8):
            s = s + v_ref[d]
        o_ref[...] = s

    return pl.pallas_call(kern, name=name, grid=(n // tn,), in_specs=[pl.BlockSpec((8, 8, tn), lambda i: (0, 0, i))],
                          out_specs=pl.BlockSpec((8, tn), lambda i: (0, i)), out_shape=_sds((8, n), F32))(v)


def _pack8(parts, quantum=8 * LANES):
    flat = jnp.concatenate([p.reshape(-1) for p in parts])
    n = flat.shape[0]
    npad = -n % quantum
    return jnp.pad(flat, (0, npad)).reshape(8, -1), n


def _unpack(flat, shapes):
    out, o = [], 0
    for s in shapes:
        k = 1
        for d in s:
            k *= d
        out.append(flat[o:o + k].reshape(s))
        o += k
    return out


def kernel(x, ssd_in_proj, ssd_conv_w, ssd_conv_b, ssd_dt_bias, ssd_A_log, ssd_D, ssd_norm_w, ssd_out_proj, pool_w, pool_b, pool_scale, mlp_w1, mlp_w2, ln_mix_g, ln_mix_b, ln_ffn_g, ln_ffn_b, loss_target, m_ssd_in_proj, m_ssd_conv_w, m_ssd_conv_b, m_ssd_dt_bias, m_ssd_A_log, m_ssd_D, m_ssd_norm_w, m_ssd_out_proj, m_pool_w, m_pool_b, m_pool_scale, m_mlp_w1, m_mlp_w2, m_ln_mix_g, m_ln_mix_b, m_ln_ffn_g, m_ln_ffn_b, v_ssd_in_proj, v_ssd_conv_w, v_ssd_conv_b, v_ssd_dt_bias, v_ssd_A_log, v_ssd_D, v_ssd_norm_w, v_ssd_out_proj, v_pool_w, v_pool_b, v_pool_scale, v_mlp_w1, v_mlp_w2, v_ln_mix_g, v_ln_mix_b, v_ln_ffn_g, v_ln_ffn_b):
    _, T, D = x.shape
    DI, DFF = 2 * D, 4 * D
    NZ = 4 * ssd_in_proj.shape[2]
    nssd, npool = ssd_in_proj.shape[0], pool_w.shape[0]
    ng = len(POOL_WINDOWS)
    dg = D // ng
    xi, yi, ci = _pos()
    chip = 2 * xi + yi

    def layer_slabs(i):
        j = i // 2
        if i % 2 == 0:
            return ["win", "wout", "w1", "w2"], [ssd_in_proj[j], ssd_out_proj[j], mlp_w1[i], mlp_w2[i]]
        return ["wp", "w1", "w2"], [pool_w[j].reshape(dg, dg), mlp_w1[i], mlp_w2[i]]

    def put_gathered(keys, arrs):
        for k, a in zip(keys, arrs):
            if k == "win":
                a = a.transpose(1, 0, 2).reshape(D, NZ)
            elif k == "wp":
                a = a.reshape(4, ng, dg // 4, dg).transpose(1, 0, 2, 3).reshape(ng, dg, dg)
            W[k].append(a)

    flights = {}

    def start(tag, keys, slabs, after):
        lands = [_cast_place("place_%s_%s" % (tag, k), s, chip) for k, s in zip(keys, slabs)]
        ssem, rsem, lands, token = _gather_start("gather_start_" + tag, lands, after)
        flights[tag] = (keys, lands, ssem, rsem)
        return lands[-1], token[0, 0]

    def finish(tag, after):
        keys, lands, ssem, rsem = flights.pop(tag)
        lands = _gather_wait("gather_wait_" + tag, lands, ssem, rsem, after)
        arrs = _sibling_fill("gather_fill_" + tag, lands)
        put_gathered(keys, arrs)
        return arrs[-1]

    def pre_fwd(i, xcur):
        if i == 0:
            keys, slabs = layer_slabs(0)
            arrs = _gather("gather_L0", [slabs[0].astype(BF16)], [True])
            put_gathered(keys[:1], arrs)
            last, _ = start("L0b", keys[1:], slabs[1:], arrs[0])
            return start("L1", *layer_slabs(1), last)[1]
        last = finish("L%d" % i, xcur)
        if i == 1:
            return start("L3", *layer_slabs(3), last)[1]
        return jnp.zeros((), F32)

    def mid_fwd(i, act):
        if i != 0:
            return jnp.zeros((), F32)
        last = finish("L0b", act)
        return start("L2", *layer_slabs(2), last)[1]

    g_cw, g_pb, g_ps = _gather("gather_small", [ssd_conv_w, pool_b, pool_scale], [False] * 3)
    W = dict(
        win=[], wout=[], w1=[], w2=[], wp=[],
        conv_w=[g_cw[:, j, :, 0, :].transpose(1, 0, 2).reshape(CONV_WIDTH, -1) for j in range(nssd)],
        conv_b=[ssd_conv_b[j].reshape(1, -1) for j in range(nssd)],
        dt_bias=[ssd_dt_bias[j].reshape(1, -1) for j in range(nssd)],
        a_log=[ssd_A_log[j].reshape(1, -1) for j in range(nssd)],
        dvec=[jnp.repeat(ssd_D[j], HEAD_DIM).reshape(1, -1) for j in range(nssd)],
        norm_w=[ssd_norm_w[j].reshape(1, -1) for j in range(nssd)],
        pool_b=[g_pb[:, j].transpose(1, 0, 2).reshape(1, -1) for j in range(npool)],
        pool_scale=[g_ps[:, j].reshape(1, -1) for j in range(npool)],
        ln_mix_g=[ln_mix_g[i].reshape(1, -1) for i in range(DEPTH)],
        ln_mix_b=[ln_mix_b[i].reshape(1, -1) for i in range(DEPTH)],
        ln_ffn_g=[ln_ffn_g[i].reshape(1, -1) for i in range(DEPTH)],
        ln_ffn_b=[ln_ffn_b[i].reshape(1, -1) for i in range(DEPTH)],
    )

    big = dict(win=(ssd_in_proj, m_ssd_in_proj, v_ssd_in_proj), wout=(ssd_out_proj, m_ssd_out_proj, v_ssd_out_proj),
               wp=(pool_w, m_pool_w, v_pool_w), w1=(mlp_w1, m_mlp_w1, v_mlp_w1), w2=(mlp_w2, m_mlp_w2, v_mlp_w2))
    big = {k: tuple(a.reshape(a.shape[0], -1, a.shape[-1]) for a in t) for k, t in big.items()}
    res = {}
    rflight = {}

    def finish_layer(i, keys, qos, r2s):
        for k, qo, r2 in zip(keys, qos, r2s):
            l = i // 2 if k in ("win", "wout", "wp") else i
            res[k] = _sum2_adam("adam_L%d_%s" % (i, k), qo, r2, *big[k], l, res.get(k))

    def pair_sums(tag, keys, Pi):
        units = [Pi[k] for k in keys]
        R1 = _rs1("rs1_" + tag, units)
        return [_sum1("sum1_%s_%s" % (tag, k), p, r, ci) for k, p, r in zip(keys, units, R1)]

    def exchange_start(tag, layer, keys, Pi):
        sems, bufs, token = _rs2_start("rs2_start_" + tag, pair_sums(tag, keys, Pi), Pi[keys[-1]])
        rflight[tag] = (layer, keys, sems, bufs)
        return token[0, 0]

    def exchange_finish(tag, after):
        layer, keys, sems, bufs = rflight.pop(tag)
        Qt, r2s, qsibs = _rs2_wait("rs2_wait_" + tag, sems, bufs, after)
        qos = []
        for q, qsib in zip(Qt, qsibs):
            h = q.shape[1]
            own = lax.dynamic_index_in_dim(q, chip, 0, keepdims=False)
            qos.append(lax.dynamic_update_slice(jnp.concatenate([qsib, qsib], axis=0), own, (ci * h, 0)))
        finish_layer(layer, keys, qos, r2s)

    def mid_bwd(i, Pi):
        if i != 0:
            return jnp.zeros((), F32)
        return exchange_start("L0a", 0, ["w1", "w2"], Pi)

    def post_bwd(i, dxcur, Pi):
        if i + 1 < DEPTH:
            exchange_finish("L%d" % (i + 1), dxcur)
        if i > 0:
            return exchange_start("L%d" % i, i, ["win", "wout", "w1", "w2"] if i % 2 == 0 else ["wp", "w1", "w2"], Pi)
        keys = ["win", "wout"]
        R2Q = _rs2("rs2_L0", pair_sums("L0", keys, Pi))
        finish_layer(0, keys, R2Q[len(keys):], R2Q[:len(keys)])
        exchange_finish("L0a", R2Q[0])
        return jnp.zeros((), F32)

    loss_blk, dx, P, small = _local_step(x[0], loss_target[0], W, pre_fwd, post_bwd, mid_fwd, mid_bwd)
    loss = lax.psum(loss_blk[0, 0], ("x", "y", "c"))
    res = {k: tuple(a.reshape(s.shape) for a in res[k])
           for k, s in dict(win=ssd_in_proj, wout=ssd_out_proj, wp=pool_w, w1=mlp_w1, w2=mlp_w2).items()}

    flat8 = jnp.concatenate([small[n] for n in SMALL_NAMES], axis=1)
    ns = flat8.shape[1]
    flat8 = jnp.pad(flat8, ((0, 0), (0, -ns % (8 * LANES))))
    mine8 = _rowsum8("small_rowsum", flat8).reshape(8, -1)
    tot = _sum_devices("small_sum", _allgather_small("small_allgather", mine8)).reshape(-1)
    sw = [ssd_conv_w, ssd_conv_b, ssd_dt_bias, ssd_A_log, ssd_D, ssd_norm_w, pool_b, pool_scale,
          ln_mix_g, ln_mix_b, ln_ffn_g, ln_ffn_b]
    sm = [m_ssd_conv_w, m_ssd_conv_b, m_ssd_dt_bias, m_ssd_A_log, m_ssd_D, m_ssd_norm_w, m_pool_b, m_pool_scale,
          m_ln_mix_g, m_ln_mix_b, m_ln_ffn_g, m_ln_ffn_b]
    sv = [v_ssd_conv_w, v_ssd_conv_b, v_ssd_dt_bias, v_ssd_A_log, v_ssd_D, v_ssd_norm_w, v_pool_b, v_pool_scale,
          v_ln_mix_g, v_ln_mix_b, v_ln_ffn_g, v_ln_ffn_b]
    full_shapes = [(nssd, CONV_WIDTH, 1, DI + 2 * N_GROUPS * D_STATE)] + [w.shape for w in sw[1:6]] \
        + [(npool, ng, dg), (npool, D)] + [w.shape for w in sw[8:]]
    sg = _unpack(tot, full_shapes)
    sg[0] = lax.dynamic_slice_in_dim(sg[0], chip * sw[0].shape[3], sw[0].shape[3], axis=3)
    sg[6] = lax.dynamic_slice_in_dim(sg[6], chip * sw[6].shape[2], sw[6].shape[2], axis=2)
    sg[7] = lax.dynamic_slice_in_dim(sg[7], chip * sw[7].shape[1], sw[7].shape[1], axis=1)
    packs = [_pack8(parts)[0] for parts in (sw, sg, sm, sv)]
    sd, smn, svn = _adam("adam_small", *packs)
    shapes = [w.shape for w in sw]
    sd, smn, svn = (_unpack(a.reshape(-1), shapes) for a in (sd, smn, svn))

    order = ["win", 0, 1, 2, 3, 4, 5, "wout", "wp", 6, 7, "w1", "w2", 8, 9, 10, 11]
    outs = [loss, dx.reshape(x.shape)]
    for slot, small_vals in ((0, sg), (1, sd), (2, smn), (3, svn)):
        for o in order:
            outs.append(res[o][slot] if isinstance(o, str) else small_vals[o])
    return tuple(outs)
```
